```python
import jax, jax.numpy as jnp
from jax import lax
import numpy as np


D_MODEL = 1024
BATCH = 8
SEQ = 4096
DEPTH = 1

CTX_LEN = 256
GRID_W = 64
D_RNN = 1024
N_LRU_BLOCKS = 16
LRU_BLOCK = D_RNN // N_LRU_BLOCKS
CONV_W = 4
LRU_C = 8.0
N_HEADS = 16
HEAD_DIM = 64
D_NA = N_HEADS * HEAD_DIM
WIN_R = 8
WIN_C = 16
QBLK_C = 16
KBLK_C = 32
ROPE_BASE = 10000.0
N_EXPERTS = 16
D_EXPERT = 2048
EC_CAPACITY = 2
IN_SIZES = (D_RNN, D_RNN, D_NA, D_NA, D_NA, D_MODEL, D_MODEL)
N_IN = sum(IN_SIZES)
EPS = 1e-6
NEG_INF = -1e30

kernel_name = 'hybrid_rglru_natten_ecmoe_dit_layer'


def rms_norm(x):
    xf = x.astype(jnp.float32)
    return (xf * lax.rsqrt(jnp.mean(xf * xf, axis=-1, keepdims=True) + EPS)).astype(x.dtype)


def modulate(x, shift, scale):
    return x * (1 + scale) + shift


def split_in(z):
    offsets = [int(o) for o in np.cumsum(IN_SIZES)[:-1]]
    return jnp.split(z, offsets, axis=-1)


def centred_dwconv(x, w, b):
    n = x.shape[1]
    left = CONV_W // 2
    xp = jnp.pad(x, ((0, 0), (left, CONV_W - 1 - left), (0, 0)))
    out = b.astype(x.dtype)
    for k in range(CONV_W):
        out = out + xp[:, k:k + n] * w[k]
    return out


def rglru_coeffs(xc, wa, ba, wi, bi, lam):
    B, N, _ = xc.shape
    f32 = jnp.float32
    xf = xc.astype(f32)
    xb = xf.reshape(B, N, N_LRU_BLOCKS, LRU_BLOCK)
    r = jax.nn.sigmoid(jnp.einsum('bngi,gij->bngj', xb, wa.astype(f32)).reshape(B, N, D_RNN) + ba.astype(f32))
    i = jax.nn.sigmoid(jnp.einsum('bngi,gij->bngj', xb, wi.astype(f32)).reshape(B, N, D_RNN) + bi.astype(f32))
    log_a = -LRU_C * r * jax.nn.softplus(-lam.astype(f32))
    a = jnp.exp(log_a)
    b = jnp.sqrt(-jnp.expm1(2.0 * log_a)) * (i * xf)
    return a, b


def _lin_combine(e1, e2):
    a1, b1 = e1
    a2, b2 = e2
    return a1 * a2, a2 * b1 + b2


def linear_recurrence(a, b, h0, reverse):
    A, Bc = lax.associative_scan(_lin_combine, (a, b), axis=1, reverse=reverse)
    return A * h0[:, None, :] + Bc


def axial_rope_angles(n):
    t = jnp.arange(n)
    row = (t // GRID_W).astype(jnp.float32)
    col = (t % GRID_W).astype(jnp.float32)
    n_freq = HEAD_DIM // 4
    inv = ROPE_BASE ** (-jnp.arange(n_freq, dtype=jnp.float32) / n_freq)
    return row[:, None] * inv, col[:, None] * inv


def _rotate(x, ang):
    half = x.shape[-1] // 2
    x1, x2 = x[..., :half], x[..., half:]
    cos = jnp.cos(ang)[:, None, :].astype(x.dtype)
    sin = jnp.sin(ang)[:, None, :].astype(x.dtype)
    return jnp.concatenate([x1 * cos - x2 * sin, x1 * sin + x2 * cos], axis=-1)


def axial_rope(x, ang_r, ang_c):
    h = HEAD_DIM // 2
    return jnp.concatenate([_rotate(x[..., :h], ang_r), _rotate(x[..., h:], ang_c)], axis=-1)


def context_attention(q, k, v):
    s = jnp.einsum('bqhd,bkhd->bhqk', q, k).astype(jnp.float32) * (HEAD_DIM ** -0.5)
    p = jax.nn.softmax(s, axis=-1).astype(v.dtype)
    return jnp.einsum('bhqk,bkhd->bqhd', p, v)


def neighbourhood_attention(q, k, v, k_ctx, v_ctx, rpb):
    B, N, H, Dh = q.shape
    rows = N // GRID_W
    win_r = min(WIN_R, rows)
    n_cb = GRID_W // QBLK_C
    qcols = np.arange(GRID_W).reshape(n_cb, QBLK_C)
    cstart = np.clip(qcols - WIN_C // 2, 0, GRID_W - WIN_C)
    kstart = np.clip(np.arange(n_cb) * QBLK_C - WIN_C // 2, 0, GRID_W - KBLK_C)
    kcols = kstart[:, None] + np.arange(KBLK_C)
    col_ok = (kcols[:, None, :] >= cstart[:, :, None]) & (kcols[:, None, :] < cstart[:, :, None] + WIN_C)
    dc_idx = np.clip(kcols[:, None, :] - qcols[:, :, None], -(WIN_C - 1), WIN_C - 1) + WIN_C - 1
    rpb_c = rpb[:, :, dc_idx]
    scale = Dh ** -0.5
    qg = q.reshape(B, rows, GRID_W, H, Dh)
    kg = k.reshape(B, rows, GRID_W, H, Dh)
    vg = v.reshape(B, rows, GRID_W, H, Dh)
    n_loc = win_r * KBLK_C

    def one_row(r):
        rs = jnp.clip(r - win_r // 2, 0, rows - win_r)
        qr = lax.dynamic_index_in_dim(qg, r, axis=1, keepdims=False).reshape(B, n_cb, QBLK_C, H, Dh)
        kb = lax.dynamic_slice_in_dim(kg, rs, win_r, axis=1)[:, :, kcols]
        vb = lax.dynamic_slice_in_dim(vg, rs, win_r, axis=1)[:, :, kcols]
        s_loc = jnp.einsum('bjqhd,brjkhd->bhjqrk', qr, kb).astype(jnp.float32) * scale
        dr_idx = rs + jnp.arange(win_r) - r + WIN_R - 1
        bias = jnp.take(rpb_c, dr_idx, axis=1).transpose(0, 2, 3, 1, 4).astype(jnp.float32)
        s_loc = jnp.where(col_ok[:, :, None, :], s_loc + bias, NEG_INF)
        s_loc = s_loc.reshape(B, H, n_cb, QBLK_C, n_loc)
        s_ctx = jnp.einsum('bjqhd,bkhd->bhjqk', qr, k_ctx).astype(jnp.float32) * scale
        p = jax.nn.softmax(jnp.concatenate([s_loc, s_ctx], axis=-1), axis=-1).astype(v.dtype)
        p_loc = p[..., :n_loc].reshape(B, H, n_cb, QBLK_C, win_r, KBLK_C)
        o = (jnp.einsum('bhjqrk,brjkhd->bjqhd', p_loc, vb)
             + jnp.einsum('bhjqk,bkhd->bjqhd', p[..., n_loc:], v_ctx))
        return o.reshape(B, GRID_W, H, Dh)

    out = lax.map(one_row, jnp.arange(rows))
    return out.transpose(1, 0, 2, 3, 4).reshape(B, N, H * Dh)


def expert_choice_moe(xn, w_router, w_gate, w_up, w_down):
    B, N, _ = xn.shape
    cap = EC_CAPACITY * N // N_EXPERTS
    logits = jnp.einsum('bnd,de->bne', xn, w_router).astype(jnp.float32)
    aff = jax.nn.softmax(logits, axis=-1)
    g, idx = lax.top_k(aff.transpose(0, 2, 1), cap)
    bidx = jnp.arange(B)[:, None, None]
    xe = xn[bidx, idx]
    hid = jax.nn.silu(jnp.einsum('becd,edf->becf', xe, w_gate)) * jnp.einsum('becd,edf->becf', xe, w_up)
    ye = jnp.einsum('becf,efd->becd', hid, w_down) * g[..., None].astype(xn.dtype)
    return jnp.zeros_like(xn).at[bidx, idx].add(ye)


def setup_inputs(seed: int = 0) -> dict:
    key = jax.random.key(seed)
    ks = jax.random.split(key, 24)
    f32 = jnp.float32

    def nrm(k, shape, fan_in):
        return jax.random.normal(k, shape, f32) * (fan_in ** -0.5)

    def small(k, shape):
        return 0.02 * jax.random.normal(k, shape, f32)

    u = jax.random.uniform(ks[14], (DEPTH, 2, D_RNN), f32, minval=0.9, maxval=0.999)
    p = u ** (1.0 / LRU_C)
    lam = jnp.log(p) - jnp.log1p(-p)
    return {
        'x': jax.random.normal(ks[0], (BATCH, SEQ, D_MODEL), f32),
        'c': jax.random.normal(ks[1], (BATCH, D_MODEL), f32),
        'ctx': jax.random.normal(ks[2], (BATCH, CTX_LEN, D_MODEL), f32),
        'c_ctx': jax.random.normal(ks[3], (D_MODEL,), f32),
        'w_mod': nrm(ks[4], (DEPTH, D_MODEL, 6 * D_MODEL), D_MODEL),
        'b_mod': small(ks[5], (DEPTH, 6 * D_MODEL)),
        'w_in': nrm(ks[6], (DEPTH, D_MODEL, N_IN), D_MODEL),
        'b_in': small(ks[7], (DEPTH, N_IN)),
        'conv_w': nrm(ks[8], (DEPTH, CONV_W, D_RNN), CONV_W),
        'conv_b': small(ks[9], (DEPTH, D_RNN)),
        'lru_wa': nrm(ks[10], (DEPTH, 2, N_LRU_BLOCKS, LRU_BLOCK, LRU_BLOCK), LRU_BLOCK),
        'lru_ba': small(ks[11], (DEPTH, 2, D_RNN)),
        'lru_wi': nrm(ks[12], (DEPTH, 2, N_LRU_BLOCKS, LRU_BLOCK, LRU_BLOCK), LRU_BLOCK),
        'lru_bi': small(ks[13], (DEPTH, 2, D_RNN)),
        'lru_lambda': lam,
        'na_rpb': 0.1 * jax.random.normal(ks[15], (DEPTH, N_HEADS, 2 * WIN_R - 1, 2 * WIN_C - 1), f32),
        'w_proj_rnn': nrm(ks[16], (DEPTH, D_RNN, D_MODEL), D_RNN),
        'w_proj_na': nrm(ks[17], (DEPTH, D_NA, D_MODEL), D_NA),
        'w_out': nrm(ks[18], (DEPTH, D_MODEL, D_MODEL), D_MODEL),
        'w_router': nrm(ks[19], (DEPTH, D_MODEL, N_EXPERTS), D_MODEL),
        'w_exp_gate': nrm(ks[20], (DEPTH, N_EXPERTS, D_MODEL, D_EXPERT), D_MODEL),
        'w_exp_up': nrm(ks[21], (DEPTH, N_EXPERTS, D_MODEL, D_EXPERT), D_MODEL),
        'w_exp_down': nrm(ks[22], (DEPTH, N_EXPERTS, D_EXPERT, D_MODEL), D_EXPERT),
        'final_norm': 1.0 + small(ks[23], (D_MODEL,)),
    }


def reference(x, c, ctx, c_ctx, w_mod, b_mod, w_in, b_in, conv_w, conv_b,
              lru_wa, lru_ba, lru_wi, lru_bi, lru_lambda, na_rpb,
              w_proj_rnn, w_proj_na, w_out, w_router, w_exp_gate, w_exp_up, w_exp_down,
              final_norm):
    B, N, _ = x.shape
    L = ctx.shape[1]
    ang_r, ang_c = axial_rope_angles(N)
    h_ctx = ctx
    for l in range(DEPTH):
        last = l == DEPTH - 1
        mod = (jax.nn.silu(c) @ w_mod[l] + b_mod[l])[:, None, :]
        sh1, sc1, g1, sh2, sc2, g2 = jnp.split(mod, 6, axis=-1)
        modc = (jax.nn.silu(c_ctx) @ w_mod[l] + b_mod[l])[None, None, :]
        csh1, csc1, cg1, csh2, csc2, cg2 = jnp.split(modc, 6, axis=-1)

        xn = modulate(rms_norm(x), sh1, sc1)
        cn = modulate(rms_norm(h_ctx), csh1, csc1)
        yg, xr, q, k, v, gr, gn = split_in(xn @ w_in[l] + b_in[l])
        ycg, xrc, qc, kc, vc, grc, gnc = split_in(cn @ w_in[l] + b_in[l])

        xr = centred_dwconv(xr, conv_w[l], conv_b[l])
        xrc = centred_dwconv(xrc, conv_w[l], conv_b[l])
        h_lat = None
        ctx_states = []
        for d in range(2):
            rev = d == 1
            pa = (lru_wa[l, d], lru_ba[l, d], lru_wi[l, d], lru_bi[l, d], lru_lambda[l, d])
            ac, bc = rglru_coeffs(xrc, *pa)
            hc = linear_recurrence(ac, bc, jnp.zeros_like(ac[:, 0]), rev)
            h_end = hc[:, 0] if rev else hc[:, -1]
            a, b = rglru_coeffs(xr, *pa)
            hd = linear_recurrence(a, b, h_end, rev)
            h_lat = hd if h_lat is None else h_lat + hd
            ctx_states.append(hc)
        y_rnn = h_lat.astype(x.dtype) * jax.nn.gelu(yg)

        q4 = axial_rope(q.reshape(B, N, N_HEADS, HEAD_DIM), ang_r, ang_c)
        k4 = axial_rope(k.reshape(B, N, N_HEADS, HEAD_DIM), ang_r, ang_c)
        v4 = v.reshape(B, N, N_HEADS, HEAD_DIM)
        kc4 = kc.reshape(B, L, N_HEADS, HEAD_DIM)
        vc4 = vc.reshape(B, L, N_HEADS, HEAD_DIM)
        y_na = neighbourhood_attention(q4, k4, v4, kc4, vc4, na_rpb[l])

        mix = jax.nn.sigmoid(gr) * (y_rnn @ w_proj_rnn[l]) + jax.nn.sigmoid(gn) * (y_na @ w_proj_na[l])
        x = x + g1 * (mix @ w_out[l])
        if not last:
            yc_rnn = (ctx_states[0] + ctx_states[1]).astype(x.dtype) * jax.nn.gelu(ycg)
            yc_na = context_attention(qc.reshape(B, L, N_HEADS, HEAD_DIM), kc4, vc4).reshape(B, L, D_NA)
            mixc = (jax.nn.sigmoid(grc) * (yc_rnn @ w_proj_rnn[l])
                    + jax.nn.sigmoid(gnc) * (yc_na @ w_proj_na[l]))
            h_ctx = h_ctx + cg1 * (mixc @ w_out[l])

        xn2 = modulate(rms_norm(x), sh2, sc2)
        x = x + g2 * expert_choice_moe(xn2, w_router[l], w_exp_gate[l], w_exp_up[l], w_exp_down[l])
        if not last:
            cn2 = modulate(rms_norm(h_ctx), csh2, csc2)
            h_ctx = h_ctx + cg2 * expert_choice_moe(cn2, w_router[l], w_exp_gate[l], w_exp_up[l], w_exp_down[l])
    return rms_norm(x) * final_norm
```

```python
import functools

import numpy as np
import jax
import jax.numpy as jnp
from jax import lax
from jax.experimental import pallas as pl
from jax.experimental.pallas import tpu as pltpu

F32 = jnp.float32
BF16 = jnp.bfloat16

GRID_W = 64
N_HEADS = 16
HEAD_DIM = 64
N_LRU_BLOCKS = 16
CONV_W = 4
LRU_C = 8.0
WIN_R = 8
WIN_C = 16
ROPE_BASE = 10000.0
N_EXPERTS = 16
EC_CAPACITY = 2
EPS = 1e-6
NEG_INF = -1e30

LANES = 128
SUBLANES = 8
VMEM_LIMIT = 56 * 1024 * 1024

TQ = 64
LRU_SLAB = 256
FFN_TM = 2048
FFN_TF = 512


def _cparams(sem):
    return pltpu.CompilerParams(dimension_semantics=sem, vmem_limit_bytes=VMEM_LIMIT)


def _rms(x):
    return x * lax.rsqrt(jnp.mean(x * x, axis=-1, keepdims=True) + EPS)


def _mod_kernel(c_ref, w_ref, b_ref, o_ref):
    c = c_ref[...]
    s = c * jax.nn.sigmoid(c)
    o_ref[...] = jnp.dot(s, w_ref[...], preferred_element_type=F32) + b_ref[...]


def _mod_call(cc, w_mod, b_mod):
    rows, d = cc.shape
    n_out = w_mod.shape[1]
    tn = 1024
    return pl.pallas_call(
        _mod_kernel,
        grid=(n_out // tn,),
        in_specs=[
            pl.BlockSpec((rows, d), lambda j: (0, 0)),
            pl.BlockSpec((d, tn), lambda j: (0, j)),
            pl.BlockSpec((1, tn), lambda j: (0, j)),
        ],
        out_specs=pl.BlockSpec((rows, tn), lambda j: (0, j)),
        out_shape=jax.ShapeDtypeStruct((rows, n_out), F32),
        compiler_params=_cparams(("arbitrary",)),
        name="adaln_mod",
    )(cc, w_mod, b_mod.reshape(1, n_out))


def _rope(z, cos, sa, sb):
    outs = []
    for g in range(z.shape[1] // LANES):
        zg = z[:, g * LANES:(g + 1) * LANES]
        outs.append(zg * cos + pltpu.roll(zg, 16, 1) * sa + pltpu.roll(zg, LANES - 16, 1) * sb)
    return jnp.concatenate(outs, axis=1)


def _inproj_kernel(*refs, kinds):
    n_out = len(kinds)
    x_ref, sh_ref, sc_ref, w_ref, b_ref, cos_ref, sa_ref, sb_ref = refs[:8]
    out_refs = refs[8:8 + n_out]
    nb, tq, d = x_ref.shape
    rows = nb * tq
    xn = _rms(x_ref[...]) * (1.0 + sc_ref[...]) + sh_ref[...]
    xb = xn.reshape(rows, d).astype(BF16)

    def tiled(t_ref):
        return jnp.broadcast_to(t_ref[...][None], (nb, tq, LANES)).reshape(rows, LANES)

    for g, (kind, o_ref) in enumerate(zip(kinds, out_refs)):
        z = jnp.dot(xb, w_ref[:, g * d:(g + 1) * d], preferred_element_type=F32) + b_ref[:, g * d:(g + 1) * d]
        if kind == "gelu":
            o_ref[...] = jax.nn.gelu(z, approximate=True).astype(BF16).reshape(nb, tq, d)
        elif kind == "tok_major":
            o_ref[...] = jnp.swapaxes(z.reshape(nb, tq, d), 0, 1)
        elif kind == "rope_q":
            r = _rope(z * (HEAD_DIM ** -0.5), tiled(cos_ref), tiled(sa_ref), tiled(sb_ref))
            o_ref[...] = r.astype(BF16).reshape(nb, tq, d)
        elif kind == "rope_k":
            r = _rope(z, tiled(cos_ref), tiled(sa_ref), tiled(sb_ref))
            o_ref[...] = r.astype(BF16).reshape(nb, tq, d)
        elif kind == "bf16":
            o_ref[...] = z.astype(BF16).reshape(nb, tq, d)
        elif kind == "sigmoid":
            o_ref[...] = jax.nn.sigmoid(z).astype(BF16).reshape(nb, tq, d)
        else:
            raise ValueError(kind)


def _inproj_call(x, sh, sc, w, b, tables, kinds, name):
    nb, n, d = x.shape
    g = len(kinds)
    assert w.shape == (d, g * d)
    cos, sa, sb = tables
    bat = lambda i: (0, i, 0)
    tokm = lambda i: (i, 0, 0)
    const3 = lambda i: (0, 0, 0)
    const2 = lambda i: (0, 0)
    out_shapes, out_specs = [], []
    for kind in kinds:
        if kind == "tok_major":
            out_shapes.append(jax.ShapeDtypeStruct((n, nb, d), F32))
            out_specs.append(pl.BlockSpec((TQ, nb, d), tokm))
        else:
            out_shapes.append(jax.ShapeDtypeStruct((nb, n, d), BF16))
            out_specs.append(pl.BlockSpec((nb, TQ, d), bat))
    rows_per_table = cos.shape[0] // TQ
    return pl.pallas_call(
        functools.partial(_inproj_kernel, kinds=kinds),
        grid=(n // TQ,),
        in_specs=[
            pl.BlockSpec((nb, TQ, d), bat),
            pl.BlockSpec(sh.shape, const3),
            pl.BlockSpec(sc.shape, const3),
            pl.BlockSpec(w.shape, const2, pipeline_mode=pl.Buffered(1)),
            pl.BlockSpec(b.shape, const2),
            pl.BlockSpec((TQ, LANES), lambda i: (i % rows_per_table, 0)),
            pl.BlockSpec((TQ, LANES), lambda i: (i % rows_per_table, 0)),
            pl.BlockSpec((TQ, LANES), lambda i: (i % rows_per_table, 0)),
        ],
        out_specs=out_specs,
        out_shape=out_shapes,
        compiler_params=_cparams(("arbitrary",)),
        name=name,
    )(x, sh, sc, w, b, cos, sa, sb)


def _rope_tables(n):
    t = jnp.arange(n)
    row = (t // GRID_W).astype(F32)
    col = (t % GRID_W).astype(F32)
    n_freq = HEAD_DIM // 4
    inv = ROPE_BASE ** (-jnp.arange(n_freq, dtype=F32) / n_freq)
    ang_r = row[:, None] * inv
    ang_c = col[:, None] * inv
    zero = jnp.zeros_like(ang_r)
    cos_h = jnp.concatenate([jnp.cos(ang_r), jnp.cos(ang_r), jnp.cos(ang_c), jnp.cos(ang_c)], axis=1)
    sa_h = jnp.concatenate([zero, jnp.sin(ang_r), zero, jnp.sin(ang_c)], axis=1)
    sb_h = jnp.concatenate([-jnp.sin(ang_r), zero, -jnp.sin(ang_c), zero], axis=1)
    rep = LANES // HEAD_DIM
    return tuple(jnp.tile(a, (1, rep)) for a in (cos_h, sa_h, sb_h))


def _lru_kernel(*refs, tn, nt, reverse, fuse_out):
    if fuse_out:
        (xp_ref, xc_ref, xn_ref, cw_ref, cb_ref, wg_ref, bg_ref, lam_ref, h0_ref, hf_ref, gy_ref,
         out_ref, hfin_ref, a_s, b_s, hcar_s) = refs
    else:
        (xp_ref, xc_ref, xn_ref, cw_ref, cb_ref, wg_ref, bg_ref, lam_ref, h0_ref,
         out_ref, hfin_ref, a_s, b_s, hcar_s) = refs
    i = pl.program_id(0)
    ti = (nt - 1 - i) if reverse else i
    nb, d = hcar_s.shape

    @pl.when(i == 0)
    def _():
        hcar_s[...] = h0_ref[...]

    x = xc_ref[...]
    xp = jnp.where(ti == 0, 0.0, xp_ref[...])
    xnx = jnp.where(ti == nt - 1, 0.0, xn_ref[...])
    ext = jnp.concatenate([xp, x, xnx], axis=0)
    xc = cb_ref[...]
    for k in range(CONV_W):
        xc = xc + ext[k:k + tn] * cw_ref[k]
    rows = tn * nb
    xc2 = xc.reshape(rows, d)
    for s in range(d // LRU_SLAB):
        sl = slice(s * LRU_SLAB, (s + 1) * LRU_SLAB)
        xs = xc2[:, sl]
        gates = jnp.dot(xs.astype(BF16), wg_ref[s], preferred_element_type=F32) + bg_ref[s]
        r = jax.nn.sigmoid(gates[:, :LRU_SLAB])
        ig = jax.nn.sigmoid(gates[:, LRU_SLAB:])
        z = -lam_ref[s]
        softplus = jnp.maximum(z, 0.0) + jnp.log1p(jnp.exp(-jnp.abs(z)))
        log_a = r * (-LRU_C * softplus)
        a = jnp.exp(log_a)
        b = jnp.sqrt(1.0 - a * a) * (ig * xs)
        a_s[:, :, sl] = a.reshape(tn, nb, LRU_SLAB)
        b_s[:, :, sl] = b.reshape(tn, nb, LRU_SLAB)

    unroll = 8

    def step(j, h):
        for k in range(unroll):
            jj = j * unroll + k
            t = (tn - 1 - jj) if reverse else jj
            h = a_s[t] * h + b_s[t]
            a_s[t] = h
        return h

    h = lax.fori_loop(0, tn // unroll, step, hcar_s[...])
    hcar_s[...] = h
    hfin_ref[...] = h
    if fuse_out:
        hs = jnp.swapaxes(hf_ref[...] + a_s[...], 0, 1)
        out_ref[...] = (hs * gy_ref[...].astype(F32)).astype(BF16)
    else:
        out_ref[...] = a_s[...]


def _lru_call(xr_t, conv_w, conv_b, wg, bg, lam, h0, hf=None, gy=None, *, reverse, name):
    n, nb, d = xr_t.shape
    tn = TQ
    nt = n // tn
    fuse_out = hf is not None
    tile = (lambda i: nt - 1 - i) if reverse else (lambda i: i)
    n_slab = d // LRU_SLAB
    in_specs = [
        pl.BlockSpec((2, nb, d), lambda i: (jnp.maximum(tile(i) * (tn // 2) - 1, 0), 0, 0)),
        pl.BlockSpec((tn, nb, d), lambda i: (tile(i), 0, 0)),
        pl.BlockSpec((1, nb, d), lambda i: (jnp.minimum((tile(i) + 1) * tn, n - 1), 0, 0)),
        pl.BlockSpec((CONV_W, 1, d), lambda i: (0, 0, 0)),
        pl.BlockSpec((1, d), lambda i: (0, 0)),
        pl.BlockSpec((n_slab, LRU_SLAB, 2 * LRU_SLAB), lambda i: (0, 0, 0)),
        pl.BlockSpec((n_slab, 1, 2 * LRU_SLAB), lambda i: (0, 0, 0)),
        pl.BlockSpec((n_slab, 1, LRU_SLAB), lambda i: (0, 0, 0)),
        pl.BlockSpec((nb, d), lambda i: (0, 0)),
    ]
    args = [xr_t, xr_t, xr_t, conv_w.reshape(CONV_W, 1, d), conv_b.reshape(1, d), wg, bg, lam, h0]
    if fuse_out:
        in_specs += [
            pl.BlockSpec((tn, nb, d), lambda i: (tile(i), 0, 0)),
            pl.BlockSpec((nb, tn, d), lambda i: (0, tile(i), 0)),
        ]
        args += [hf, gy]
        out_shape0 = jax.ShapeDtypeStruct((nb, n, d), BF16)
        out_spec0 = pl.BlockSpec((nb, tn, d), lambda i: (0, tile(i), 0))
    else:
        out_shape0 = jax.ShapeDtypeStruct((n, nb, d), F32)
        out_spec0 = pl.BlockSpec((tn, nb, d), lambda i: (tile(i), 0, 0))
    return pl.pallas_call(
        functools.partial(_lru_kernel, tn=tn, nt=nt, reverse=reverse, fuse_out=fuse_out),
        grid=(nt,),
        in_specs=in_specs,
        out_specs=[out_spec0, pl.BlockSpec((nb, d), lambda i: (0, 0))],
        out_shape=[out_shape0, jax.ShapeDtypeStruct((nb, d), F32)],
        scratch_shapes=[
            pltpu.VMEM((tn, nb, d), F32),
            pltpu.VMEM((tn, nb, d), F32),
            pltpu.VMEM((nb, d), F32),
        ],
        compiler_params=_cparams(("arbitrary",)),
        name=name,
    )(*args)


def _lru_gate_weights(wa, ba, wi, bi, lam):
    d = wa.shape[0] * wa.shape[1]
    n_slab = d // LRU_SLAB
    eye = jnp.eye(wa.shape[0], dtype=F32)

    def dense_slabs(w):
        full = jnp.einsum("gij,gh->gihj", w, eye).reshape(d, d)
        return jnp.stack([full[s * LRU_SLAB:(s + 1) * LRU_SLAB, s * LRU_SLAB:(s + 1) * LRU_SLAB]
                          for s in range(n_slab)])

    wg = jnp.concatenate([dense_slabs(wa), dense_slabs(wi)], axis=2).astype(BF16)
    bg = jnp.concatenate([ba.reshape(n_slab, 1, LRU_SLAB), bi.reshape(n_slab, 1, LRU_SLAB)], axis=2)
    return wg, bg, lam.reshape(n_slab, 1, LRU_SLAB)


def _na_kernel(q_ref, k_ref, v_ref, kc_ref, vc_ref, tab_ref, o_ref, *, rows):
    win = WIN_R * GRID_W
    kc = kc_ref[0]
    vc = vc_ref[0]
    lane = lax.broadcasted_iota(jnp.int32, (GRID_W, LANES), 1)
    first_head = lane < HEAD_DIM
    nt_dims = (((1,), (1,)), ((), ()))

    def row_body(r, carry):
        rs = jnp.clip(r - WIN_R // 2, 0, rows - WIN_R)
        q = q_ref[0, pl.ds(pl.multiple_of(r * GRID_W, GRID_W), GRID_W), :]
        zero = jnp.zeros_like(q)
        qs = jnp.concatenate([jnp.where(first_head, q, zero), jnp.where(first_head, zero, q)], axis=0)
        k0 = pl.multiple_of(rs * GRID_W, GRID_W)
        kl = k_ref[0, pl.ds(k0, win), :]
        vl = v_ref[0, pl.ds(k0, win), :]
        s = lax.dot_general(qs, kl, nt_dims, preferred_element_type=F32)
        d0 = rs - r + (WIN_R - 1)
        bias = jnp.concatenate([tab_ref[0, d0 + 2 * m] for m in range(WIN_R // 2)], axis=1)
        s = s + bias
        sc = lax.dot_general(qs, kc, nt_dims, preferred_element_type=F32)
        m = jnp.maximum(jnp.max(s, axis=1, keepdims=True), jnp.max(sc, axis=1, keepdims=True))
        p = jnp.exp(s - m)
        pc = jnp.exp(sc - m)
        den = jnp.sum(p, axis=1, keepdims=True) + jnp.sum(pc, axis=1, keepdims=True)
        o = (jnp.dot(p.astype(BF16), vl, preferred_element_type=F32)
             + jnp.dot(pc.astype(BF16), vc, preferred_element_type=F32))
        o = o / den
        out = jnp.where(first_head, o[:GRID_W], o[GRID_W:])
        o_ref[0, pl.ds(pl.multiple_of(r * GRID_W, GRID_W), GRID_W), :] = out.astype(BF16)
        return carry

    lax.fori_loop(0, rows, row_body, 0)


def _na_call(q, k, v, kc, vc, tab):
    nb, n, d = q.shape
    l = kc.shape[1]
    rows = n // GRID_W
    n_pair = d // LANES
    blk = lambda b, p: (b, 0, p)
    return pl.pallas_call(
        functools.partial(_na_kernel, rows=rows),
        grid=(nb, n_pair),
        in_specs=[
            pl.BlockSpec((1, n, LANES), blk),
            pl.BlockSpec((1, n, LANES), blk),
            pl.BlockSpec((1, n, LANES), blk),
            pl.BlockSpec((1, l, LANES), blk),
            pl.BlockSpec((1, l, LANES), blk),
            pl.BlockSpec((1,) + tab.shape[1:], lambda b, p: (p, 0, 0, 0)),
        ],
        out_specs=pl.BlockSpec((1, n, LANES), blk),
        out_shape=jax.ShapeDtypeStruct((nb, n, d), BF16),
        compiler_params=_cparams(("arbitrary", "arbitrary")),
        name="neighbourhood_attention",
    )(q, k, v, kc, vc, tab)


def _na_bias_table(rpb):
    qc = np.arange(GRID_W)
    kc = np.arange(GRID_W)
    cstart = np.clip(qc - WIN_C // 2, 0, GRID_W - WIN_C)
    ok = (kc[None, :] >= cstart[:, None]) & (kc[None, :] < cstart[:, None] + WIN_C)
    dc = np.clip(kc[None, :] - qc[:, None], -(WIN_C - 1), WIN_C - 1) + WIN_C - 1
    tab = jnp.where(ok[None, None], rpb[:, :, dc], NEG_INF).astype(F32)
    pair = jnp.concatenate([tab[:, :-1], tab[:, 1:]], axis=-1)
    h = pair.shape[0]
    pair = pair.reshape(h // 2, 2, 2 * WIN_R - 2, GRID_W, 2 * GRID_W)
    return pair.transpose(0, 2, 1, 3, 4).reshape(h // 2, 2 * WIN_R - 2, 2 * GRID_W, 2 * GRID_W)


def _mix_kernel(yr_ref, yn_ref, gr_ref, gn_ref, x_ref, g1_ref, sh2_ref, sc2_ref,
                wpr_ref, wpn_ref, wo_ref, wr_ref, x1_ref, xn2_ref, lg_ref):
    nb, tq, d = x_ref.shape
    rows = nb * tq
    pr = jnp.dot(yr_ref[...].reshape(rows, d), wpr_ref[...], preferred_element_type=F32)
    pn = jnp.dot(yn_ref[...].reshape(rows, d), wpn_ref[...], preferred_element_type=F32)
    mix = gr_ref[...].reshape(rows, d).astype(F32) * pr + gn_ref[...].reshape(rows, d).astype(F32) * pn
    o = jnp.dot(mix.astype(BF16), wo_ref[...], preferred_element_type=F32)
    x1 = x_ref[...] + g1_ref[...] * o.reshape(nb, tq, d)
    x1_ref[...] = x1
    xb = (_rms(x1) * (1.0 + sc2_ref[...]) + sh2_ref[...]).astype(BF16)
    xn2_ref[...] = xb
    lg = jnp.dot(xb.reshape(rows, d), wr_ref[...], preferred_element_type=F32)
    lg_ref[...] = lg.reshape(nb, tq, LANES)


def _mix_call(yr, yn, gr, gn, x, g1, sh2, sc2, wpr, wpn, wo, wr):
    nb, n, d = x.shape
    bat = lambda i: (0, i, 0)
    const3 = lambda i: (0, 0, 0)
    const2 = lambda i: (0, 0)
    act = pl.BlockSpec((nb, TQ, d), bat)
    vec = pl.BlockSpec((nb, 1, d), const3)
    wsp = lambda w: pl.BlockSpec(w.shape, const2, pipeline_mode=pl.Buffered(1))
    return pl.pallas_call(
        _mix_kernel,
        grid=(n // TQ,),
        in_specs=[act, act, act, act, act, vec, vec, vec, wsp(wpr), wsp(wpn), wsp(wo), wsp(wr)],
        out_specs=[act, act, pl.BlockSpec((nb, TQ, LANES), bat)],
        out_shape=[
            jax.ShapeDtypeStruct((nb, n, d), F32),
            jax.ShapeDtypeStruct((nb, n, d), BF16),
            jax.ShapeDtypeStruct((nb, n, LANES), F32),
        ],
        compiler_params=_cparams(("arbitrary",)),
        name="merge_out_norm_router",
    )(yr, yn, gr, gn, x, g1, sh2, sc2, wpr, wpn, wo, wr)


def _ffn_kernel(xe_ref, g_ref, wg_ref, wu_ref, wd_ref, o_ref, acc_s):
    f = pl.program_id(2)
    xe = xe_ref[0]
    h1 = jnp.dot(xe, wg_ref[0].astype(BF16), preferred_element_type=F32)
    h2 = jnp.dot(xe, wu_ref[0].astype(BF16), preferred_element_type=F32)
    hid = (h1 * jax.nn.sigmoid(h1) * h2).astype(BF16)
    part = jnp.dot(hid, wd_ref[0].astype(BF16), preferred_element_type=F32)

    @pl.when(f == 0)
    def _():
        acc_s[...] = part

    @pl.when(f > 0)
    def _():
        acc_s[...] += part

    @pl.when(f == pl.num_programs(2) - 1)
    def _():
        o_ref[0] = (acc_s[...] * g_ref[0]).astype(BF16)


def _ffn_call(xe, g, w_gate, w_up, w_down):
    e, m, d = xe.shape
    dff = w_gate.shape[2]
    tm = min(FFN_TM, m)
    return pl.pallas_call(
        _ffn_kernel,
        grid=(e, m // tm, dff // FFN_TF),
        in_specs=[
            pl.BlockSpec((1, tm, d), lambda ei, mi, fi: (ei, mi, 0)),
            pl.BlockSpec((1, tm, 1), lambda ei, mi, fi: (ei, mi, 0)),
            pl.BlockSpec((1, d, FFN_TF), lambda ei, mi, fi: (ei, 0, fi)),
            pl.BlockSpec((1, d, FFN_TF), lambda ei, mi, fi: (ei, 0, fi)),
            pl.BlockSpec((1, FFN_TF, d), lambda ei, mi, fi: (ei, fi, 0)),
        ],
        out_specs=pl.BlockSpec((1, tm, d), lambda ei, mi, fi: (ei, mi, 0)),
        out_shape=jax.ShapeDtypeStruct((e, m, d), BF16),
        scratch_shapes=[pltpu.VMEM((tm, d), F32)],
        compiler_params=_cparams(("arbitrary", "arbitrary", "arbitrary")),
        name="expert_ffn",
    )(xe, g, w_gate, w_up, w_down)


def _final_kernel(x1_ref, moe_ref, g2_ref, fn_ref, o_ref):
    x = x1_ref[...] + g2_ref[...] * moe_ref[...]
    o_ref[...] = _rms(x) * fn_ref[...]


def _final_call(x1, moe, g2, fnorm):
    nb, n, d = x1.shape
    bat = lambda i: (0, i, 0)
    act = pl.BlockSpec((nb, TQ, d), bat)
    return pl.pallas_call(
        _final_kernel,
        grid=(n // TQ,),
        in_specs=[act, act, pl.BlockSpec((nb, 1, d), lambda i: (0, 0, 0)),
                  pl.BlockSpec((1, 1, d), lambda i: (0, 0, 0))],
        out_specs=act,
        out_shape=jax.ShapeDtypeStruct((nb, n, d), F32),
        compiler_params=_cparams(("arbitrary",)),
        name="final_residual_norm",
    )(x1, moe, g2, fnorm.reshape(1, 1, d))


def kernel(x, c, ctx, c_ctx, w_mod, b_mod, w_in, b_in, conv_w, conv_b, lru_wa, lru_ba, lru_wi, lru_bi,
           lru_lambda, na_rpb, w_proj_rnn, w_proj_na, w_out, w_router, w_exp_gate, w_exp_up, w_exp_down,
           final_norm):
    nb, n, d = x.shape
    l = ctx.shape[1]
    assert w_mod.shape[0] == 1, "single-layer problem"
    assert nb == SUBLANES and n % TQ == 0 and l % TQ == 0 and d % LRU_SLAB == 0
    lyr = 0

    cc = jnp.concatenate([c, c_ctx[None], jnp.zeros((2 * SUBLANES - nb - 1, d), F32)], axis=0)
    mod = _mod_call(cc, w_mod[lyr], b_mod[lyr])
    sh1, sc1, g1, sh2, sc2, g2 = [m[:nb, None, :] for m in jnp.split(mod, 6, axis=-1)]
    csh1, csc1 = [jnp.broadcast_to(m[nb:nb + 1, None, :], (nb, 1, d)) for m in jnp.split(mod, 6, axis=-1)[:2]]

    w_in_b = w_in[lyr].astype(BF16)
    b_in_r = b_in[lyr].reshape(1, -1)
    tables = _rope_tables(n)
    gy, xr_t, q, k, v, sgr, sgn = _inproj_call(
        x, sh1, sc1, w_in_b, b_in_r, tables,
        ("gelu", "tok_major", "rope_q", "rope_k", "bf16", "sigmoid", "sigmoid"), "in_proj_latent")
    ctx_cols = np.concatenate([np.arange(d, 2 * d), np.arange(3 * d, 5 * d)])
    xrc_t, kc, vc = _inproj_call(
        ctx, csh1, csc1, w_in_b[:, ctx_cols], b_in_r[:, ctx_cols], tables,
        ("tok_major", "bf16", "bf16"), "in_proj_context")

    gw = [_lru_gate_weights(lru_wa[lyr, dr], lru_ba[lyr, dr], lru_wi[lyr, dr], lru_bi[lyr, dr], lru_lambda[lyr, dr])
          for dr in range(2)]
    zeros_h = jnp.zeros((nb, d), F32)
    _, hc_f = _lru_call(xrc_t, conv_w[lyr], conv_b[lyr], *gw[0], zeros_h, reverse=False, name="rglru_ctx_fwd")
    _, hc_b = _lru_call(xrc_t, conv_w[lyr], conv_b[lyr], *gw[1], zeros_h, reverse=True, name="rglru_ctx_bwd")
    hf_t, _ = _lru_call(xr_t, conv_w[lyr], conv_b[lyr], *gw[0], hc_f, reverse=False, name="rglru_fwd")
    y_rnn, _ = _lru_call(xr_t, conv_w[lyr], conv_b[lyr], *gw[1], hc_b, hf_t, gy, reverse=True, name="rglru_bwd")

    y_na = _na_call(q, k, v, kc, vc, _na_bias_table(na_rpb[lyr]))

    wr = jnp.zeros((d, LANES), BF16).at[:, :N_EXPERTS].set(w_router[lyr].astype(BF16))
    x1, xn2, logits = _mix_call(y_rnn, y_na, sgr, sgn, x, g1, sh2, sc2,
                                w_proj_rnn[lyr].astype(BF16), w_proj_na[lyr].astype(BF16),
                                w_out[lyr].astype(BF16), wr)

    cap = EC_CAPACITY * n // N_EXPERTS
    aff = jax.nn.softmax(logits[..., :N_EXPERTS], axis=-1)
    gsel, idx = lax.top_k(aff.transpose(0, 2, 1), cap)
    bidx = jnp.arange(nb)[:, None, None]
    xe = xn2[bidx, idx].transpose(1, 0, 2, 3).reshape(N_EXPERTS, nb * cap, d)
    ge = gsel.transpose(1, 0, 2).reshape(N_EXPERTS, nb * cap, 1)
    ye = _ffn_call(xe, ge, w_exp_gate[lyr], w_exp_up[lyr], w_exp_down[lyr])
    ye = ye.reshape(N_EXPERTS, nb, cap, d).transpose(1, 0, 2, 3).astype(F32)
    moe = jnp.zeros((nb, n, d), F32).at[bidx, idx].add(ye)

    return _final_call(x1, moe, g2, final_norm)
```

```python
import functools

import numpy as np
import jax
import jax.numpy as jnp
from jax import lax
from jax.experimental import pallas as pl
from jax.experimental.pallas import tpu as pltpu

F32 = jnp.float32
BF16 = jnp.bfloat16

GRID_W = 64
N_HEADS = 16
HEAD_DIM = 64
N_LRU_BLOCKS = 16
CONV_W = 4
LRU_C = 8.0
WIN_R = 8
WIN_C = 16
ROPE_BASE = 10000.0
N_EXPERTS = 16
EC_CAPACITY = 2
EPS = 1e-6
NEG_INF = -1e30

LANES = 128
SUBLANES = 8
VMEM_LIMIT = 56 * 1024 * 1024

TQ = 64
LRU_SLAB = 256
FFN_TM = 2048
FFN_TF = 512
NA_CHUNK = 2


def _cparams(sem):
    return pltpu.CompilerParams(dimension_semantics=sem, vmem_limit_bytes=VMEM_LIMIT)


def _rms(x):
    return x * lax.rsqrt(jnp.mean(x * x, axis=-1, keepdims=True) + EPS)


def _mod_kernel(c_ref, w_ref, b_ref, o_ref):
    c = c_ref[...]
    s = c * jax.nn.sigmoid(c)
    o_ref[...] = jnp.dot(s, w_ref[...], preferred_element_type=F32) + b_ref[...]


def _mod_call(cc, w_mod, b_mod):
    rows, d = cc.shape
    n_out = w_mod.shape[1]
    tn = 1024
    return pl.pallas_call(
        _mod_kernel,
        grid=(n_out // tn,),
        in_specs=[
            pl.BlockSpec((rows, d), lambda j: (0, 0)),
            pl.BlockSpec((d, tn), lambda j: (0, j)),
            pl.BlockSpec((1, tn), lambda j: (0, j)),
        ],
        out_specs=pl.BlockSpec((rows, tn), lambda j: (0, j)),
        out_shape=jax.ShapeDtypeStruct((rows, n_out), F32),
        compiler_params=_cparams(("arbitrary",)),
        name="adaln_mod",
    )(cc, w_mod, b_mod.reshape(1, n_out))


def _rope(z, cos, sa, sb):
    outs = []
    for g in range(z.shape[1] // LANES):
        zg = z[:, g * LANES:(g + 1) * LANES]
        outs.append(zg * cos + pltpu.roll(zg, 16, 1) * sa + pltpu.roll(zg, LANES - 16, 1) * sb)
    return jnp.concatenate(outs, axis=1)


def _inproj_kernel(*refs, kinds):
    n_out = len(kinds)
    x_ref, sh_ref, sc_ref, w_ref, b_ref, cos_ref, sa_ref, sb_ref = refs[:8]
    out_refs = refs[8:8 + n_out]
    nb, tq, d = x_ref.shape
    rows = nb * tq
    xn = _rms(x_ref[...]) * (1.0 + sc_ref[...]) + sh_ref[...]
    xb = xn.reshape(rows, d).astype(BF16)

    def tiled(t_ref):
        return jnp.broadcast_to(t_ref[...][None], (nb, tq, LANES)).reshape(rows, LANES)

    for g, (kind, o_ref) in enumerate(zip(kinds, out_refs)):
        z = jnp.dot(xb, w_ref[:, g * d:(g + 1) * d], preferred_element_type=F32) + b_ref[:, g * d:(g + 1) * d]
        if kind == "gelu":
            o_ref[...] = jax.nn.gelu(z, approximate=True).astype(BF16).reshape(nb, tq, d)
        elif kind == "tok_major":
            o_ref[...] = jnp.swapaxes(z.reshape(nb, tq, d), 0, 1)
        elif kind == "rope_q":
            r = _rope(z * (HEAD_DIM ** -0.5), tiled(cos_ref), tiled(sa_ref), tiled(sb_ref))
            o_ref[...] = r.astype(BF16).reshape(nb, tq, d)
        elif kind == "rope_k":
            r = _rope(z, tiled(cos_ref), tiled(sa_ref), tiled(sb_ref))
            o_ref[...] = r.astype(BF16).reshape(nb, tq, d)
        elif kind == "bf16":
            o_ref[...] = z.astype(BF16).reshape(nb, tq, d)
        elif kind == "sigmoid":
            o_ref[...] = jax.nn.sigmoid(z).astype(BF16).reshape(nb, tq, d)
        else:
            raise ValueError(kind)


def _inproj_call(x, sh, sc, w, b, tables, kinds, name):
    nb, n, d = x.shape
    g = len(kinds)
    assert w.shape == (d, g * d)
    cos, sa, sb = tables
    bat = lambda i: (0, i, 0)
    tokm = lambda i: (i, 0, 0)
    const3 = lambda i: (0, 0, 0)
    const2 = lambda i: (0, 0)
    out_shapes, out_specs = [], []
    for kind in kinds:
        if kind == "tok_major":
            out_shapes.append(jax.ShapeDtypeStruct((n, nb, d), F32))
            out_specs.append(pl.BlockSpec((TQ, nb, d), tokm))
        else:
            out_shapes.append(jax.ShapeDtypeStruct((nb, n, d), BF16))
            out_specs.append(pl.BlockSpec((nb, TQ, d), bat))
    rows_per_table = cos.shape[0] // TQ
    return pl.pallas_call(
        functools.partial(_inproj_kernel, kinds=kinds),
        grid=(n // TQ,),
        in_specs=[
            pl.BlockSpec((nb, TQ, d), bat),
            pl.BlockSpec(sh.shape, const3),
            pl.BlockSpec(sc.shape, const3),
            pl.BlockSpec(w.shape, const2, pipeline_mode=pl.Buffered(1)),
            pl.BlockSpec(b.shape, const2),
            pl.BlockSpec((TQ, LANES), lambda i: (i % rows_per_table, 0)),
            pl.BlockSpec((TQ, LANES), lambda i: (i % rows_per_table, 0)),
            pl.BlockSpec((TQ, LANES), lambda i: (i % rows_per_table, 0)),
        ],
        out_specs=out_specs,
        out_shape=out_shapes,
        compiler_params=_cparams(("arbitrary",)),
        name=name,
    )(x, sh, sc, w, b, cos, sa, sb)


def _rope_tables(n):
    t = jnp.arange(n)
    row = (t // GRID_W).astype(F32)
    col = (t % GRID_W).astype(F32)
    n_freq = HEAD_DIM // 4
    inv = ROPE_BASE ** (-jnp.arange(n_freq, dtype=F32) / n_freq)
    ang_r = row[:, None] * inv
    ang_c = col[:, None] * inv
    zero = jnp.zeros_like(ang_r)
    cos_h = jnp.concatenate([jnp.cos(ang_r), jnp.cos(ang_r), jnp.cos(ang_c), jnp.cos(ang_c)], axis=1)
    sa_h = jnp.concatenate([zero, jnp.sin(ang_r), zero, jnp.sin(ang_c)], axis=1)
    sb_h = jnp.concatenate([-jnp.sin(ang_r), zero, -jnp.sin(ang_c), zero], axis=1)
    rep = LANES // HEAD_DIM
    return tuple(jnp.tile(a, (1, rep)) for a in (cos_h, sa_h, sb_h))


def _lru_kernel(*refs, tn, nt, reverse, fuse_out):
    if fuse_out:
        (xp_ref, xc_ref, xn_ref, cw_ref, cb_ref, wg_ref, bg_ref, lam_ref, h0_ref, hf_ref, gy_ref,
         out_ref, hfin_ref, a_s, b_s, hcar_s) = refs
    else:
        (xp_ref, xc_ref, xn_ref, cw_ref, cb_ref, wg_ref, bg_ref, lam_ref, h0_ref,
         out_ref, hfin_ref, a_s, b_s, hcar_s) = refs
    i = pl.program_id(0)
    ti = (nt - 1 - i) if reverse else i
    nb, d = hcar_s.shape

    @pl.when(i == 0)
    def _():
        hcar_s[...] = h0_ref[...]

    x = xc_ref[...]
    xp = jnp.where(ti == 0, 0.0, xp_ref[...])
    xnx = jnp.where(ti == nt - 1, 0.0, xn_ref[...])
    ext = jnp.concatenate([xp, x, xnx], axis=0)
    xc = cb_ref[...]
    for k in range(CONV_W):
        xc = xc + ext[k:k + tn] * cw_ref[k]
    rows = tn * nb
    xc2 = xc.reshape(rows, d)
    for s in range(d // LRU_SLAB):
        sl = slice(s * LRU_SLAB, (s + 1) * LRU_SLAB)
        xs = xc2[:, sl]
        gates = jnp.dot(xs.astype(BF16), wg_ref[s], preferred_element_type=F32) + bg_ref[s]
        r = jax.nn.sigmoid(gates[:, :LRU_SLAB])
        ig = jax.nn.sigmoid(gates[:, LRU_SLAB:])
        z = -lam_ref[s]
        softplus = jnp.maximum(z, 0.0) + jnp.log1p(jnp.exp(-jnp.abs(z)))
        log_a = r * (-LRU_C * softplus)
        a = jnp.exp(log_a)
        b = jnp.sqrt(1.0 - a * a) * (ig * xs)
        a_s[:, :, sl] = a.reshape(tn, nb, LRU_SLAB)
        b_s[:, :, sl] = b.reshape(tn, nb, LRU_SLAB)

    unroll = 8

    def step(j, h):
        for k in range(unroll):
            jj = j * unroll + k
            t = (tn - 1 - jj) if reverse else jj
            h = a_s[t] * h + b_s[t]
            a_s[t] = h
        return h

    h = lax.fori_loop(0, tn // unroll, step, hcar_s[...])
    hcar_s[...] = h
    hfin_ref[...] = h
    if fuse_out:
        hs = jnp.swapaxes(hf_ref[...] + a_s[...], 0, 1)
        out_ref[...] = (hs * gy_ref[...].astype(F32)).astype(BF16)
    else:
        out_ref[...] = a_s[...]


def _lru_call(xr_t, conv_w, conv_b, wg, bg, lam, h0, hf=None, gy=None, *, reverse, name):
    n, nb, d = xr_t.shape
    tn = TQ
    nt = n // tn
    fuse_out = hf is not None
    tile = (lambda i: nt - 1 - i) if reverse else (lambda i: i)
    n_slab = d // LRU_SLAB
    in_specs = [
        pl.BlockSpec((2, nb, d), lambda i: (jnp.maximum(tile(i) * (tn // 2) - 1, 0), 0, 0)),
        pl.BlockSpec((tn, nb, d), lambda i: (tile(i), 0, 0)),
        pl.BlockSpec((1, nb, d), lambda i: (jnp.minimum((tile(i) + 1) * tn, n - 1), 0, 0)),
        pl.BlockSpec((CONV_W, 1, d), lambda i: (0, 0, 0)),
        pl.BlockSpec((1, d), lambda i: (0, 0)),
        pl.BlockSpec((n_slab, LRU_SLAB, 2 * LRU_SLAB), lambda i: (0, 0, 0)),
        pl.BlockSpec((n_slab, 1, 2 * LRU_SLAB), lambda i: (0, 0, 0)),
        pl.BlockSpec((n_slab, 1, LRU_SLAB), lambda i: (0, 0, 0)),
        pl.BlockSpec((nb, d), lambda i: (0, 0)),
    ]
    args = [xr_t, xr_t, xr_t, conv_w.reshape(CONV_W, 1, d), conv_b.reshape(1, d), wg, bg, lam, h0]
    if fuse_out:
        in_specs += [
            pl.BlockSpec((tn, nb, d), lambda i: (tile(i), 0, 0)),
            pl.BlockSpec((nb, tn, d), lambda i: (0, tile(i), 0)),
        ]
        args += [hf, gy]
        out_shape0 = jax.ShapeDtypeStruct((nb, n, d), BF16)
        out_spec0 = pl.BlockSpec((nb, tn, d), lambda i: (0, tile(i), 0))
    else:
        out_shape0 = jax.ShapeDtypeStruct((n, nb, d), F32)
        out_spec0 = pl.BlockSpec((tn, nb, d), lambda i: (tile(i), 0, 0))
    return pl.pallas_call(
        functools.partial(_lru_kernel, tn=tn, nt=nt, reverse=reverse, fuse_out=fuse_out),
        grid=(nt,),
        in_specs=in_specs,
        out_specs=[out_spec0, pl.BlockSpec((nb, d), lambda i: (0, 0))],
        out_shape=[out_shape0, jax.ShapeDtypeStruct((nb, d), F32)],
        scratch_shapes=[
            pltpu.VMEM((tn, nb, d), F32),
            pltpu.VMEM((tn, nb, d), F32),
            pltpu.VMEM((nb, d), F32),
        ],
        compiler_params=_cparams(("arbitrary",)),
        name=name,
    )(*args)


def _lru_gate_weights(wa, ba, wi, bi, lam):
    d = wa.shape[0] * wa.shape[1]
    n_slab = d // LRU_SLAB
    eye = jnp.eye(wa.shape[0], dtype=F32)

    def dense_slabs(w):
        full = jnp.einsum("gij,gh->gihj", w, eye).reshape(d, d)
        return jnp.stack([full[s * LRU_SLAB:(s + 1) * LRU_SLAB, s * LRU_SLAB:(s + 1) * LRU_SLAB]
                          for s in range(n_slab)])

    wg = jnp.concatenate([dense_slabs(wa), dense_slabs(wi)], axis=2).astype(BF16)
    bg = jnp.concatenate([ba.reshape(n_slab, 1, LRU_SLAB), bi.reshape(n_slab, 1, LRU_SLAB)], axis=2)
    return wg, bg, lam.reshape(n_slab, 1, LRU_SLAB)


def _na_kernel(q_ref, k_ref, v_ref, kc_ref, vc_ref, tab_ref, o_ref, s_s, sc_s, p_s, pc_s, inv_s, *, rows):
    win = WIN_R * GRID_W
    ch = NA_CHUNK
    n_chunk = rows // ch
    qh = 2 * GRID_W
    lane = lax.broadcasted_iota(jnp.int32, (GRID_W, LANES), 1)
    first_head = lane < HEAD_DIM
    nt_dims = (((1,), (1,)), ((), ()))

    def row_start(r):
        return pl.multiple_of(r * GRID_W, GRID_W)

    def stage1(c, slot):
        qs_all = []
        for u in range(ch):
            r = c * ch + u
            rs = jnp.clip(r - WIN_R // 2, 0, rows - WIN_R)
            q = q_ref[0, pl.ds(row_start(r), GRID_W), :]
            zero = jnp.zeros_like(q)
            qs = jnp.concatenate([jnp.where(first_head, q, zero), jnp.where(first_head, zero, q)], axis=0)
            qs_all.append(qs)
            kl = k_ref[0, pl.ds(row_start(rs), win), :]
            s = lax.dot_general(qs, kl, nt_dims, preferred_element_type=F32)
            d0 = rs - r + (WIN_R - 1)
            for m in range(WIN_R // 2):
                s_s[slot, u, :, m * LANES:(m + 1) * LANES] = s[:, m * LANES:(m + 1) * LANES] + tab_ref[0, d0 + 2 * m]
        sc_s[slot] = lax.dot_general(jnp.concatenate(qs_all, axis=0), kc_ref[0], nt_dims,
                                     preferred_element_type=F32)

    def stage2(slot):
        for u in range(ch):
            s = s_s[slot, u]
            sc = sc_s[slot, u * qh:(u + 1) * qh, :]
            m = jnp.maximum(jnp.max(s, axis=1, keepdims=True), jnp.max(sc, axis=1, keepdims=True))
            p = jnp.exp(s - m)
            pc = jnp.exp(sc - m)
            inv = 1.0 / (jnp.sum(p, axis=1, keepdims=True) + jnp.sum(pc, axis=1, keepdims=True))
            p_s[slot, u] = p.astype(BF16)
            pc_s[slot, u * qh:(u + 1) * qh, :] = pc.astype(BF16)
            inv_s[slot, u] = jnp.broadcast_to(inv, (qh, LANES))

    def stage3(c, slot):
        oc = jnp.dot(pc_s[slot], vc_ref[0], preferred_element_type=F32)
        for u in range(ch):
            r = c * ch + u
            rs = jnp.clip(r - WIN_R // 2, 0, rows - WIN_R)
            vl = v_ref[0, pl.ds(row_start(rs), win), :]
            o = (jnp.dot(p_s[slot, u], vl, preferred_element_type=F32) + oc[u * qh:(u + 1) * qh]) * inv_s[slot, u]
            out = jnp.where(first_head, o[:GRID_W], o[GRID_W:])
            o_ref[0, pl.ds(row_start(r), GRID_W), :] = out.astype(BF16)

    stage1(0, 0)
    stage2(0)
    stage1(1, 1)

    def body(j, carry):
        slot = j % 2
        stage3(j - 2, slot)
        stage2(1 - slot)
        stage1(j, slot)
        return carry

    lax.fori_loop(2, n_chunk, body, 0)
    last = n_chunk % 2
    stage3(n_chunk - 2, last)
    stage2(1 - last)
    stage3(n_chunk - 1, 1 - last)


def _na_call(q, k, v, kc, vc, tab):
    nb, n, d = q.shape
    l = kc.shape[1]
    rows = n // GRID_W
    n_pair = d // LANES
    win = WIN_R * GRID_W
    qh = 2 * GRID_W
    assert rows % NA_CHUNK == 0 and rows // NA_CHUNK >= 2
    blk = lambda b, p: (b, 0, p)
    return pl.pallas_call(
        functools.partial(_na_kernel, rows=rows),
        grid=(nb, n_pair),
        in_specs=[
            pl.BlockSpec((1, n, LANES), blk),
            pl.BlockSpec((1, n, LANES), blk),
            pl.BlockSpec((1, n, LANES), blk),
            pl.BlockSpec((1, l, LANES), blk),
            pl.BlockSpec((1, l, LANES), blk),
            pl.BlockSpec((1,) + tab.shape[1:], lambda b, p: (p, 0, 0, 0)),
        ],
        out_specs=pl.BlockSpec((1, n, LANES), blk),
        out_shape=jax.ShapeDtypeStruct((nb, n, d), BF16),
        scratch_shapes=[
            pltpu.VMEM((2, NA_CHUNK, qh, win), F32),
            pltpu.VMEM((2, NA_CHUNK * qh, l), F32),
            pltpu.VMEM((2, NA_CHUNK, qh, win), BF16),
            pltpu.VMEM((2, NA_CHUNK * qh, l), BF16),
            pltpu.VMEM((2, NA_CHUNK, qh, LANES), F32),
        ],
        compiler_params=_cparams(("arbitrary", "arbitrary")),
        name="neighbourhood_attention",
    )(q, k, v, kc, vc, tab)


def _na_bias_table(rpb):
    qc = np.arange(GRID_W)
    kc = np.arange(GRID_W)
    cstart = np.clip(qc - WIN_C // 2, 0, GRID_W - WIN_C)
    ok = (kc[None, :] >= cstart[:, None]) & (kc[None, :] < cstart[:, None] + WIN_C)
    dc = np.clip(kc[None, :] - qc[:, None], -(WIN_C - 1), WIN_C - 1) + WIN_C - 1
    tab = jnp.where(ok[None, None], rpb[:, :, dc], NEG_INF).astype(F32)
    pair = jnp.concatenate([tab[:, :-1], tab[:, 1:]], axis=-1)
    h = pair.shape[0]
    pair = pair.reshape(h // 2, 2, 2 * WIN_R - 2, GRID_W, 2 * GRID_W)
    return pair.transpose(0, 2, 1, 3, 4).reshape(h // 2, 2 * WIN_R - 2, 2 * GRID_W, 2 * GRID_W)


def _mix_kernel(yr_ref, yn_ref, gr_ref, gn_ref, x_ref, g1_ref, sh2_ref, sc2_ref,
                wpr_ref, wpn_ref, wo_ref, wr_ref, x1_ref, xn2_ref, lg_ref):
    nb, tq, d = x_ref.shape
    rows = nb * tq
    pr = jnp.dot(yr_ref[...].reshape(rows, d), wpr_ref[...], preferred_element_type=F32)
    pn = jnp.dot(yn_ref[...].reshape(rows, d), wpn_ref[...], preferred_element_type=F32)
    mix = gr_ref[...].reshape(rows, d).astype(F32) * pr + gn_ref[...].reshape(rows, d).astype(F32) * pn
    o = jnp.dot(mix.astype(BF16), wo_ref[...], preferred_element_type=F32)
    x1 = x_ref[...] + g1_ref[...] * o.reshape(nb, tq, d)
    x1_ref[...] = x1
    xb = (_rms(x1) * (1.0 + sc2_ref[...]) + sh2_ref[...]).astype(BF16)
    xn2_ref[...] = xb
    lg = jnp.dot(xb.reshape(rows, d), wr_ref[...], preferred_element_type=F32)
    lg_ref[...] = lg.reshape(nb, tq, LANES)


def _mix_call(yr, yn, gr, gn, x, g1, sh2, sc2, wpr, wpn, wo, wr):
    nb, n, d = x.shape
    bat = lambda i: (0, i, 0)
    const3 = lambda i: (0, 0, 0)
    const2 = lambda i: (0, 0)
    act = pl.BlockSpec((nb, TQ, d), bat)
    vec = pl.BlockSpec((nb, 1, d), const3)
    wsp = lambda w: pl.BlockSpec(w.shape, const2, pipeline_mode=pl.Buffered(1))
    return pl.pallas_call(
        _mix_kernel,
        grid=(n // TQ,),
        in_specs=[act, act, act, act, act, vec, vec, vec, wsp(wpr), wsp(wpn), wsp(wo), wsp(wr)],
        out_specs=[act, act, pl.BlockSpec((nb, TQ, LANES), bat)],
        out_shape=[
            jax.ShapeDtypeStruct((nb, n, d), F32),
            jax.ShapeDtypeStruct((nb, n, d), BF16),
            jax.ShapeDtypeStruct((nb, n, LANES), F32),
        ],
        compiler_params=_cparams(("arbitrary",)),
        name="merge_out_norm_router",
    )(yr, yn, gr, gn, x, g1, sh2, sc2, wpr, wpn, wo, wr)


def _ffn_kernel(xe_ref, g_ref, wg_ref, wu_ref, wd_ref, o_ref, acc_s):
    f = pl.program_id(2)
    xe = xe_ref[0]
    h1 = jnp.dot(xe, wg_ref[0].astype(BF16), preferred_element_type=F32)
    h2 = jnp.dot(xe, wu_ref[0].astype(BF16), preferred_element_type=F32)
    hid = (h1 * jax.nn.sigmoid(h1) * h2).astype(BF16)
    part = jnp.dot(hid, wd_ref[0].astype(BF16), preferred_element_type=F32)

    @pl.when(f == 0)
    def _():
        acc_s[...] = part

    @pl.when(f > 0)
    def _():
        acc_s[...] += part

    @pl.when(f == pl.num_programs(2) - 1)
    def _():
        o_ref[0] = (acc_s[...] * g_ref[0]).astype(BF16)


def _ffn_call(xe, g, w_gate, w_up, w_down):
    e, m, d = xe.shape
    dff = w_gate.shape[2]
    tm = min(FFN_TM, m)
    return pl.pallas_call(
        _ffn_kernel,
        grid=(e, m // tm, dff // FFN_TF),
        in_specs=[
            pl.BlockSpec((1, tm, d), lambda ei, mi, fi: (ei, mi, 0)),
            pl.BlockSpec((1, tm, 1), lambda ei, mi, fi: (ei, mi, 0)),
            pl.BlockSpec((1, d, FFN_TF), lambda ei, mi, fi: (ei, 0, fi)),
            pl.BlockSpec((1, d, FFN_TF), lambda ei, mi, fi: (ei, 0, fi)),
            pl.BlockSpec((1, FFN_TF, d), lambda ei, mi, fi: (ei, fi, 0)),
        ],
        out_specs=pl.BlockSpec((1, tm, d), lambda ei, mi, fi: (ei, mi, 0)),
        out_shape=jax.ShapeDtypeStruct((e, m, d), BF16),
        scratch_shapes=[pltpu.VMEM((tm, d), F32)],
        compiler_params=_cparams(("arbitrary", "arbitrary", "arbitrary")),
        name="expert_ffn",
    )(xe, g, w_gate, w_up, w_down)


def _final_kernel(x1_ref, moe_ref, g2_ref, fn_ref, o_ref):
    x = x1_ref[...] + g2_ref[...] * moe_ref[...]
    o_ref[...] = _rms(x) * fn_ref[...]


def _final_call(x1, moe, g2, fnorm):
    nb, n, d = x1.shape
    bat = lambda i: (0, i, 0)
    act = pl.BlockSpec((nb, TQ, d), bat)
    return pl.pallas_call(
        _final_kernel,
        grid=(n // TQ,),
        in_specs=[act, act, pl.BlockSpec((nb, 1, d), lambda i: (0, 0, 0)),
                  pl.BlockSpec((1, 1, d), lambda i: (0, 0, 0))],
        out_specs=act,
        out_shape=jax.ShapeDtypeStruct((nb, n, d), F32),
        compiler_params=_cparams(("arbitrary",)),
        name="final_residual_norm",
    )(x1, moe, g2, fnorm.reshape(1, 1, d))


def kernel(x, c, ctx, c_ctx, w_mod, b_mod, w_in, b_in, conv_w, conv_b, lru_wa, lru_ba, lru_wi, lru_bi,
           lru_lambda, na_rpb, w_proj_rnn, w_proj_na, w_out, w_router, w_exp_gate, w_exp_up, w_exp_down,
           final_norm):
    nb, n, d = x.shape
    l = ctx.shape[1]
    assert w_mod.shape[0] == 1, "single-layer problem"
    assert nb == SUBLANES and n % TQ == 0 and l % TQ == 0 and d % LRU_SLAB == 0
    lyr = 0

    cc = jnp.concatenate([c, c_ctx[None], jnp.zeros((2 * SUBLANES - nb - 1, d), F32)], axis=0)
    mod = _mod_call(cc, w_mod[lyr], b_mod[lyr])
    sh1, sc1, g1, sh2, sc2, g2 = [m[:nb, None, :] for m in jnp.split(mod, 6, axis=-1)]
    csh1, csc1 = [jnp.broadcast_to(m[nb:nb + 1, None, :], (nb, 1, d)) for m in jnp.split(mod, 6, axis=-1)[:2]]

    w_in_b = w_in[lyr].astype(BF16)
    b_in_r = b_in[lyr].reshape(1, -1)
    tables = _rope_tables(n)
    gy, xr_t, q, k, v, sgr, sgn = _inproj_call(
        x, sh1, sc1, w_in_b, b_in_r, tables,
        ("gelu", "tok_major", "rope_q", "rope_k", "bf16", "sigmoid", "sigmoid"), "in_proj_latent")
    ctx_cols = np.concatenate([np.arange(d, 2 * d), np.arange(3 * d, 5 * d)])
    xrc_t, kc, vc = _inproj_call(
        ctx, csh1, csc1, w_in_b[:, ctx_cols], b_in_r[:, ctx_cols], tables,
        ("tok_major", "bf16", "bf16"), "in_proj_context")

    gw = [_lru_gate_weights(lru_wa[lyr, dr], lru_ba[lyr, dr], lru_wi[lyr, dr], lru_bi[lyr, dr], lru_lambda[lyr, dr])
          for dr in range(2)]
    zeros_h = jnp.zeros((nb, d), F32)
    _, hc_f = _lru_call(xrc_t, conv_w[lyr], conv_b[lyr], *gw[0], zeros_h, reverse=False, name="rglru_ctx_fwd")
    _, hc_b = _lru_call(xrc_t, conv_w[lyr], conv_b[lyr], *gw[1], zeros_h, reverse=True, name="rglru_ctx_bwd")
    hf_t, _ = _lru_call(xr_t, conv_w[lyr], conv_b[lyr], *gw[0], hc_f, reverse=False, name="rglru_fwd")
    y_rnn, _ = _lru_call(xr_t, conv_w[lyr], conv_b[lyr], *gw[1], hc_b, hf_t, gy, reverse=True, name="rglru_bwd")

    y_na = _na_call(q, k, v, kc, vc, _na_bias_table(na_rpb[lyr]))

    wr = jnp.zeros((d, LANES), BF16).at[:, :N_EXPERTS].set(w_router[lyr].astype(BF16))
    x1, xn2, logits = _mix_call(y_rnn, y_na, sgr, sgn, x, g1, sh2, sc2,
                                w_proj_rnn[lyr].astype(BF16), w_proj_na[lyr].astype(BF16),
                                w_out[lyr].astype(BF16), wr)

    cap = EC_CAPACITY * n // N_EXPERTS
    aff = jax.nn.softmax(logits[..., :N_EXPERTS], axis=-1)
    gsel, idx = lax.top_k(aff.transpose(0, 2, 1), cap)
    bidx = jnp.arange(nb)[:, None, None]
    xe = xn2[bidx, idx].transpose(1, 0, 2, 3).reshape(N_EXPERTS, nb * cap, d)
    ge = gsel.transpose(1, 0, 2).reshape(N_EXPERTS, nb * cap, 1)
    ye = _ffn_call(xe, ge, w_exp_gate[lyr], w_exp_up[lyr], w_exp_down[lyr])
    ye = ye.reshape(N_EXPERTS, nb, cap, d).transpose(1, 0, 2, 3).astype(F32)
    moe = jnp.zeros((nb, n, d), F32).at[bidx, idx].add(ye)

    return _final_call(x1, moe, g2, final_norm)
```

```python
import functools

import numpy as np
import jax
import jax.numpy as jnp
from jax import lax
from jax.experimental import pallas as pl
from jax.experimental.pallas import tpu as pltpu

F32 = jnp.float32
BF16 = jnp.bfloat16

GRID_W = 64
N_HEADS = 16
HEAD_DIM = 64
N_LRU_BLOCKS = 16
CONV_W = 4
LRU_C = 8.0
WIN_R = 8
WIN_C = 16
ROPE_BASE = 10000.0
N_EXPERTS = 16
EC_CAPACITY = 2
EPS = 1e-6
NEG_INF = -1e30

LANES = 128
SUBLANES = 8
VMEM_LIMIT = 56 * 1024 * 1024

TQ = 64
LRU_SLAB = 256
FFN_TM = 2048
FFN_TF = 512
NA_CHUNK = 2
ROUTE_CHUNK = 256
ROUTE_WIN = 64
SLOT_ALIGN = 16


def _cparams(sem):
    return pltpu.CompilerParams(dimension_semantics=sem, vmem_limit_bytes=VMEM_LIMIT)


def _rms(x):
    return x * lax.rsqrt(jnp.mean(x * x, axis=-1, keepdims=True) + EPS)


def _mod_kernel(c_ref, w_ref, b_ref, o_ref):
    c = c_ref[...]
    s = c * jax.nn.sigmoid(c)
    o_ref[...] = jnp.dot(s, w_ref[...], preferred_element_type=F32) + b_ref[...]


def _mod_call(cc, w_mod, b_mod):
    rows, d = cc.shape
    n_out = w_mod.shape[1]
    tn = 1024
    return pl.pallas_call(
        _mod_kernel,
        grid=(n_out // tn,),
        in_specs=[
            pl.BlockSpec((rows, d), lambda j: (0, 0)),
            pl.BlockSpec((d, tn), lambda j: (0, j)),
            pl.BlockSpec((1, tn), lambda j: (0, j)),
        ],
        out_specs=pl.BlockSpec((rows, tn), lambda j: (0, j)),
        out_shape=jax.ShapeDtypeStruct((rows, n_out), F32),
        compiler_params=_cparams(("arbitrary",)),
        name="adaln_mod",
    )(cc, w_mod, b_mod.reshape(1, n_out))


def _rope(z, cos, sa, sb):
    outs = []
    for g in range(z.shape[1] // LANES):
        zg = z[:, g * LANES:(g + 1) * LANES]
        outs.append(zg * cos + pltpu.roll(zg, 16, 1) * sa + pltpu.roll(zg, LANES - 16, 1) * sb)
    return jnp.concatenate(outs, axis=1)


def _inproj_kernel(*refs, kinds):
    n_out = len(kinds)
    x_ref, sh_ref, sc_ref, w_ref, b_ref, cos_ref, sa_ref, sb_ref = refs[:8]
    out_refs = refs[8:8 + n_out]
    nb, tq, d = x_ref.shape
    rows = nb * tq
    xn = _rms(x_ref[...]) * (1.0 + sc_ref[...]) + sh_ref[...]
    xb = xn.reshape(rows, d).astype(BF16)

    def tiled(t_ref):
        return jnp.broadcast_to(t_ref[...][None], (nb, tq, LANES)).reshape(rows, LANES)

    for g, (kind, o_ref) in enumerate(zip(kinds, out_refs)):
        z = jnp.dot(xb, w_ref[:, g * d:(g + 1) * d], preferred_element_type=F32) + b_ref[:, g * d:(g + 1) * d]
        if kind == "gelu":
            o_ref[...] = jax.nn.gelu(z, approximate=True).astype(BF16).reshape(nb, tq, d)
        elif kind == "tok_major":
            o_ref[...] = jnp.swapaxes(z.reshape(nb, tq, d), 0, 1)
        elif kind == "rope_q":
            r = _rope(z * (HEAD_DIM ** -0.5), tiled(cos_ref), tiled(sa_ref), tiled(sb_ref))
            o_ref[...] = r.astype(BF16).reshape(nb, tq, d)
        elif kind == "rope_k":
            r = _rope(z, tiled(cos_ref), tiled(sa_ref), tiled(sb_ref))
            o_ref[...] = r.astype(BF16).reshape(nb, tq, d)
        elif kind == "bf16":
            o_ref[...] = z.astype(BF16).reshape(nb, tq, d)
        elif kind == "sigmoid":
            o_ref[...] = jax.nn.sigmoid(z).astype(BF16).reshape(nb, tq, d)
        else:
            raise ValueError(kind)


def _inproj_call(x, sh, sc, w, b, tables, kinds, name):
    nb, n, d = x.shape
    g = len(kinds)
    assert w.shape == (d, g * d)
    cos, sa, sb = tables
    bat = lambda i: (0, i, 0)
    tokm = lambda i: (i, 0, 0)
    const3 = lambda i: (0, 0, 0)
    const2 = lambda i: (0, 0)
    out_shapes, out_specs = [], []
    for kind in kinds:
        if kind == "tok_major":
            out_shapes.append(jax.ShapeDtypeStruct((n, nb, d), F32))
            out_specs.append(pl.BlockSpec((TQ, nb, d), tokm))
        else:
            out_shapes.append(jax.ShapeDtypeStruct((nb, n, d), BF16))
            out_specs.append(pl.BlockSpec((nb, TQ, d), bat))
    rows_per_table = cos.shape[0] // TQ
    return pl.pallas_call(
        functools.partial(_inproj_kernel, kinds=kinds),
        grid=(n // TQ,),
        in_specs=[
            pl.BlockSpec((nb, TQ, d), bat),
            pl.BlockSpec(sh.shape, const3),
            pl.BlockSpec(sc.shape, const3),
            pl.BlockSpec(w.shape, const2, pipeline_mode=pl.Buffered(1)),
            pl.BlockSpec(b.shape, const2),
            pl.BlockSpec((TQ, LANES), lambda i: (i % rows_per_table, 0)),
            pl.BlockSpec((TQ, LANES), lambda i: (i % rows_per_table, 0)),
            pl.BlockSpec((TQ, LANES), lambda i: (i % rows_per_table, 0)),
        ],
        out_specs=out_specs,
        out_shape=out_shapes,
        compiler_params=_cparams(("arbitrary",)),
        name=name,
    )(x, sh, sc, w, b, cos, sa, sb)


def _rope_tables(n):
    t = jnp.arange(n)
    row = (t // GRID_W).astype(F32)
    col = (t % GRID_W).astype(F32)
    n_freq = HEAD_DIM // 4
    inv = ROPE_BASE ** (-jnp.arange(n_freq, dtype=F32) / n_freq)
    ang_r = row[:, None] * inv
    ang_c = col[:, None] * inv
    zero = jnp.zeros_like(ang_r)
    cos_h = jnp.concatenate([jnp.cos(ang_r), jnp.cos(ang_r), jnp.cos(ang_c), jnp.cos(ang_c)], axis=1)
    sa_h = jnp.concatenate([zero, jnp.sin(ang_r), zero, jnp.sin(ang_c)], axis=1)
    sb_h = jnp.concatenate([-jnp.sin(ang_r), zero, -jnp.sin(ang_c), zero], axis=1)
    rep = LANES // HEAD_DIM
    return tuple(jnp.tile(a, (1, rep)) for a in (cos_h, sa_h, sb_h))


def _lru_kernel(*refs, tn, nt, reverse, fuse_out):
    if fuse_out:
        (xp_ref, xc_ref, xn_ref, cw_ref, cb_ref, wg_ref, bg_ref, lam_ref, h0_ref, hf_ref, gy_ref,
         out_ref, hfin_ref, a_s, b_s, hcar_s) = refs
    else:
        (xp_ref, xc_ref, xn_ref, cw_ref, cb_ref, wg_ref, bg_ref, lam_ref, h0_ref,
         out_ref, hfin_ref, a_s, b_s, hcar_s) = refs
    i = pl.program_id(0)
    ti = (nt - 1 - i) if reverse else i
    nb, d = hcar_s.shape

    @pl.when(i == 0)
    def _():
        hcar_s[...] = h0_ref[...]

    x = xc_ref[...]
    xp = jnp.where(ti == 0, 0.0, xp_ref[...])
    xnx = jnp.where(ti == nt - 1, 0.0, xn_ref[...])
    ext = jnp.concatenate([xp, x, xnx], axis=0)
    xc = cb_ref[...]
    for k in range(CONV_W):
        xc = xc + ext[k:k + tn] * cw_ref[k]
    rows = tn * nb
    xc2 = xc.reshape(rows, d)
    for s in range(d // LRU_SLAB):
        sl = slice(s * LRU_SLAB, (s + 1) * LRU_SLAB)
        xs = xc2[:, sl]
        gates = jnp.dot(xs.astype(BF16), wg_ref[s], preferred_element_type=F32) + bg_ref[s]
        r = jax.nn.sigmoid(gates[:, :LRU_SLAB])
        ig = jax.nn.sigmoid(gates[:, LRU_SLAB:])
        z = -lam_ref[s]
        softplus = jnp.maximum(z, 0.0) + jnp.log1p(jnp.exp(-jnp.abs(z)))
        log_a = r * (-LRU_C * softplus)
        a = jnp.exp(log_a)
        b = jnp.sqrt(1.0 - a * a) * (ig * xs)
        a_s[:, :, sl] = a.reshape(tn, nb, LRU_SLAB)
        b_s[:, :, sl] = b.reshape(tn, nb, LRU_SLAB)

    unroll = 8

    def step(j, h):
        for k in range(unroll):
            jj = j * unroll + k
            t = (tn - 1 - jj) if reverse else jj
            h = a_s[t] * h + b_s[t]
            a_s[t] = h
        return h

    h = lax.fori_loop(0, tn // unroll, step, hcar_s[...])
    hcar_s[...] = h
    hfin_ref[...] = h
    if fuse_out:
        hs = jnp.swapaxes(hf_ref[...] + a_s[...], 0, 1)
        out_ref[...] = (hs * gy_ref[...].astype(F32)).astype(BF16)
    else:
        out_ref[...] = a_s[...]


def _lru_call(xr_t, conv_w, conv_b, wg, bg, lam, h0, hf=None, gy=None, *, reverse, name):
    n, nb, d = xr_t.shape
    tn = TQ
    nt = n // tn
    fuse_out = hf is not None
    tile = (lambda i: nt - 1 - i) if reverse else (lambda i: i)
    n_slab = d // LRU_SLAB
    in_specs = [
        pl.BlockSpec((2, nb, d), lambda i: (jnp.maximum(tile(i) * (tn // 2) - 1, 0), 0, 0)),
        pl.BlockSpec((tn, nb, d), lambda i: (tile(i), 0, 0)),
        pl.BlockSpec((1, nb, d), lambda i: (jnp.minimum((tile(i) + 1) * tn, n - 1), 0, 0)),
        pl.BlockSpec((CONV_W, 1, d), lambda i: (0, 0, 0)),
        pl.BlockSpec((1, d), lambda i: (0, 0)),
        pl.BlockSpec((n_slab, LRU_SLAB, 2 * LRU_SLAB), lambda i: (0, 0, 0)),
        pl.BlockSpec((n_slab, 1, 2 * LRU_SLAB), lambda i: (0, 0, 0)),
        pl.BlockSpec((n_slab, 1, LRU_SLAB), lambda i: (0, 0, 0)),
        pl.BlockSpec((nb, d), lambda i: (0, 0)),
    ]
    args = [xr_t, xr_t, xr_t, conv_w.reshape(CONV_W, 1, d), conv_b.reshape(1, d), wg, bg, lam, h0]
    if fuse_out:
        in_specs += [
            pl.BlockSpec((tn, nb, d), lambda i: (tile(i), 0, 0)),
            pl.BlockSpec((nb, tn, d), lambda i: (0, tile(i), 0)),
        ]
        args += [hf, gy]
        out_shape0 = jax.ShapeDtypeStruct((nb, n, d), BF16)
        out_spec0 = pl.BlockSpec((nb, tn, d), lambda i: (0, tile(i), 0))
    else:
        out_shape0 = jax.ShapeDtypeStruct((n, nb, d), F32)
        out_spec0 = pl.BlockSpec((tn, nb, d), lambda i: (tile(i), 0, 0))
    return pl.pallas_call(
        functools.partial(_lru_kernel, tn=tn, nt=nt, reverse=reverse, fuse_out=fuse_out),
        grid=(nt,),
        in_specs=in_specs,
        out_specs=[out_spec0, pl.BlockSpec((nb, d), lambda i: (0, 0))],
        out_shape=[out_shape0, jax.ShapeDtypeStruct((nb, d), F32)],
        scratch_shapes=[
            pltpu.VMEM((tn, nb, d), F32),
            pltpu.VMEM((tn, nb, d), F32),
            pltpu.VMEM((nb, d), F32),
        ],
        compiler_params=_cparams(("arbitrary",)),
        name=name,
    )(*args)


def _lru_gate_weights(wa, ba, wi, bi, lam):
    d = wa.shape[0] * wa.shape[1]
    n_slab = d // LRU_SLAB
    eye = jnp.eye(wa.shape[0], dtype=F32)

    def dense_slabs(w):
        full = jnp.einsum("gij,gh->gihj", w, eye).reshape(d, d)
        return jnp.stack([full[s * LRU_SLAB:(s + 1) * LRU_SLAB, s * LRU_SLAB:(s + 1) * LRU_SLAB]
                          for s in range(n_slab)])

    wg = jnp.concatenate([dense_slabs(wa), dense_slabs(wi)], axis=2).astype(BF16)
    bg = jnp.concatenate([ba.reshape(n_slab, 1, LRU_SLAB), bi.reshape(n_slab, 1, LRU_SLAB)], axis=2)
    return wg, bg, lam.reshape(n_slab, 1, LRU_SLAB)


def _na_kernel(q_ref, k_ref, v_ref, kc_ref, vc_ref, tab_ref, o_ref, s_s, sc_s, p_s, pc_s, inv_s, *, rows):
    win = WIN_R * GRID_W
    ch = NA_CHUNK
    n_chunk = rows // ch
    qh = 2 * GRID_W
    lane = lax.broadcasted_iota(jnp.int32, (GRID_W, LANES), 1)
    first_head = lane < HEAD_DIM
    nt_dims = (((1,), (1,)), ((), ()))

    def row_start(r):
        return pl.multiple_of(r * GRID_W, GRID_W)

    def stage1(c, slot):
        qs_all = []
        for u in range(ch):
            r = c * ch + u
            rs = jnp.clip(r - WIN_R // 2, 0, rows - WIN_R)
            q = q_ref[0, pl.ds(row_start(r), GRID_W), :]
            zero = jnp.zeros_like(q)
            qs = jnp.concatenate([jnp.where(first_head, q, zero), jnp.where(first_head, zero, q)], axis=0)
            qs_all.append(qs)
            kl = k_ref[0, pl.ds(row_start(rs), win), :]
            s = lax.dot_general(qs, kl, nt_dims, preferred_element_type=F32)
            d0 = rs - r + (WIN_R - 1)
            for m in range(WIN_R // 2):
                s_s[slot, u, :, m * LANES:(m + 1) * LANES] = s[:, m * LANES:(m + 1) * LANES] + tab_ref[0, d0 + 2 * m]
        sc_s[slot] = lax.dot_general(jnp.concatenate(qs_all, axis=0), kc_ref[0], nt_dims,
                                     preferred_element_type=F32)

    def stage2(slot):
        for u in range(ch):
            s = s_s[slot, u]
            sc = sc_s[slot, u * qh:(u + 1) * qh, :]
            m = jnp.maximum(jnp.max(s, axis=1, keepdims=True), jnp.max(sc, axis=1, keepdims=True))
            p = jnp.exp(s - m)
            pc = jnp.exp(sc - m)
            inv = 1.0 / (jnp.sum(p, axis=1, keepdims=True) + jnp.sum(pc, axis=1, keepdims=True))
            p_s[slot, u] = p.astype(BF16)
            pc_s[slot, u * qh:(u + 1) * qh, :] = pc.astype(BF16)
            inv_s[slot, u] = jnp.broadcast_to(inv, (qh, LANES))

    def stage3(c, slot):
        oc = jnp.dot(pc_s[slot], vc_ref[0], preferred_element_type=F32)
        for u in range(ch):
            r = c * ch + u
            rs = jnp.clip(r - WIN_R // 2, 0, rows - WIN_R)
            vl = v_ref[0, pl.ds(row_start(rs), win), :]
            o = (jnp.dot(p_s[slot, u], vl, preferred_element_type=F32) + oc[u * qh:(u + 1) * qh]) * inv_s[slot, u]
            out = jnp.where(first_head, o[:GRID_W], o[GRID_W:])
            o_ref[0, pl.ds(row_start(r), GRID_W), :] = out.astype(BF16)

    stage1(0, 0)
    stage2(0)
    stage1(1, 1)

    def body(j, carry):
        slot = j % 2
        stage3(j - 2, slot)
        stage2(1 - slot)
        stage1(j, slot)
        return carry

    lax.fori_loop(2, n_chunk, body, 0)
    last = n_chunk % 2
    stage3(n_chunk - 2, last)
    stage2(1 - last)
    stage3(n_chunk - 1, 1 - last)


def _na_call(q, k, v, kc, vc, tab):
    nb, n, d = q.shape
    l = kc.shape[1]
    rows = n // GRID_W
    n_pair = d // LANES
    win = WIN_R * GRID_W
    qh = 2 * GRID_W
    assert rows % NA_CHUNK == 0 and rows // NA_CHUNK >= 2
    blk = lambda b, p: (b, 0, p)
    return pl.pallas_call(
        functools.partial(_na_kernel, rows=rows),
        grid=(nb, n_pair),
        in_specs=[
            pl.BlockSpec((1, n, LANES), blk),
            pl.BlockSpec((1, n, LANES), blk),
            pl.BlockSpec((1, n, LANES), blk),
            pl.BlockSpec((1, l, LANES), blk),
            pl.BlockSpec((1, l, LANES), blk),
            pl.BlockSpec((1,) + tab.shape[1:], lambda b, p: (p, 0, 0, 0)),
        ],
        out_specs=pl.BlockSpec((1, n, LANES), blk),
        out_shape=jax.ShapeDtypeStruct((nb, n, d), BF16),
        scratch_shapes=[
            pltpu.VMEM((2, NA_CHUNK, qh, win), F32),
            pltpu.VMEM((2, NA_CHUNK * qh, l), F32),
            pltpu.VMEM((2, NA_CHUNK, qh, win), BF16),
            pltpu.VMEM((2, NA_CHUNK * qh, l), BF16),
            pltpu.VMEM((2, NA_CHUNK, qh, LANES), F32),
        ],
        compiler_params=_cparams(("arbitrary", "arbitrary")),
        name="neighbourhood_attention",
    )(q, k, v, kc, vc, tab)


def _na_bias_table(rpb):
    qc = np.arange(GRID_W)
    kc = np.arange(GRID_W)
    cstart = np.clip(qc - WIN_C // 2, 0, GRID_W - WIN_C)
    ok = (kc[None, :] >= cstart[:, None]) & (kc[None, :] < cstart[:, None] + WIN_C)
    dc = np.clip(kc[None, :] - qc[:, None], -(WIN_C - 1), WIN_C - 1) + WIN_C - 1
    tab = jnp.where(ok[None, None], rpb[:, :, dc], NEG_INF).astype(F32)
    pair = jnp.concatenate([tab[:, :-1], tab[:, 1:]], axis=-1)
    h = pair.shape[0]
    pair = pair.reshape(h // 2, 2, 2 * WIN_R - 2, GRID_W, 2 * GRID_W)
    return pair.transpose(0, 2, 1, 3, 4).reshape(h // 2, 2 * WIN_R - 2, 2 * GRID_W, 2 * GRID_W)


def _mix_kernel(yr_ref, yn_ref, gr_ref, gn_ref, x_ref, g1_ref, sh2_ref, sc2_ref,
                wpr_ref, wpn_ref, wo_ref, wr_ref, x1_ref, xn2_ref, lg_ref):
    nb, tq, d = x_ref.shape
    rows = nb * tq
    pr = jnp.dot(yr_ref[...].reshape(rows, d), wpr_ref[...], preferred_element_type=F32)
    pn = jnp.dot(yn_ref[...].reshape(rows, d), wpn_ref[...], preferred_element_type=F32)
    mix = gr_ref[...].reshape(rows, d).astype(F32) * pr + gn_ref[...].reshape(rows, d).astype(F32) * pn
    o = jnp.dot(mix.astype(BF16), wo_ref[...], preferred_element_type=F32)
    x1 = x_ref[...] + g1_ref[...] * o.reshape(nb, tq, d)
    x1_ref[...] = x1
    xb = (_rms(x1) * (1.0 + sc2_ref[...]) + sh2_ref[...]).astype(BF16)
    xn2_ref[...] = xb
    lg_ref[0] = lax.dot_general(wr_ref[...], xb.reshape(rows, d), (((1,), (1,)), ((), ())),
                                preferred_element_type=F32)


def _mix_call(yr, yn, gr, gn, x, g1, sh2, sc2, wpr, wpn, wo, wr):
    nb, n, d = x.shape
    bat = lambda i: (0, i, 0)
    const3 = lambda i: (0, 0, 0)
    const2 = lambda i: (0, 0)
    act = pl.BlockSpec((nb, TQ, d), bat)
    vec = pl.BlockSpec((nb, 1, d), const3)
    wsp = lambda w: pl.BlockSpec(w.shape, const2, pipeline_mode=pl.Buffered(1))
    return pl.pallas_call(
        _mix_kernel,
        grid=(n // TQ,),
        in_specs=[act, act, act, act, act, vec, vec, vec, wsp(wpr), wsp(wpn), wsp(wo), wsp(wr)],
        out_specs=[act, act, pl.BlockSpec((1, wr.shape[0], nb * TQ), lambda i: (i, 0, 0))],
        out_shape=[
            jax.ShapeDtypeStruct((nb, n, d), F32),
            jax.ShapeDtypeStruct((nb, n, d), BF16),
            jax.ShapeDtypeStruct((n // TQ, wr.shape[0], nb * TQ), F32),
        ],
        compiler_params=_cparams(("arbitrary",)),
        name="merge_out_norm_router",
    )(yr, yn, gr, gn, x, g1, sh2, sc2, wpr, wpn, wo, wr)


def _route_kernel(lg_ref, rank_ref, aff_ref, cnt_ref, *, cap):
    lg = lg_ref[0]
    n_e, n = lg.shape
    ex = jnp.exp(lg - jnp.max(lg, axis=0, keepdims=True))
    aff = ex / jnp.sum(ex, axis=0, keepdims=True)
    aff_ref[0] = aff
    bits = lax.bitcast_convert_type(aff, jnp.int32)

    thr = jnp.zeros((n_e, 1), jnp.int32)
    for bit in range(30, -1, -1):
        cand = thr | (1 << bit)
        cnt = jnp.sum((bits >= cand).astype(F32), axis=1, keepdims=True)
        thr = jnp.where(cnt >= cap, cand, thr)
    gt = bits > thr
    eq = bits == thr
    need = cap - jnp.sum(gt.astype(F32), axis=1, keepdims=True)

    blk = ROUTE_CHUNK
    row = lax.broadcasted_iota(jnp.int32, (blk, blk), 0)
    col = lax.broadcasted_iota(jnp.int32, (blk, blk), 1)
    upper = (row <= col).astype(BF16)

    def cumsum_tokens(mask):
        off = jnp.zeros((n_e, 1), F32)
        outs, offs = [], []
        for c in range(n // blk):
            x = mask[:, c * blk:(c + 1) * blk].astype(BF16)
            cs = jnp.dot(x, upper, preferred_element_type=F32) + off
            offs.append(off)
            outs.append(cs)
            off = cs[:, blk - 1:blk]
        offs.append(off)
        return jnp.concatenate(outs, axis=1), offs

    cum_eq, _ = cumsum_tokens(eq)
    sel = gt | (eq & ((cum_eq - eq.astype(F32)) < need))
    cum_sel, offs = cumsum_tokens(sel)
    rank_ref[0] = jnp.where(sel, cum_sel - 1.0, -1.0).astype(jnp.int32)
    pad = jnp.zeros((n_e, LANES - len(offs)), F32)
    cnt_ref[0] = jnp.concatenate(offs + [pad], axis=1).astype(jnp.int32)


def _route_call(lg_t, cap):
    nb, n_e, n = lg_t.shape
    blk3 = lambda b: (b, 0, 0)
    return pl.pallas_call(
        functools.partial(_route_kernel, cap=cap),
        grid=(nb,),
        in_specs=[pl.BlockSpec((1, n_e, n), blk3)],
        out_specs=[pl.BlockSpec((1, n_e, n), blk3), pl.BlockSpec((1, n_e, n), blk3),
                   pl.BlockSpec((1, n_e, LANES), blk3)],
        out_shape=[jax.ShapeDtypeStruct((nb, n_e, n), jnp.int32), jax.ShapeDtypeStruct((nb, n_e, n), F32),
                   jax.ShapeDtypeStruct((nb, n_e, LANES), jnp.int32)],
        compiler_params=_cparams(("arbitrary",)),
        name="route_select",
    )(lg_t)


def _slot_windows(cnt_ref, b, c, n_e, n_chunk, cap):
    wins = []
    for e in range(n_e):
        base = (b * n_e + e) * (n_chunk + 1) + c
        lo = cnt_ref[base]
        hi = cnt_ref[base + 1]
        lo_al = jnp.minimum((lo // SLOT_ALIGN) * SLOT_ALIGN, cap - ROUTE_WIN)
        extra = jnp.maximum((hi - lo_al + ROUTE_WIN - 1) // ROUTE_WIN - 1, 0)
        wins.append((pl.multiple_of(lo_al, SLOT_ALIGN), extra))
    return wins


def _one_hot_t(rank_row, start, first=None):
    slot = start + lax.broadcasted_iota(jnp.int32, (ROUTE_WIN, rank_row.shape[1]), 0)
    hit = rank_row == slot
    if first is not None:
        hit = hit & (slot >= first)
    return hit


def _spill_window(k, lo_al, cap):
    first = lo_al + k * ROUTE_WIN
    start = pl.multiple_of(jnp.minimum(first, cap - ROUTE_WIN), SLOT_ALIGN)
    return start, first


def _dispatch_kernel(cnt_ref, x_ref, rank_ref, aff_ref, xe_ref, g_ref, *, cap, n_chunk):
    b = pl.program_id(0)
    c = pl.program_id(1)
    n_e = rank_ref.shape[1]

    @pl.when(c == 0)
    def _():
        xe_ref[...] = jnp.zeros_like(xe_ref)
        g_ref[...] = jnp.zeros_like(g_ref)

    x = x_ref[0]
    wins = _slot_windows(cnt_ref, b, c, n_e, n_chunk, cap)
    hots = [_one_hot_t(rank_ref[0, e:e + 1, :], wins[e][0]) for e in range(n_e)]
    stacked = jnp.concatenate([h.astype(BF16) for h in hots], axis=0)
    rows = jnp.dot(stacked, x, preferred_element_type=F32)
    for e in range(n_e):
        sl = pl.ds(wins[e][0], ROUTE_WIN)
        xe_ref[e, sl, :] = xe_ref[e, sl, :] + rows[e * ROUTE_WIN:(e + 1) * ROUTE_WIN].astype(BF16)
        g_ref[e, sl, :] = g_ref[e, sl, :] + jnp.sum(jnp.where(hots[e], aff_ref[0, e:e + 1, :], 0.0),
                                                    axis=1, keepdims=True)

    @pl.when(sum(extra for _, extra in wins) > 0)
    def _():
        for e in range(n_e):
            lo_al, extra = wins[e]

            def spill(k, carry, e=e, lo_al=lo_al):
                start, first = _spill_window(k, lo_al, cap)
                hot = _one_hot_t(rank_ref[0, e:e + 1, :], start, first)
                sl2 = pl.ds(start, ROUTE_WIN)
                xe_ref[e, sl2, :] = xe_ref[e, sl2, :] + jnp.dot(hot.astype(BF16), x_ref[0],
                                                               preferred_element_type=F32).astype(BF16)
                g_ref[e, sl2, :] = g_ref[e, sl2, :] + jnp.sum(jnp.where(hot, aff_ref[0, e:e + 1, :], 0.0),
                                                              axis=1, keepdims=True)
                return carry

            lax.fori_loop(1, extra + 1, spill, 0)


def _dispatch_call(cnt_flat, xn2, rank_t, aff_t, cap):
    nb, n, d = xn2.shape
    n_e = rank_t.shape[1]
    n_chunk = n // ROUTE_CHUNK
    grid_spec = pltpu.PrefetchScalarGridSpec(
        num_scalar_prefetch=1,
        grid=(nb, n_chunk),
        in_specs=[
            pl.BlockSpec((1, ROUTE_CHUNK, d), lambda b, c, cnt: (b, c, 0)),
            pl.BlockSpec((1, n_e, ROUTE_CHUNK), lambda b, c, cnt: (b, 0, c)),
            pl.BlockSpec((1, n_e, ROUTE_CHUNK), lambda b, c, cnt: (b, 0, c)),
        ],
        out_specs=[
            pl.BlockSpec((n_e, cap, d), lambda b, c, cnt: (0, b, 0)),
            pl.BlockSpec((n_e, cap, 1), lambda b, c, cnt: (0, b, 0)),
        ],
    )
    return pl.pallas_call(
        functools.partial(_dispatch_kernel, cap=cap, n_chunk=n_chunk),
        grid_spec=grid_spec,
        out_shape=[jax.ShapeDtypeStruct((n_e, nb * cap, d), BF16), jax.ShapeDtypeStruct((n_e, nb * cap, 1), F32)],
        compiler_params=_cparams(("arbitrary", "arbitrary")),
        name="moe_dispatch",
    )(cnt_flat, xn2, rank_t, aff_t)


def _ffn_kernel(xe_ref, g_ref, wg_ref, wu_ref, wd_ref, o_ref, acc_s):
    f = pl.program_id(2)
    xe = xe_ref[0]
    h1 = jnp.dot(xe, wg_ref[0].astype(BF16), preferred_element_type=F32)
    h2 = jnp.dot(xe, wu_ref[0].astype(BF16), preferred_element_type=F32)
    hid = (h1 * jax.nn.sigmoid(h1) * h2).astype(BF16)
    part = jnp.dot(hid, wd_ref[0].astype(BF16), preferred_element_type=F32)

    @pl.when(f == 0)
    def _():
        acc_s[...] = part

    @pl.when(f > 0)
    def _():
        acc_s[...] += part

    @pl.when(f == pl.num_programs(2) - 1)
    def _():
        o_ref[0] = (acc_s[...] * g_ref[0]).astype(BF16)


def _ffn_call(xe, g, w_gate, w_up, w_down):
    e, m, d = xe.shape
    dff = w_gate.shape[2]
    tm = min(FFN_TM, m)
    return pl.pallas_call(
        _ffn_kernel,
        grid=(e, m // tm, dff // FFN_TF),
        in_specs=[
            pl.BlockSpec((1, tm, d), lambda ei, mi, fi: (ei, mi, 0)),
            pl.BlockSpec((1, tm, 1), lambda ei, mi, fi: (ei, mi, 0)),
            pl.BlockSpec((1, d, FFN_TF), lambda ei, mi, fi: (ei, 0, fi)),
            pl.BlockSpec((1, d, FFN_TF), lambda ei, mi, fi: (ei, 0, fi)),
            pl.BlockSpec((1, FFN_TF, d), lambda ei, mi, fi: (ei, fi, 0)),
        ],
        out_specs=pl.BlockSpec((1, tm, d), lambda ei, mi, fi: (ei, mi, 0)),
        out_shape=jax.ShapeDtypeStruct((e, m, d), BF16),
        scratch_shapes=[pltpu.VMEM((tm, d), F32)],
        compiler_params=_cparams(("arbitrary", "arbitrary", "arbitrary")),
        name="expert_ffn",
    )(xe, g, w_gate, w_up, w_down)


def _combine_kernel(cnt_ref, ye_ref, rank_ref, x1_ref, g2_ref, fn_ref, o_ref, acc_s, *, cap, n_chunk):
    b = pl.program_id(0)
    c = pl.program_id(1)
    n_e = rank_ref.shape[1]
    tn_dims = (((0,), (0,)), ((), ()))
    wins = _slot_windows(cnt_ref, b, c, n_e, n_chunk, cap)
    hots = [_one_hot_t(rank_ref[0, e:e + 1, :], wins[e][0]).astype(BF16) for e in range(n_e)]
    ys = [ye_ref[e, pl.ds(wins[e][0], ROUTE_WIN), :] for e in range(n_e)]
    moe = lax.dot_general(jnp.concatenate(hots, axis=0), jnp.concatenate(ys, axis=0), tn_dims,
                          preferred_element_type=F32)

    def finish(m):
        x = x1_ref[0] + g2_ref[0] * m
        o_ref[0] = _rms(x) * fn_ref[0]

    any_spill = sum(extra for _, extra in wins) > 0

    @pl.when(jnp.logical_not(any_spill))
    def _():
        finish(moe)

    @pl.when(any_spill)
    def _():
        acc_s[...] = moe
        for e in range(n_e):
            lo_al, extra = wins[e]

            def spill(k, carry, e=e, lo_al=lo_al):
                start, first = _spill_window(k, lo_al, cap)
                hot = _one_hot_t(rank_ref[0, e:e + 1, :], start, first).astype(BF16)
                acc_s[...] += lax.dot_general(hot, ye_ref[e, pl.ds(start, ROUTE_WIN), :], tn_dims,
                                              preferred_element_type=F32)
                return carry

            lax.fori_loop(1, extra + 1, spill, 0)
        finish(acc_s[...])


def _combine_call(cnt_flat, ye, rank_t, x1, g2, fnorm, cap):
    nb, n, d = x1.shape
    n_e = rank_t.shape[1]
    n_chunk = n // ROUTE_CHUNK
    grid_spec = pltpu.PrefetchScalarGridSpec(
        num_scalar_prefetch=1,
        grid=(nb, n_chunk),
        in_specs=[
            pl.BlockSpec((n_e, cap, d), lambda b, c, cnt: (0, b, 0)),
            pl.BlockSpec((1, n_e, ROUTE_CHUNK), lambda b, c, cnt: (b, 0, c)),
            pl.BlockSpec((1, ROUTE_CHUNK, d), lambda b, c, cnt: (b, c, 0)),
            pl.BlockSpec((1, 1, d), lambda b, c, cnt: (b, 0, 0)),
            pl.BlockSpec((1, 1, d), lambda b, c, cnt: (0, 0, 0)),
        ],
        out_specs=pl.BlockSpec((1, ROUTE_CHUNK, d), lambda b, c, cnt: (b, c, 0)),
        scratch_shapes=[pltpu.VMEM((ROUTE_CHUNK, d), F32)],
    )
    return pl.pallas_call(
        functools.partial(_combine_kernel, cap=cap, n_chunk=n_chunk),
        grid_spec=grid_spec,
        out_shape=jax.ShapeDtypeStruct((nb, n, d), F32),
        compiler_params=_cparams(("arbitrary", "arbitrary")),
        name="moe_combine_final_norm",
    )(cnt_flat, ye, rank_t, x1, g2, fnorm.reshape(1, 1, d))


def kernel(x, c, ctx, c_ctx, w_mod, b_mod, w_in, b_in, conv_w, conv_b, lru_wa, lru_ba, lru_wi, lru_bi,
           lru_lambda, na_rpb, w_proj_rnn, w_proj_na, w_out, w_router, w_exp_gate, w_exp_up, w_exp_down,
           final_norm):
    nb, n, d = x.shape
    l = ctx.shape[1]
    assert w_mod.shape[0] == 1, "single-layer problem"
    assert nb == SUBLANES and n % TQ == 0 and l % TQ == 0 and d % LRU_SLAB == 0
    lyr = 0

    cc = jnp.concatenate([c, c_ctx[None], jnp.zeros((2 * SUBLANES - nb - 1, d), F32)], axis=0)
    mod = _mod_call(cc, w_mod[lyr], b_mod[lyr])
    sh1, sc1, g1, sh2, sc2, g2 = [m[:nb, None, :] for m in jnp.split(mod, 6, axis=-1)]
    csh1, csc1 = [jnp.broadcast_to(m[nb:nb + 1, None, :], (nb, 1, d)) for m in jnp.split(mod, 6, axis=-1)[:2]]

    w_in_b = w_in[lyr].astype(BF16)
    b_in_r = b_in[lyr].reshape(1, -1)
    tables = _rope_tables(n)
    gy, xr_t, q, k, v, sgr, sgn = _inproj_call(
        x, sh1, sc1, w_in_b, b_in_r, tables,
        ("gelu", "tok_major", "rope_q", "rope_k", "bf16", "sigmoid", "sigmoid"), "in_proj_latent")
    ctx_cols = np.concatenate([np.arange(d, 2 * d), np.arange(3 * d, 5 * d)])
    xrc_t, kc, vc = _inproj_call(
        ctx, csh1, csc1, w_in_b[:, ctx_cols], b_in_r[:, ctx_cols], tables,
        ("tok_major", "bf16", "bf16"), "in_proj_context")

    gw = [_lru_gate_weights(lru_wa[lyr, dr], lru_ba[lyr, dr], lru_wi[lyr, dr], lru_bi[lyr, dr], lru_lambda[lyr, dr])
          for dr in range(2)]
    zeros_h = jnp.zeros((nb, d), F32)
    _, hc_f = _lru_call(xrc_t, conv_w[lyr], conv_b[lyr], *gw[0], zeros_h, reverse=False, name="rglru_ctx_fwd")
    _, hc_b = _lru_call(xrc_t, conv_w[lyr], conv_b[lyr], *gw[1], zeros_h, reverse=True, name="rglru_ctx_bwd")
    hf_t, _ = _lru_call(xr_t, conv_w[lyr], conv_b[lyr], *gw[0], hc_f, reverse=False, name="rglru_fwd")
    y_rnn, _ = _lru_call(xr_t, conv_w[lyr], conv_b[lyr], *gw[1], hc_b, hf_t, gy, reverse=True, name="rglru_bwd")

    y_na = _na_call(q, k, v, kc, vc, _na_bias_table(na_rpb[lyr]))

    x1, xn2, lg_tiles = _mix_call(y_rnn, y_na, sgr, sgn, x, g1, sh2, sc2,
                                  w_proj_rnn[lyr].astype(BF16), w_proj_na[lyr].astype(BF16),
                                  w_out[lyr].astype(BF16), w_router[lyr].T.astype(BF16))
    n_e = w_router.shape[2]
    lg_t = lg_tiles.reshape(n // TQ, n_e, nb, TQ).transpose(2, 1, 0, 3).reshape(nb, n_e, n)

    cap = EC_CAPACITY * n // N_EXPERTS
    n_chunk = n // ROUTE_CHUNK
    assert cap % SLOT_ALIGN == 0 and cap >= ROUTE_WIN and n % ROUTE_CHUNK == 0
    rank_t, aff_t, cnt = _route_call(lg_t, cap)
    cnt_flat = cnt[:, :, :n_chunk + 1].reshape(-1)
    xe, ge = _dispatch_call(cnt_flat, xn2, rank_t, aff_t, cap)
    ye = _ffn_call(xe, ge, w_exp_gate[lyr], w_exp_up[lyr], w_exp_down[lyr])
    return _combine_call(cnt_flat, ye, rank_t, x1, g2, final_norm, cap)
```

```python
import functools

import numpy as np
import jax
import jax.numpy as jnp
from jax import lax
from jax.experimental import pallas as pl
from jax.experimental.pallas import tpu as pltpu

F32 = jnp.float32
BF16 = jnp.bfloat16

GRID_W = 64
N_HEADS = 16
HEAD_DIM = 64
N_LRU_BLOCKS = 16
CONV_W = 4
LRU_C = 8.0
WIN_R = 8
WIN_C = 16
ROPE_BASE = 10000.0
N_EXPERTS = 16
EC_CAPACITY = 2
EPS = 1e-6
NEG_INF = -1e30

LANES = 128
SUBLANES = 8
VMEM_LIMIT = 56 * 1024 * 1024

TQ = 64
LRU_SLAB = 256
FFN_TM = 512
FFN_TF = 1024
NA_CHUNK = 2
ROUTE_CHUNK = 256
ROUTE_WIN = 64
SLOT_ALIGN = 16
MIN_NORMAL_BITS = 0x00800000


def _cparams(sem):
    return pltpu.CompilerParams(dimension_semantics=sem, vmem_limit_bytes=VMEM_LIMIT)


def _sigmoid(x):
    return 0.5 * jnp.tanh(0.5 * x) + 0.5


def _rms(x):
    return x * lax.rsqrt(jnp.mean(x * x, axis=-1, keepdims=True) + EPS)


def _mod_kernel(c_ref, w_ref, b_ref, o_ref):
    c = c_ref[...]
    s = c * jax.nn.sigmoid(c)
    o_ref[...] = jnp.dot(s, w_ref[...], preferred_element_type=F32) + b_ref[...]


def _mod_call(cc, w_mod, b_mod):
    rows, d = cc.shape
    n_out = w_mod.shape[1]
    tn = 1024
    return pl.pallas_call(
        _mod_kernel,
        grid=(n_out // tn,),
        in_specs=[
            pl.BlockSpec((rows, d), lambda j: (0, 0)),
            pl.BlockSpec((d, tn), lambda j: (0, j)),
            pl.BlockSpec((1, tn), lambda j: (0, j)),
        ],
        out_specs=pl.BlockSpec((rows, tn), lambda j: (0, j)),
        out_shape=jax.ShapeDtypeStruct((rows, n_out), F32),
        compiler_params=_cparams(("arbitrary",)),
        name="adaln_mod",
    )(cc, w_mod, b_mod.reshape(1, n_out))


def _rope(z, cos, sa, sb):
    outs = []
    for g in range(z.shape[1] // LANES):
        zg = z[:, g * LANES:(g + 1) * LANES]
        outs.append(zg * cos + pltpu.roll(zg, 16, 1) * sa + pltpu.roll(zg, LANES - 16, 1) * sb)
    return jnp.concatenate(outs, axis=1)


def _inproj_kernel(*refs, kinds):
    n_out = len(kinds)
    x_ref, sh_ref, sc_ref, w_ref, b_ref, cos_ref, sa_ref, sb_ref = refs[:8]
    out_refs = refs[8:8 + n_out]
    nb, tq, d = x_ref.shape
    rows = nb * tq
    xn = _rms(x_ref[...]) * (1.0 + sc_ref[...]) + sh_ref[...]
    xb = xn.reshape(rows, d).astype(BF16)

    def tiled(t_ref):
        return jnp.broadcast_to(t_ref[...][None], (nb, tq, LANES)).reshape(rows, LANES)

    for g, (kind, o_ref) in enumerate(zip(kinds, out_refs)):
        z = jnp.dot(xb, w_ref[:, g * d:(g + 1) * d], preferred_element_type=F32) + b_ref[:, g * d:(g + 1) * d]
        if kind == "gelu":
            o_ref[...] = jax.nn.gelu(z, approximate=True).astype(BF16).reshape(nb, tq, d)
        elif kind == "tok_major":
            o_ref[...] = jnp.swapaxes(z.reshape(nb, tq, d), 0, 1).reshape(rows, d).astype(BF16)
        elif kind == "rope_q":
            r = _rope(z * (HEAD_DIM ** -0.5), tiled(cos_ref), tiled(sa_ref), tiled(sb_ref))
            o_ref[...] = r.astype(BF16).reshape(nb, tq, d)
        elif kind == "rope_k":
            r = _rope(z, tiled(cos_ref), tiled(sa_ref), tiled(sb_ref))
            o_ref[...] = r.astype(BF16).reshape(nb, tq, d)
        elif kind == "bf16":
            o_ref[...] = z.astype(BF16).reshape(nb, tq, d)
        elif kind == "sigmoid":
            o_ref[...] = _sigmoid(z).astype(BF16).reshape(nb, tq, d)
        else:
            raise ValueError(kind)


def _inproj_call(x, sh, sc, w, b, tables, kinds, name):
    nb, n, d = x.shape
    g = len(kinds)
    assert w.shape == (d, g * d)
    cos, sa, sb = tables
    bat = lambda i: (0, i, 0)
    const3 = lambda i: (0, 0, 0)
    const2 = lambda i: (0, 0)
    out_shapes, out_specs = [], []
    for kind in kinds:
        if kind == "tok_major":
            out_shapes.append(jax.ShapeDtypeStruct((n * nb, d), BF16))
            out_specs.append(pl.BlockSpec((TQ * nb, d), lambda i: (i, 0)))
        else:
            out_shapes.append(jax.ShapeDtypeStruct((nb, n, d), BF16))
            out_specs.append(pl.BlockSpec((nb, TQ, d), bat))
    rows_per_table = cos.shape[0] // TQ
    return pl.pallas_call(
        functools.partial(_inproj_kernel, kinds=kinds),
        grid=(n // TQ,),
        in_specs=[
            pl.BlockSpec((nb, TQ, d), bat),
            pl.BlockSpec(sh.shape, const3),
            pl.BlockSpec(sc.shape, const3),
            pl.BlockSpec(w.shape, const2, pipeline_mode=pl.Buffered(1)),
            pl.BlockSpec(b.shape, const2),
            pl.BlockSpec((TQ, LANES), lambda i: (i % rows_per_table, 0)),
            pl.BlockSpec((TQ, LANES), lambda i: (i % rows_per_table, 0)),
            pl.BlockSpec((TQ, LANES), lambda i: (i % rows_per_table, 0)),
        ],
        out_specs=out_specs,
        out_shape=out_shapes,
        compiler_params=_cparams(("arbitrary",)),
        name=name,
    )(x, sh, sc, w, b, cos, sa, sb)


def _rope_tables(n):
    t = jnp.arange(n)
    row = (t // GRID_W).astype(F32)
    col = (t % GRID_W).astype(F32)
    n_freq = HEAD_DIM // 4
    inv = ROPE_BASE ** (-jnp.arange(n_freq, dtype=F32) / n_freq)
    ang_r = row[:, None] * inv
    ang_c = col[:, None] * inv
    zero = jnp.zeros_like(ang_r)
    cos_h = jnp.concatenate([jnp.cos(ang_r), jnp.cos(ang_r), jnp.cos(ang_c), jnp.cos(ang_c)], axis=1)
    sa_h = jnp.concatenate([zero, jnp.sin(ang_r), zero, jnp.sin(ang_c)], axis=1)
    sb_h = jnp.concatenate([-jnp.sin(ang_r), zero, -jnp.sin(ang_c), zero], axis=1)
    rep = LANES // HEAD_DIM
    return tuple(jnp.tile(a, (1, rep)) for a in (cos_h, sa_h, sb_h))


def _lru_kernel(*refs, tn, nt, reverse, fuse_out):
    if fuse_out:
        (xp_ref, xc_ref, xn_ref, cw_ref, cb_ref, wg_ref, bg_ref, lam_ref, h0_ref, hf_ref, gy_ref,
         out_ref, hfin_ref, a_s, b_s, hcar_s) = refs
    else:
        (xp_ref, xc_ref, xn_ref, cw_ref, cb_ref, wg_ref, bg_ref, lam_ref, h0_ref,
         out_ref, hfin_ref, a_s, b_s, hcar_s) = refs
    i = pl.program_id(0)
    ti = (nt - 1 - i) if reverse else i
    nb, d = hcar_s.shape

    @pl.when(i == 0)
    def _():
        hcar_s[...] = h0_ref[...]

    def tokens(ref):
        return ref[...].astype(F32).reshape(ref.shape[0] // nb, nb, d)

    x = tokens(xc_ref)
    xp = jnp.where(ti == 0, 0.0, tokens(xp_ref))
    xnx = jnp.where(ti == nt - 1, 0.0, tokens(xn_ref)[:1])
    ext = jnp.concatenate([xp, x, xnx], axis=0)
    xc = cb_ref[...]
    for k in range(CONV_W):
        xc = xc + ext[k:k + tn] * cw_ref[k]
    rows = tn * nb
    xc2 = xc.reshape(rows, d)
    for s in range(d // LRU_SLAB):
        sl = slice(s * LRU_SLAB, (s + 1) * LRU_SLAB)
        xs = xc2[:, sl]
        gates = jnp.dot(xs.astype(BF16), wg_ref[s], preferred_element_type=F32) + bg_ref[s]
        r = _sigmoid(gates[:, :LRU_SLAB])
        ig = _sigmoid(gates[:, LRU_SLAB:])
        z = -lam_ref[s]
        softplus = jnp.maximum(z, 0.0) + jnp.log1p(jnp.exp(-jnp.abs(z)))
        log_a = r * (-LRU_C * softplus)
        a = jnp.exp(log_a)
        b = jnp.sqrt(1.0 - a * a) * (ig * xs)
        a_s[:, :, sl] = a.reshape(tn, nb, LRU_SLAB)
        b_s[:, :, sl] = b.reshape(tn, nb, LRU_SLAB)

    unroll = 8

    def step(j, h):
        for k in range(unroll):
            jj = j * unroll + k
            t = (tn - 1 - jj) if reverse else jj
            h = a_s[t] * h + b_s[t]
            a_s[t] = h
        return h

    h = lax.fori_loop(0, tn // unroll, step, hcar_s[...])
    hcar_s[...] = h
    hfin_ref[...] = h
    if fuse_out:
        hs = jnp.swapaxes(tokens(hf_ref) + a_s[...], 0, 1)
        out_ref[...] = (hs * gy_ref[...].astype(F32)).astype(BF16)
    else:
        out_ref[...] = a_s[...].reshape(tn * nb, d).astype(BF16)


def _lru_call(xr_t, conv_w, conv_b, wg, bg, lam, h0, hf=None, gy=None, *, reverse, name):
    nb, d = h0.shape
    n = xr_t.shape[0] // nb
    tn = TQ
    nt = n // tn
    halo = 2
    fuse_out = hf is not None
    tile = (lambda i: nt - 1 - i) if reverse else (lambda i: i)
    n_slab = d // LRU_SLAB
    in_specs = [
        pl.BlockSpec((halo * nb, d), lambda i: (jnp.maximum(tile(i) * (tn // halo) - 1, 0), 0)),
        pl.BlockSpec((tn * nb, d), lambda i: (tile(i), 0)),
        pl.BlockSpec((halo * nb, d), lambda i: (jnp.minimum((tile(i) + 1) * (tn // halo), n // halo - 1), 0)),
        pl.BlockSpec((CONV_W, 1, d), lambda i: (0, 0, 0)),
        pl.BlockSpec((1, d), lambda i: (0, 0)),
        pl.BlockSpec((n_slab, LRU_SLAB, 2 * LRU_SLAB), lambda i: (0, 0, 0)),
        pl.BlockSpec((n_slab, 1, 2 * LRU_SLAB), lambda i: (0, 0, 0)),
        pl.BlockSpec((n_slab, 1, LRU_SLAB), lambda i: (0, 0, 0)),
        pl.BlockSpec((nb, d), lambda i: (0, 0)),
    ]
    args = [xr_t, xr_t, xr_t, conv_w.reshape(CONV_W, 1, d), conv_b.reshape(1, d), wg, bg, lam, h0]
    if fuse_out:
        in_specs += [
            pl.BlockSpec((tn * nb, d), lambda i: (tile(i), 0)),
            pl.BlockSpec((nb, tn, d), lambda i: (0, tile(i), 0)),
        ]
        args += [hf, gy]
        out_shape0 = jax.ShapeDtypeStruct((nb, n, d), BF16)
        out_spec0 = pl.BlockSpec((nb, tn, d), lambda i: (0, tile(i), 0))
    else:
        out_shape0 = jax.ShapeDtypeStruct((n * nb, d), BF16)
        out_spec0 = pl.BlockSpec((tn * nb, d), lambda i: (tile(i), 0))
    return pl.pallas_call(
        functools.partial(_lru_kernel, tn=tn, nt=nt, reverse=reverse, fuse_out=fuse_out),
        grid=(nt,),
        in_specs=in_specs,
        out_specs=[out_spec0, pl.BlockSpec((nb, d), lambda i: (0, 0))],
        out_shape=[out_shape0, jax.ShapeDtypeStruct((nb, d), F32)],
        scratch_shapes=[
            pltpu.VMEM((tn, nb, d), F32),
            pltpu.VMEM((tn, nb, d), F32),
            pltpu.VMEM((nb, d), F32),
        ],
        compiler_params=_cparams(("arbitrary",)),
        name=name,
    )(*args)


def _lru_gate_weights(wa, ba, wi, bi, lam):
    d = wa.shape[0] * wa.shape[1]
    n_slab = d // LRU_SLAB
    eye = jnp.eye(wa.shape[0], dtype=F32)

    def dense_slabs(w):
        full = jnp.einsum("gij,gh->gihj", w, eye).reshape(d, d)
        return jnp.stack([full[s * LRU_SLAB:(s + 1) * LRU_SLAB, s * LRU_SLAB:(s + 1) * LRU_SLAB]
                          for s in range(n_slab)])

    wg = jnp.concatenate([dense_slabs(wa), dense_slabs(wi)], axis=2).astype(BF16)
    bg = jnp.concatenate([ba.reshape(n_slab, 1, LRU_SLAB), bi.reshape(n_slab, 1, LRU_SLAB)], axis=2)
    return wg, bg, lam.reshape(n_slab, 1, LRU_SLAB)


def _na_kernel(q_ref, k_ref, v_ref, kc_ref, vc_ref, tab_ref, o_ref, s_s, sc_s, p_s, pc_s, inv_s, *, rows):
    win = WIN_R * GRID_W
    ch = NA_CHUNK
    n_chunk = rows // ch
    qh = 2 * GRID_W
    lane = lax.broadcasted_iota(jnp.int32, (GRID_W, LANES), 1)
    first_head = lane < HEAD_DIM
    nt_dims = (((1,), (1,)), ((), ()))

    def row_start(r):
        return pl.multiple_of(r * GRID_W, GRID_W)

    def stage1(c, slot):
        qs_all = []
        for u in range(ch):
            r = c * ch + u
            rs = jnp.clip(r - WIN_R // 2, 0, rows - WIN_R)
            q = q_ref[0, pl.ds(row_start(r), GRID_W), :]
            zero = jnp.zeros_like(q)
            qs = jnp.concatenate([jnp.where(first_head, q, zero), jnp.where(first_head, zero, q)], axis=0)
            qs_all.append(qs)
            kl = k_ref[0, pl.ds(row_start(rs), win), :]
            s = lax.dot_general(qs, kl, nt_dims, preferred_element_type=F32)
            d0 = rs - r + (WIN_R - 1)
            for m in range(WIN_R // 2):
                s_s[slot, u, :, m * LANES:(m + 1) * LANES] = s[:, m * LANES:(m + 1) * LANES] + tab_ref[0, d0 + 2 * m]
        sc_s[slot] = lax.dot_general(jnp.concatenate(qs_all, axis=0), kc_ref[0], nt_dims,
                                     preferred_element_type=F32)

    def stage2(slot):
        for u in range(ch):
            s = s_s[slot, u]
            sc = sc_s[slot, u * qh:(u + 1) * qh, :]
            m = jnp.maximum(jnp.max(s, axis=1, keepdims=True), jnp.max(sc, axis=1, keepdims=True))
            p = jnp.exp(s - m)
            pc = jnp.exp(sc - m)
            inv = 1.0 / (jnp.sum(p, axis=1, keepdims=True) + jnp.sum(pc, axis=1, keepdims=True))
            p_s[slot, u] = p.astype(BF16)
            pc_s[slot, u * qh:(u + 1) * qh, :] = pc.astype(BF16)
            inv_s[slot, u] = jnp.broadcast_to(inv, (qh, LANES))

    def stage3(c, slot):
        oc = jnp.dot(pc_s[slot], vc_ref[0], preferred_element_type=F32)
        for u in range(ch):
            r = c * ch + u
            rs = jnp.clip(r - WIN_R // 2, 0, rows - WIN_R)
            vl = v_ref[0, pl.ds(row_start(rs), win), :]
            o = (jnp.dot(p_s[slot, u], vl, preferred_element_type=F32) + oc[u * qh:(u + 1) * qh]) * inv_s[slot, u]
            out = jnp.where(first_head, o[:GRID_W], o[GRID_W:])
            o_ref[0, pl.ds(row_start(r), GRID_W), :] = out.astype(BF16)

    stage1(0, 0)
    stage2(0)
    stage1(1, 1)

    def body(j, carry):
        slot = j % 2
        stage3(j - 2, slot)
        stage2(1 - slot)
        stage1(j, slot)
        return carry

    lax.fori_loop(2, n_chunk, body, 0)
    last = n_chunk % 2
    stage3(n_chunk - 2, last)
    stage2(1 - last)
    stage3(n_chunk - 1, 1 - last)


def _na_call(q, k, v, kc, vc, tab):
    nb, n, d = q.shape
    l = kc.shape[1]
    rows = n // GRID_W
    n_pair = d // LANES
    win = WIN_R * GRID_W
    qh = 2 * GRID_W
    assert rows % NA_CHUNK == 0 and rows // NA_CHUNK >= 2
    blk = lambda b, p: (b, 0, p)
    return pl.pallas_call(
        functools.partial(_na_kernel, rows=rows),
        grid=(nb, n_pair),
        in_specs=[
            pl.BlockSpec((1, n, LANES), blk),
            pl.BlockSpec((1, n, LANES), blk),
            pl.BlockSpec((1, n, LANES), blk),
            pl.BlockSpec((1, l, LANES), blk),
            pl.BlockSpec((1, l, LANES), blk),
            pl.BlockSpec((1,) + tab.shape[1:], lambda b, p: (p, 0, 0, 0)),
        ],
        out_specs=pl.BlockSpec((1, n, LANES), blk),
        out_shape=jax.ShapeDtypeStruct((nb, n, d), BF16),
        scratch_shapes=[
            pltpu.VMEM((2, NA_CHUNK, qh, win), F32),
            pltpu.VMEM((2, NA_CHUNK * qh, l), F32),
            pltpu.VMEM((2, NA_CHUNK, qh, win), BF16),
            pltpu.VMEM((2, NA_CHUNK * qh, l), BF16),
            pltpu.VMEM((2, NA_CHUNK, qh, LANES), F32),
        ],
        compiler_params=_cparams(("arbitrary", "arbitrary")),
        name="neighbourhood_attention",
    )(q, k, v, kc, vc, tab)


def _na_bias_table(rpb):
    qc = np.arange(GRID_W)
    kc = np.arange(GRID_W)
    cstart = np.clip(qc - WIN_C // 2, 0, GRID_W - WIN_C)
    ok = (kc[None, :] >= cstart[:, None]) & (kc[None, :] < cstart[:, None] + WIN_C)
    dc = np.clip(kc[None, :] - qc[:, None], -(WIN_C - 1), WIN_C - 1) + WIN_C - 1
    tab = jnp.where(ok[None, None], rpb[:, :, dc], NEG_INF).astype(F32)
    pair = jnp.concatenate([tab[:, :-1], tab[:, 1:]], axis=-1)
    h = pair.shape[0]
    pair = pair.reshape(h // 2, 2, 2 * WIN_R - 2, GRID_W, 2 * GRID_W)
    return pair.transpose(0, 2, 1, 3, 4).reshape(h // 2, 2 * WIN_R - 2, 2 * GRID_W, 2 * GRID_W)


def _mix_kernel(yr_ref, yn_ref, gr_ref, gn_ref, x_ref, g1_ref, sh2_ref, sc2_ref,
                wpr_ref, wpn_ref, wo_ref, wr_ref, x1_ref, xn2_ref, lg_ref):
    nb, tq, d = x_ref.shape
    rows = nb * tq
    pr = jnp.dot(yr_ref[...].reshape(rows, d), wpr_ref[...], preferred_element_type=F32)
    pn = jnp.dot(yn_ref[...].reshape(rows, d), wpn_ref[...], preferred_element_type=F32)
    mix = gr_ref[...].reshape(rows, d).astype(F32) * pr + gn_ref[...].reshape(rows, d).astype(F32) * pn
    o = jnp.dot(mix.astype(BF16), wo_ref[...], preferred_element_type=F32)
    x1 = x_ref[...] + g1_ref[...] * o.reshape(nb, tq, d)
    x1_ref[...] = x1
    xb = (_rms(x1) * (1.0 + sc2_ref[...]) + sh2_ref[...]).astype(BF16)
    xn2_ref[...] = xb
    lg_ref[0] = lax.dot_general(wr_ref[...], xb.reshape(rows, d), (((1,), (1,)), ((), ())),
                                preferred_element_type=F32)


def _mix_call(yr, yn, gr, gn, x, g1, sh2, sc2, wpr, wpn, wo, wr):
    nb, n, d = x.shape
    bat = lambda i: (0, i, 0)
    const3 = lambda i: (0, 0, 0)
    const2 = lambda i: (0, 0)
    act = pl.BlockSpec((nb, TQ, d), bat)
    vec = pl.BlockSpec((nb, 1, d), const3)
    wsp = lambda w: pl.BlockSpec(w.shape, const2, pipeline_mode=pl.Buffered(1))
    return pl.pallas_call(
        _mix_kernel,
        grid=(n // TQ,),
        in_specs=[act, act, act, act, act, vec, vec, vec, wsp(wpr), wsp(wpn), wsp(wo), wsp(wr)],
        out_specs=[act, act, pl.BlockSpec((1, wr.shape[0], nb * TQ), lambda i: (i, 0, 0))],
        out_shape=[
            jax.ShapeDtypeStruct((nb, n, d), F32),
            jax.ShapeDtypeStruct((nb, n, d), BF16),
            jax.ShapeDtypeStruct((n // TQ, wr.shape[0], nb * TQ), F32),
        ],
        compiler_params=_cparams(("arbitrary",)),
        name="merge_out_norm_router",
    )(yr, yn, gr, gn, x, g1, sh2, sc2, wpr, wpn, wo, wr)


def _route_kernel(lg_ref, rank_ref, aff_ref, cnt_ref, *, cap):
    lg = lg_ref[0]
    n_e, n = lg.shape
    ex = jnp.exp(lg - jnp.max(lg, axis=0, keepdims=True))
    aff = ex / jnp.sum(ex, axis=0, keepdims=True)
    aff_ref[0] = aff

    def as_float(bits):
        return lax.bitcast_convert_type(bits, F32)

    thr = jnp.zeros((n_e, 1), jnp.int32)
    for bit in range(30, -1, -1):
        cand = thr | (1 << bit)
        cnt = jnp.sum((aff >= as_float(cand)).astype(F32), axis=1, keepdims=True)
        thr = jnp.where(cnt >= cap, cand, thr)
    thr = jnp.where(thr < MIN_NORMAL_BITS, 0, thr)
    gt = aff >= as_float(jnp.where(thr == 0, MIN_NORMAL_BITS, thr + 1))
    eq = (aff >= as_float(thr)) & jnp.logical_not(gt)
    need = cap - jnp.sum(gt.astype(F32), axis=1, keepdims=True)

    blk = ROUTE_CHUNK
    row = lax.broadcasted_iota(jnp.int32, (blk, blk), 0)
    col = lax.broadcasted_iota(jnp.int32, (blk, blk), 1)
    upper = (row <= col).astype(BF16)

    def cumsum_tokens(mask):
        off = jnp.zeros((n_e, 1), F32)
        outs, offs = [], []
        for c in range(n // blk):
            x = mask[:, c * blk:(c + 1) * blk].astype(BF16)
            cs = jnp.dot(x, upper, preferred_element_type=F32) + off
            offs.append(off)
            outs.append(cs)
            off = cs[:, blk - 1:blk]
        offs.append(off)
        return jnp.concatenate(outs, axis=1), offs

    cum_eq, _ = cumsum_tokens(eq)
    sel = gt | (eq & ((cum_eq - eq.astype(F32)) < need))
    cum_sel, offs = cumsum_tokens(sel)
    rank_ref[0] = jnp.where(sel, cum_sel - 1.0, -1.0).astype(jnp.int32)
    pad = jnp.zeros((n_e, LANES - len(offs)), F32)
    cnt_ref[0] = jnp.concatenate(offs + [pad], axis=1).astype(jnp.int32)


def _route_call(lg_t, cap):
    nb, n_e, n = lg_t.shape
    blk3 = lambda b: (b, 0, 0)
    return pl.pallas_call(
        functools.partial(_route_kernel, cap=cap),
        grid=(nb,),
        in_specs=[pl.BlockSpec((1, n_e, n), blk3)],
        out_specs=[pl.BlockSpec((1, n_e, n), blk3), pl.BlockSpec((1, n_e, n), blk3),
                   pl.BlockSpec((1, n_e, LANES), blk3)],
        out_shape=[jax.ShapeDtypeStruct((nb, n_e, n), jnp.int32), jax.ShapeDtypeStruct((nb, n_e, n), F32),
                   jax.ShapeDtypeStruct((nb, n_e, LANES), jnp.int32)],
        compiler_params=_cparams(("arbitrary",)),
        name="route_select",
    )(lg_t)


def _slot_windows(cnt_ref, b, c, n_e, n_chunk, cap):
    wins = []
    for e in range(n_e):
        base = (b * n_e + e) * (n_chunk + 1) + c
        lo = cnt_ref[base]
        hi = cnt_ref[base + 1]
        lo_al = jnp.minimum((lo // SLOT_ALIGN) * SLOT_ALIGN, cap - ROUTE_WIN)
        extra = jnp.maximum((hi - lo_al + ROUTE_WIN - 1) // ROUTE_WIN - 1, 0)
        wins.append((pl.multiple_of(lo_al, SLOT_ALIGN), extra))
    return wins


def _one_hot_t(rank_row, start, first=None):
    slot = start + lax.broadcasted_iota(jnp.int32, (ROUTE_WIN, rank_row.shape[1]), 0)
    hit = rank_row == slot
    if first is not None:
        hit = hit & (slot >= first)
    return hit


def _spill_window(k, lo_al, cap):
    first = lo_al + k * ROUTE_WIN
    start = pl.multiple_of(jnp.minimum(first, cap - ROUTE_WIN), SLOT_ALIGN)
    return start, first


def _dispatch_kernel(cnt_ref, x_ref, rank_ref, aff_ref, xe_ref, g_ref, *, cap, n_chunk):
    b = pl.program_id(0)
    c = pl.program_id(1)
    n_e = rank_ref.shape[1]

    @pl.when(c == 0)
    def _():
        xe_ref[...] = jnp.zeros_like(xe_ref)
        g_ref[...] = jnp.zeros_like(g_ref)

    x = x_ref[0]
    wins = _slot_windows(cnt_ref, b, c, n_e, n_chunk, cap)
    hots = [_one_hot_t(rank_ref[0, e:e + 1, :], wins[e][0]) for e in range(n_e)]
    stacked = jnp.concatenate([h.astype(BF16) for h in hots], axis=0)
    rows = jnp.dot(stacked, x, preferred_element_type=F32)
    for e in range(n_e):
        sl = pl.ds(wins[e][0], ROUTE_WIN)
        xe_ref[e, sl, :] = xe_ref[e, sl, :] + rows[e * ROUTE_WIN:(e + 1) * ROUTE_WIN].astype(BF16)
        g_ref[e, sl, :] = g_ref[e, sl, :] + jnp.sum(jnp.where(hots[e], aff_ref[0, e:e + 1, :], 0.0),
                                                    axis=1, keepdims=True)

    @pl.when(sum(extra for _, extra in wins) > 0)
    def _():
        for e in range(n_e):
            lo_al, extra = wins[e]

            def spill(k, carry, e=e, lo_al=lo_al):
                start, first = _spill_window(k, lo_al, cap)
                hot = _one_hot_t(rank_ref[0, e:e + 1, :], start, first)
                sl2 = pl.ds(start, ROUTE_WIN)
                xe_ref[e, sl2, :] = xe_ref[e, sl2, :] + jnp.dot(hot.astype(BF16), x_ref[0],
                                                               preferred_element_type=F32).astype(BF16)
                g_ref[e, sl2, :] = g_ref[e, sl2, :] + jnp.sum(jnp.where(hot, aff_ref[0, e:e + 1, :], 0.0),
                                                              axis=1, keepdims=True)
                return carry

            lax.fori_loop(1, extra + 1, spill, 0)


def _dispatch_call(cnt_flat, xn2, rank_t, aff_t, cap):
    nb, n, d = xn2.shape
    n_e = rank_t.shape[1]
    n_chunk = n // ROUTE_CHUNK
    grid_spec = pltpu.PrefetchScalarGridSpec(
        num_scalar_prefetch=1,
        grid=(nb, n_chunk),
        in_specs=[
            pl.BlockSpec((1, ROUTE_CHUNK, d), lambda b, c, cnt: (b, c, 0)),
            pl.BlockSpec((1, n_e, ROUTE_CHUNK), lambda b, c, cnt: (b, 0, c)),
            pl.BlockSpec((1, n_e, ROUTE_CHUNK), lambda b, c, cnt: (b, 0, c)),
        ],
        out_specs=[
            pl.BlockSpec((n_e, cap, d), lambda b, c, cnt: (0, b, 0)),
            pl.BlockSpec((n_e, cap, 1), lambda b, c, cnt: (0, b, 0)),
        ],
    )
    return pl.pallas_call(
        functools.partial(_dispatch_kernel, cap=cap, n_chunk=n_chunk),
        grid_spec=grid_spec,
        out_shape=[jax.ShapeDtypeStruct((n_e, nb * cap, d), BF16), jax.ShapeDtypeStruct((n_e, nb * cap, 1), F32)],
        compiler_params=_cparams(("arbitrary", "arbitrary")),
        name="moe_dispatch",
    )(cnt_flat, xn2, rank_t, aff_t)


def _ffn_kernel(xe_ref, g_ref, wg_ref, wu_ref, wd_ref, o_ref, acc_s):
    f = pl.program_id(1)
    m = pl.program_id(2)
    tm = xe_ref.shape[1]
    xe = xe_ref[0]
    h1 = jnp.dot(xe, wg_ref[0].astype(BF16), preferred_element_type=F32)
    h2 = jnp.dot(xe, wu_ref[0].astype(BF16), preferred_element_type=F32)
    hid = (h1 * _sigmoid(h1) * h2).astype(BF16)
    part = jnp.dot(hid, wd_ref[0].astype(BF16), preferred_element_type=F32)
    rows = pl.ds(pl.multiple_of(m * tm, tm), tm)
    last = pl.num_programs(1) - 1

    @pl.when(f == 0)
    def _():
        acc_s[rows, :] = part

    @pl.when((f > 0) & (f < last))
    def _():
        acc_s[rows, :] += part

    @pl.when(f == last)
    def _():
        o_ref[0] = ((acc_s[rows, :] + part) * g_ref[0]).astype(BF16)


def _ffn_call(xe, g, w_gate, w_up, w_down):
    e, m, d = xe.shape
    dff = w_gate.shape[2]
    tm = min(FFN_TM, m)
    n_f = dff // FFN_TF
    assert n_f >= 2 and m % tm == 0
    out_row = lambda fi, mi: jnp.where(fi == n_f - 1, mi, 0)
    return pl.pallas_call(
        _ffn_kernel,
        grid=(e, n_f, m // tm),
        in_specs=[
            pl.BlockSpec((1, tm, d), lambda ei, fi, mi: (ei, mi, 0)),
            pl.BlockSpec((1, tm, 1), lambda ei, fi, mi: (ei, mi, 0)),
            pl.BlockSpec((1, d, FFN_TF), lambda ei, fi, mi: (ei, 0, fi)),
            pl.BlockSpec((1, d, FFN_TF), lambda ei, fi, mi: (ei, 0, fi)),
            pl.BlockSpec((1, FFN_TF, d), lambda ei, fi, mi: (ei, fi, 0)),
        ],
        out_specs=pl.BlockSpec((1, tm, d), lambda ei, fi, mi: (ei, out_row(fi, mi), 0)),
        out_shape=jax.ShapeDtypeStruct((e, m, d), BF16),
        scratch_shapes=[pltpu.VMEM((m, d), F32)],
        compiler_params=_cparams(("arbitrary", "arbitrary", "arbitrary")),
        name="expert_ffn",
    )(xe, g, w_gate, w_up, w_down)


def _combine_kernel(cnt_ref, ye_ref, rank_ref, x1_ref, g2_ref, fn_ref, o_ref, acc_s, *, cap, n_chunk):
    b = pl.program_id(0)
    c = pl.program_id(1)
    n_e = rank_ref.shape[1]
    tn_dims = (((0,), (0,)), ((), ()))
    wins = _slot_windows(cnt_ref, b, c, n_e, n_chunk, cap)
    hots = [_one_hot_t(rank_ref[0, e:e + 1, :], wins[e][0]).astype(BF16) for e in range(n_e)]
    ys = [ye_ref[e, pl.ds(wins[e][0], ROUTE_WIN), :] for e in range(n_e)]
    moe = lax.dot_general(jnp.concatenate(hots, axis=0), jnp.concatenate(ys, axis=0), tn_dims,
                          preferred_element_type=F32)

    def finish(m):
        x = x1_ref[0] + g2_ref[0] * m
        o_ref[0] = _rms(x) * fn_ref[0]

    any_spill = sum(extra for _, extra in wins) > 0

    @pl.when(jnp.logical_not(any_spill))
    def _():
        finish(moe)

    @pl.when(any_spill)
    def _():
        acc_s[...] = moe
        for e in range(n_e):
            lo_al, extra = wins[e]

            def spill(k, carry, e=e, lo_al=lo_al):
                start, first = _spill_window(k, lo_al, cap)
                hot = _one_hot_t(rank_ref[0, e:e + 1, :], start, first).astype(BF16)
                acc_s[...] += lax.dot_general(hot, ye_ref[e, pl.ds(start, ROUTE_WIN), :], tn_dims,
                                              preferred_element_type=F32)
                return carry

            lax.fori_loop(1, extra + 1, spill, 0)
        finish(acc_s[...])


def _combine_call(cnt_flat, ye, rank_t, x1, g2, fnorm, cap):
    nb, n, d = x1.shape
    n_e = rank_t.shape[1]
    n_chunk = n // ROUTE_CHUNK
    grid_spec = pltpu.PrefetchScalarGridSpec(
        num_scalar_prefetch=1,
        grid=(nb, n_chunk),
        in_specs=[
            pl.BlockSpec((n_e, cap, d), lambda b, c, cnt: (0, b, 0)),
            pl.BlockSpec((1, n_e, ROUTE_CHUNK), lambda b, c, cnt: (b, 0, c)),
            pl.BlockSpec((1, ROUTE_CHUNK, d), lambda b, c, cnt: (b, c, 0)),
            pl.BlockSpec((1, 1, d), lambda b, c, cnt: (b, 0, 0)),
            pl.BlockSpec((1, 1, d), lambda b, c, cnt: (0, 0, 0)),
        ],
        out_specs=pl.BlockSpec((1, ROUTE_CHUNK, d), lambda b, c, cnt: (b, c, 0)),
        scratch_shapes=[pltpu.VMEM((ROUTE_CHUNK, d), F32)],
    )
    return pl.pallas_call(
        functools.partial(_combine_kernel, cap=cap, n_chunk=n_chunk),
        grid_spec=grid_spec,
        out_shape=jax.ShapeDtypeStruct((nb, n, d), F32),
        compiler_params=_cparams(("arbitrary", "arbitrary")),
        name="moe_combine_final_norm",
    )(cnt_flat, ye, rank_t, x1, g2, fnorm.reshape(1, 1, d))


def kernel(x, c, ctx, c_ctx, w_mod, b_mod, w_in, b_in, conv_w, conv_b, lru_wa, lru_ba, lru_wi, lru_bi,
           lru_lambda, na_rpb, w_proj_rnn, w_proj_na, w_out, w_router, w_exp_gate, w_exp_up, w_exp_down,
           final_norm):
    nb, n, d = x.shape
    l = ctx.shape[1]
    assert w_mod.shape[0] == 1, "single-layer problem"
    assert nb == SUBLANES and n % TQ == 0 and l % TQ == 0 and d % LRU_SLAB == 0
    lyr = 0

    cc = jnp.concatenate([c, c_ctx[None], jnp.zeros((2 * SUBLANES - nb - 1, d), F32)], axis=0)
    mod = _mod_call(cc, w_mod[lyr], b_mod[lyr])
    sh1, sc1, g1, sh2, sc2, g2 = [m[:nb, None, :] for m in jnp.split(mod, 6, axis=-1)]
    csh1, csc1 = [jnp.broadcast_to(m[nb:nb + 1, None, :], (nb, 1, d)) for m in jnp.split(mod, 6, axis=-1)[:2]]

    w_in_b = w_in[lyr].astype(BF16)
    b_in_r = b_in[lyr].reshape(1, -1)
    tables = _rope_tables(n)
    gy, xr_t, q, k, v, sgr, sgn = _inproj_call(
        x, sh1, sc1, w_in_b, b_in_r, tables,
        ("gelu", "tok_major", "rope_q", "rope_k", "bf16", "sigmoid", "sigmoid"), "in_proj_latent")
    ctx_cols = np.concatenate([np.arange(d, 2 * d), np.arange(3 * d, 5 * d)])
    xrc_t, kc, vc = _inproj_call(
        ctx, csh1, csc1, w_in_b[:, ctx_cols], b_in_r[:, ctx_cols], tables,
        ("tok_major", "bf16", "bf16"), "in_proj_context")

    gw = [_lru_gate_weights(lru_wa[lyr, dr], lru_ba[lyr, dr], lru_wi[lyr, dr], lru_bi[lyr, dr], lru_lambda[lyr, dr])
          for dr in range(2)]
    zeros_h = jnp.zeros((nb, d), F32)
    _, hc_f = _lru_call(xrc_t, conv_w[lyr], conv_b[lyr], *gw[0], zeros_h, reverse=False, name="rglru_ctx_fwd")
    _, hc_b = _lru_call(xrc_t, conv_w[lyr], conv_b[lyr], *gw[1], zeros_h, reverse=True, name="rglru_ctx_bwd")
    hf_t, _ = _lru_call(xr_t, conv_w[lyr], conv_b[lyr], *gw[0], hc_f, reverse=False, name="rglru_fwd")
    y_rnn, _ = _lru_call(xr_t, conv_w[lyr], conv_b[lyr], *gw[1], hc_b, hf_t, gy, reverse=True, name="rglru_bwd")

    y_na = _na_call(q, k, v, kc, vc, _na_bias_table(na_rpb[lyr]))

    x1, xn2, lg_tiles = _mix_call(y_rnn, y_na, sgr, sgn, x, g1, sh2, sc2,
                                  w_proj_rnn[lyr].astype(BF16), w_proj_na[lyr].astype(BF16),
                                  w_out[lyr].astype(BF16), w_router[lyr].T.astype(BF16))
    n_e = w_router.shape[2]
    lg_t = lg_tiles.reshape(n // TQ, n_e, nb, TQ).transpose(2, 1, 0, 3).reshape(nb, n_e, n)

    cap = EC_CAPACITY * n // N_EXPERTS
    n_chunk = n // ROUTE_CHUNK
    assert cap % SLOT_ALIGN == 0 and cap >= ROUTE_WIN and n % ROUTE_CHUNK == 0
    rank_t, aff_t, cnt = _route_call(lg_t, cap)
    cnt_flat = cnt[:, :, :n_chunk + 1].reshape(-1)
    xe, ge = _dispatch_call(cnt_flat, xn2, rank_t, aff_t, cap)
    ye = _ffn_call(xe, ge, w_exp_gate[lyr], w_exp_up[lyr], w_exp_down[lyr])
    return _combine_call(cnt_flat, ye, rank_t, x1, g2, final_norm, cap)
```

```python
import functools

import numpy as np
import jax
import jax.numpy as jnp
from jax import lax
from jax.experimental import pallas as pl
from jax.experimental.pallas import tpu as pltpu

F32 = jnp.float32
BF16 = jnp.bfloat16

GRID_W = 64
N_HEADS = 16
HEAD_DIM = 64
N_LRU_BLOCKS = 16
CONV_W = 4
LRU_C = 8.0
WIN_R = 8
WIN_C = 16
ROPE_BASE = 10000.0
N_EXPERTS = 16
EC_CAPACITY = 2
EPS = 1e-6
NEG_INF = -1e30

LANES = 128
SUBLANES = 8
VMEM_LIMIT = 56 * 1024 * 1024

TQ = 64
LRU_SLAB = 256
LRU_TN = 128
FFN_TM = 512
FFN_TF = 1024
NA_CHUNK = 2
ROUTE_CHUNK = 256
ROUTE_WIN = 64
SLOT_ALIGN = 16
MIN_NORMAL_BITS = 0x00800000


def _cparams(sem):
    return pltpu.CompilerParams(dimension_semantics=sem, vmem_limit_bytes=VMEM_LIMIT)


def _sigmoid(x):
    return 0.5 * jnp.tanh(0.5 * x) + 0.5


def _rms(x):
    return x * lax.rsqrt(jnp.mean(x * x, axis=-1, keepdims=True) + EPS)


def _mod_kernel(c_ref, w_ref, b_ref, o_ref):
    c = c_ref[...]
    s = c * jax.nn.sigmoid(c)
    o_ref[...] = jnp.dot(s, w_ref[...], preferred_element_type=F32) + b_ref[...]


def _mod_call(cc, w_mod, b_mod):
    rows, d = cc.shape
    n_out = w_mod.shape[1]
    tn = 1024
    return pl.pallas_call(
        _mod_kernel,
        grid=(n_out // tn,),
        in_specs=[
            pl.BlockSpec((rows, d), lambda j: (0, 0)),
            pl.BlockSpec((d, tn), lambda j: (0, j)),
            pl.BlockSpec((1, tn), lambda j: (0, j)),
        ],
        out_specs=pl.BlockSpec((rows, tn), lambda j: (0, j)),
        out_shape=jax.ShapeDtypeStruct((rows, n_out), F32),
        compiler_params=_cparams(("arbitrary",)),
        name="adaln_mod",
    )(cc, w_mod, b_mod.reshape(1, n_out))


def _rope(z, cos, sa, sb):
    outs = []
    for g in range(z.shape[1] // LANES):
        zg = z[:, g * LANES:(g + 1) * LANES]
        outs.append(zg * cos + pltpu.roll(zg, 16, 1) * sa + pltpu.roll(zg, LANES - 16, 1) * sb)
    return jnp.concatenate(outs, axis=1)


def _inproj_kernel(*refs, kinds, groups):
    n_out = len(kinds)
    x_ref, sh_ref, sc_ref, w_ref, b_ref, cos_ref, sa_ref, sb_ref = refs[:8]
    out_refs = refs[8:8 + n_out]
    nb, tq, d = x_ref.shape
    rows = nb * tq
    xn = _rms(x_ref[...]) * (1.0 + sc_ref[...]) + sh_ref[...]
    xb = xn.reshape(rows, d).astype(BF16)

    def tiled(t_ref):
        return jnp.broadcast_to(t_ref[...][None], (nb, tq, LANES)).reshape(rows, LANES)

    for g, kind, o_ref in zip(groups, kinds, out_refs):
        z = jnp.dot(xb, w_ref[:, g * d:(g + 1) * d], preferred_element_type=F32) + b_ref[:, g * d:(g + 1) * d]
        if kind == "gelu":
            o_ref[...] = jax.nn.gelu(z, approximate=True).astype(BF16).reshape(nb, tq, d)
        elif kind == "tok_major":
            o_ref[...] = jnp.swapaxes(z.reshape(nb, tq, d), 0, 1).reshape(rows, d).astype(BF16)
        elif kind == "rope_q":
            r = _rope(z * (HEAD_DIM ** -0.5), tiled(cos_ref), tiled(sa_ref), tiled(sb_ref))
            o_ref[...] = r.astype(BF16).reshape(nb, tq, d)
        elif kind == "rope_k":
            r = _rope(z, tiled(cos_ref), tiled(sa_ref), tiled(sb_ref))
            o_ref[...] = r.astype(BF16).reshape(o_ref.shape)
        elif kind == "bf16":
            o_ref[...] = z.astype(BF16).reshape(o_ref.shape)
        elif kind == "sigmoid":
            o_ref[...] = _sigmoid(z).astype(BF16).reshape(nb, tq, d)
        else:
            raise ValueError(kind)


def _inproj_call(x, sh, sc, w, b, tables, kinds, groups, name, col_blocked=()):
    nb, n, d = x.shape
    assert len(groups) == len(kinds) and w.shape[0] == d and max(groups) * d < w.shape[1]
    cos, sa, sb = tables
    bat = lambda i: (0, i, 0)
    const3 = lambda i: (0, 0, 0)
    const2 = lambda i: (0, 0)
    out_shapes, out_specs = [], []
    for kind in kinds:
        if kind == "tok_major":
            out_shapes.append(jax.ShapeDtypeStruct((n * nb, d), BF16))
            out_specs.append(pl.BlockSpec((TQ * nb, d), lambda i: (i, 0)))
        elif len(out_shapes) in col_blocked:
            assert TQ == GRID_W
            n_cb = GRID_W // WIN_C
            out_shapes.append(jax.ShapeDtypeStruct((nb, n_cb, n // n_cb, d), BF16))
            out_specs.append(pl.BlockSpec((nb, n_cb, WIN_C, d), lambda i: (0, 0, i, 0)))
        else:
            out_shapes.append(jax.ShapeDtypeStruct((nb, n, d), BF16))
            out_specs.append(pl.BlockSpec((nb, TQ, d), bat))
    rows_per_table = cos.shape[0] // TQ
    return pl.pallas_call(
        functools.partial(_inproj_kernel, kinds=kinds, groups=groups),
        grid=(n // TQ,),
        in_specs=[
            pl.BlockSpec((nb, TQ, d), bat),
            pl.BlockSpec(sh.shape, const3),
            pl.BlockSpec(sc.shape, const3),
            pl.BlockSpec(w.shape, const2, pipeline_mode=pl.Buffered(1)),
            pl.BlockSpec(b.shape, const2),
            pl.BlockSpec((TQ, LANES), lambda i: (i % rows_per_table, 0)),
            pl.BlockSpec((TQ, LANES), lambda i: (i % rows_per_table, 0)),
            pl.BlockSpec((TQ, LANES), lambda i: (i % rows_per_table, 0)),
        ],
        out_specs=out_specs,
        out_shape=out_shapes,
        compiler_params=_cparams(("arbitrary",)),
        name=name,
    )(x, sh, sc, w, b, cos, sa, sb)


def _rope_tables(n):
    t = np.arange(n)
    row = (t // GRID_W).astype(np.float32)
    col = (t % GRID_W).astype(np.float32)
    n_freq = HEAD_DIM // 4
    inv = (np.float32(ROPE_BASE) ** (-np.arange(n_freq, dtype=np.float32) / np.float32(n_freq))).astype(np.float32)
    ang_r = (row[:, None] * inv).astype(np.float32)
    ang_c = (col[:, None] * inv).astype(np.float32)
    zero = np.zeros_like(ang_r)
    cos_h = np.concatenate([np.cos(ang_r), np.cos(ang_r), np.cos(ang_c), np.cos(ang_c)], axis=1)
    sa_h = np.concatenate([zero, np.sin(ang_r), zero, np.sin(ang_c)], axis=1)
    sb_h = np.concatenate([-np.sin(ang_r), zero, -np.sin(ang_c), zero], axis=1)
    rep = LANES // HEAD_DIM
    return tuple(jnp.asarray(np.tile(a.astype(np.float32), (1, rep))) for a in (cos_h, sa_h, sb_h))


def _lru_kernel(*refs, tn, nt, reverse, fuse_out):
    if fuse_out:
        (xp_ref, xc_ref, xn_ref, cw_ref, cb_ref, wg_ref, bg_ref, lam_ref, h0_ref, hf_ref, gy_ref,
         out_ref, hfin_ref, a_s, b_s, hcar_s) = refs
    else:
        (xp_ref, xc_ref, xn_ref, cw_ref, cb_ref, wg_ref, bg_ref, lam_ref, h0_ref,
         out_ref, hfin_ref, a_s, b_s, hcar_s) = refs
    i = pl.program_id(0)
    ti = (nt - 1 - i) if reverse else i
    nb, d = hcar_s.shape

    @pl.when(i == 0)
    def _():
        hcar_s[...] = h0_ref[...]

    def tokens(ref):
        return ref[...].astype(F32).reshape(ref.shape[0] // nb, nb, d)

    x = tokens(xc_ref)
    xp = jnp.where(ti == 0, 0.0, tokens(xp_ref))
    xnx = jnp.where(ti == nt - 1, 0.0, tokens(xn_ref)[:1])
    ext = jnp.concatenate([xp, x, xnx], axis=0)
    xc = cb_ref[...]
    for k in range(CONV_W):
        xc = xc + ext[k:k + tn] * cw_ref[k]
    rows = tn * nb
    xc2 = xc.reshape(rows, d)
    for s in range(d // LRU_SLAB):
        sl = slice(s * LRU_SLAB, (s + 1) * LRU_SLAB)
        xs = xc2[:, sl]
        gates = jnp.dot(xs.astype(BF16), wg_ref[s], preferred_element_type=F32) + bg_ref[s]
        r = _sigmoid(gates[:, :LRU_SLAB])
        ig = _sigmoid(gates[:, LRU_SLAB:])
        z = -lam_ref[s]
        softplus = jnp.maximum(z, 0.0) + jnp.log1p(jnp.exp(-jnp.abs(z)))
        log_a = r * (-LRU_C * softplus)
        a = jnp.exp(log_a)
        b = jnp.sqrt(1.0 - a * a) * (ig * xs)
        a_s[:, :, sl] = a.reshape(tn, nb, LRU_SLAB)
        b_s[:, :, sl] = b.reshape(tn, nb, LRU_SLAB)

    unroll = 8

    def step(j, h):
        for k in range(unroll):
            jj = j * unroll + k
            t = (tn - 1 - jj) if reverse else jj
            h = a_s[t] * h + b_s[t]
            a_s[t] = h
        return h

    h = lax.fori_loop(0, tn // unroll, step, hcar_s[...])
    hcar_s[...] = h
    hfin_ref[...] = h
    if fuse_out:
        hs = jnp.swapaxes(tokens(hf_ref) + a_s[...], 0, 1)
        out_ref[...] = (hs * gy_ref[...].astype(F32)).astype(BF16)
    else:
        out_ref[...] = a_s[...].reshape(tn * nb, d).astype(BF16)


def _lru_call(xr_t, conv_w, conv_b, wg, bg, lam, h0, hf=None, gy=None, *, reverse, name):
    nb, d = h0.shape
    n = xr_t.shape[0] // nb
    tn = LRU_TN
    nt = n // tn
    halo = 2
    fuse_out = hf is not None
    tile = (lambda i: nt - 1 - i) if reverse else (lambda i: i)
    n_slab = d // LRU_SLAB
    in_specs = [
        pl.BlockSpec((halo * nb, d), lambda i: (jnp.maximum(tile(i) * (tn // halo) - 1, 0), 0)),
        pl.BlockSpec((tn * nb, d), lambda i: (tile(i), 0)),
        pl.BlockSpec((halo * nb, d), lambda i: (jnp.minimum((tile(i) + 1) * (tn // halo), n // halo - 1), 0)),
        pl.BlockSpec((CONV_W, 1, d), lambda i: (0, 0, 0)),
        pl.BlockSpec((1, d), lambda i: (0, 0)),
        pl.BlockSpec((n_slab, LRU_SLAB, 2 * LRU_SLAB), lambda i: (0, 0, 0)),
        pl.BlockSpec((n_slab, 1, 2 * LRU_SLAB), lambda i: (0, 0, 0)),
        pl.BlockSpec((n_slab, 1, LRU_SLAB), lambda i: (0, 0, 0)),
        pl.BlockSpec((nb, d), lambda i: (0, 0)),
    ]
    args = [xr_t, xr_t, xr_t, conv_w.reshape(CONV_W, 1, d), conv_b.reshape(1, d), wg, bg, lam, h0]
    if fuse_out:
        in_specs += [
            pl.BlockSpec((tn * nb, d), lambda i: (tile(i), 0)),
            pl.BlockSpec((nb, tn, d), lambda i: (0, tile(i), 0)),
        ]
        args += [hf, gy]
        out_shape0 = jax.ShapeDtypeStruct((nb, n, d), BF16)
        out_spec0 = pl.BlockSpec((nb, tn, d), lambda i: (0, tile(i), 0))
    else:
        out_shape0 = jax.ShapeDtypeStruct((n * nb, d), BF16)
        out_spec0 = pl.BlockSpec((tn * nb, d), lambda i: (tile(i), 0))
    return pl.pallas_call(
        functools.partial(_lru_kernel, tn=tn, nt=nt, reverse=reverse, fuse_out=fuse_out),
        grid=(nt,),
        in_specs=in_specs,
        out_specs=[out_spec0, pl.BlockSpec((nb, d), lambda i: (0, 0))],
        out_shape=[out_shape0, jax.ShapeDtypeStruct((nb, d), F32)],
        scratch_shapes=[
            pltpu.VMEM((tn, nb, d), F32),
            pltpu.VMEM((tn, nb, d), F32),
            pltpu.VMEM((nb, d), F32),
        ],
        compiler_params=_cparams(("arbitrary",)),
        name=name,
    )(*args)


def _lru_gate_weights(wa, ba, wi, bi, lam):
    d = wa.shape[0] * wa.shape[1]
    n_slab = d // LRU_SLAB
    eye = jnp.eye(wa.shape[0], dtype=F32)

    def dense_slabs(w):
        full = jnp.einsum("gij,gh->gihj", w, eye).reshape(d, d)
        return jnp.stack([full[s * LRU_SLAB:(s + 1) * LRU_SLAB, s * LRU_SLAB:(s + 1) * LRU_SLAB]
                          for s in range(n_slab)])

    wg = jnp.concatenate([dense_slabs(wa), dense_slabs(wi)], axis=2).astype(BF16)
    bg = jnp.concatenate([ba.reshape(n_slab, 1, LRU_SLAB), bi.reshape(n_slab, 1, LRU_SLAB)], axis=2)
    return wg, bg, lam.reshape(n_slab, 1, LRU_SLAB)


def _na_blocks(g):
    lo = min(max(8 * g - WIN_C // 2, 0) // WIN_C, GRID_W // WIN_C - 2)
    return (lo, lo + 1)


def _na_kernel(q_ref, k_ref, v_ref, kc_ref, vc_ref, tab_ref, o_ref, s_s, sc_s, p_s, pc_s, inv_s, *, rows):
    win = WIN_R * GRID_W
    ch = NA_CHUNK
    n_chunk = rows // ch
    qh = 2 * GRID_W
    n_cb = GRID_W // WIN_C
    blk = WIN_R * WIN_C
    grp = SUBLANES
    lane = lax.broadcasted_iota(jnp.int32, (GRID_W, LANES), 1)
    first_head = lane < HEAD_DIM
    nt_dims = (((1,), (1,)), ((), ()))

    def row_start(r):
        return pl.multiple_of(r * GRID_W, GRID_W)

    def window(ref, rs):
        run = pl.ds(pl.multiple_of(rs * WIN_C, WIN_C), blk)
        return jnp.concatenate([ref[0, cb, run, :] for cb in range(n_cb)], axis=0)

    def groups():
        for h in range(2):
            for g in range(GRID_W // grp):
                yield h * GRID_W + g * grp, _na_blocks(g)

    p_s[...] = jnp.zeros_like(p_s)

    def stage1(c, slot):
        qs_all = []
        for u in range(ch):
            r = c * ch + u
            rs = jnp.clip(r - WIN_R // 2, 0, rows - WIN_R)
            q = q_ref[0, pl.ds(row_start(r), GRID_W), :]
            zero = jnp.zeros_like(q)
            qs = jnp.concatenate([jnp.where(first_head, q, zero), jnp.where(first_head, zero, q)], axis=0)
            qs_all.append(qs)
            s = lax.dot_general(qs, window(k_ref, rs), nt_dims, preferred_element_type=F32)
            d0 = rs - r + (WIN_R - 1)
            for r0, cbs in groups():
                for cb in cbs:
                    sl = (slice(r0, r0 + grp), slice(cb * blk, (cb + 1) * blk))
                    s_s[(slot, u) + sl] = s[sl] + tab_ref[(0, d0) + sl]
        sc_s[slot] = lax.dot_general(jnp.concatenate(qs_all, axis=0), kc_ref[0], nt_dims,
                                     preferred_element_type=F32)

    def stage2(slot):
        for u in range(ch):
            for r0 in range(0, qh, 2 * grp):
                parts = {}
                pcs, invs = [], []
                for half in range(2):
                    ra = r0 + half * grp
                    cbs = _na_blocks((ra % GRID_W) // grp)
                    sl = [s_s[slot, u, ra:ra + grp, cb * blk:(cb + 1) * blk] for cb in cbs]
                    sc = sc_s[slot, u * qh + ra:u * qh + ra + grp, :]
                    m = jnp.maximum(jnp.max(jnp.maximum(sl[0], sl[1]), axis=1, keepdims=True),
                                    jnp.max(sc, axis=1, keepdims=True))
                    ps = [jnp.exp(x - m) for x in sl]
                    pc = jnp.exp(sc - m)
                    den = jnp.sum(ps[0] + ps[1], axis=1, keepdims=True) + jnp.sum(pc, axis=1, keepdims=True)
                    for cb, pb in zip(cbs, ps):
                        parts[(half, cb)] = pb
                    pcs.append(pc)
                    invs.append(jnp.broadcast_to(1.0 / den, (grp, LANES)))
                zero = jnp.zeros((grp, blk), F32)
                for cb in sorted({cb for _, cb in parts}):
                    both = jnp.concatenate([parts.get((0, cb), zero), parts.get((1, cb), zero)], axis=0)
                    p_s[slot, u, r0:r0 + 2 * grp, cb * blk:(cb + 1) * blk] = both.astype(BF16)
                pc_s[slot, u * qh + r0:u * qh + r0 + 2 * grp, :] = jnp.concatenate(pcs, axis=0).astype(BF16)
                inv_s[slot, u, r0:r0 + 2 * grp, :] = jnp.concatenate(invs, axis=0)

    def stage3(c, slot):
        oc = jnp.dot(pc_s[slot], vc_ref[0], preferred_element_type=F32)
        for u in range(ch):
            r = c * ch + u
            rs = jnp.clip(r - WIN_R // 2, 0, rows - WIN_R)
            o = (jnp.dot(p_s[slot, u], window(v_ref, rs), preferred_element_type=F32)
                 + oc[u * qh:(u + 1) * qh]) * inv_s[slot, u]
            out = jnp.where(first_head, o[:GRID_W], o[GRID_W:])
            o_ref[0, pl.ds(row_start(r), GRID_W), :] = out.astype(BF16)

    stage1(0, 0)
    stage2(0)
    stage1(1, 1)

    def body(j, carry):
        slot = j % 2
        stage3(j - 2, slot)
        stage2(1 - slot)
        stage1(j, slot)
        return carry

    lax.fori_loop(2, n_chunk, body, 0)
    last = n_chunk % 2
    stage3(n_chunk - 2, last)
    stage2(1 - last)
    stage3(n_chunk - 1, 1 - last)


def _na_call(q, k, v, kc, vc, tab):
    nb, n, d = q.shape
    l = kc.shape[1]
    rows = n // GRID_W
    n_cb = GRID_W // WIN_C
    assert k.shape == v.shape == (nb, n_cb, n // n_cb, d)
    kv_spec = pl.BlockSpec((1, n_cb, n // n_cb, LANES), lambda b, p: (b, 0, 0, p))
    n_pair = d // LANES
    win = WIN_R * GRID_W
    qh = 2 * GRID_W
    assert rows % NA_CHUNK == 0 and rows // NA_CHUNK >= 2
    blk = lambda b, p: (b, 0, p)
    return pl.pallas_call(
        functools.partial(_na_kernel, rows=rows),
        grid=(nb, n_pair),
        in_specs=[
            pl.BlockSpec((1, n, LANES), blk),
            kv_spec,
            kv_spec,
            pl.BlockSpec((1, l, LANES), blk),
            pl.BlockSpec((1, l, LANES), blk),
            pl.BlockSpec((1,) + tab.shape[1:], lambda b, p: (p, 0, 0, 0)),
        ],
        out_specs=pl.BlockSpec((1, n, LANES), blk),
        out_shape=jax.ShapeDtypeStruct((nb, n, d), BF16),
        scratch_shapes=[
            pltpu.VMEM((2, NA_CHUNK, qh, win), F32),
            pltpu.VMEM((2, NA_CHUNK * qh, l), F32),
            pltpu.VMEM((2, NA_CHUNK, qh, win), BF16),
            pltpu.VMEM((2, NA_CHUNK * qh, l), BF16),
            pltpu.VMEM((2, NA_CHUNK, qh, LANES), F32),
        ],
        compiler_params=_cparams(("arbitrary", "arbitrary")),
        name="neighbourhood_attention",
    )(q, k, v, kc, vc, tab)


def _na_bias_table(rpb):
    qc = np.arange(GRID_W)
    kc = np.arange(GRID_W)
    cstart = np.clip(qc - WIN_C // 2, 0, GRID_W - WIN_C)
    ok = (kc[None, :] >= cstart[:, None]) & (kc[None, :] < cstart[:, None] + WIN_C)
    dc = np.clip(kc[None, :] - qc[:, None], -(WIN_C - 1), WIN_C - 1) + WIN_C - 1
    expand = (dc[None] == np.arange(2 * WIN_C - 1)[:, None, None]).astype(np.float32)
    tab = jnp.einsum("hdj,jqk->hdqk", rpb.astype(F32), jnp.asarray(expand), precision=lax.Precision.HIGHEST)
    tab = tab + jnp.asarray(np.where(ok, 0.0, NEG_INF).astype(np.float32))
    h = tab.shape[0]
    n_cb = GRID_W // WIN_C
    wins = jnp.stack([tab[:, d0:d0 + WIN_R] for d0 in range(WIN_R)], axis=1)
    wins = wins.reshape(h, WIN_R, WIN_R, GRID_W, n_cb, WIN_C).transpose(0, 1, 3, 4, 2, 5)
    wins = wins.reshape(h // 2, 2, WIN_R, GRID_W, WIN_R * GRID_W).transpose(0, 2, 1, 3, 4)
    return wins.reshape(h // 2, WIN_R, 2 * GRID_W, WIN_R * GRID_W)


def _mix_kernel(yr_ref, yn_ref, gr_ref, gn_ref, x_ref, g1_ref, sh2_ref, sc2_ref,
                wpr_ref, wpn_ref, wo_ref, wr_ref, x1_ref, xn2_ref, lg_ref):
    nb, tq, d = x_ref.shape
    rows = nb * tq
    pr = jnp.dot(yr_ref[...].reshape(rows, d), wpr_ref[...], preferred_element_type=F32)
    pn = jnp.dot(yn_ref[...].reshape(rows, d), wpn_ref[...], preferred_element_type=F32)
    mix = gr_ref[...].reshape(rows, d).astype(F32) * pr + gn_ref[...].reshape(rows, d).astype(F32) * pn
    o = jnp.dot(mix.astype(BF16), wo_ref[...], preferred_element_type=F32)
    x1 = x_ref[...] + g1_ref[...] * o.reshape(nb, tq, d)
    x1_ref[...] = x1
    xb = (_rms(x1) * (1.0 + sc2_ref[...]) + sh2_ref[...]).astype(BF16)
    xn2_ref[...] = xb
    lg_ref[0] = lax.dot_general(wr_ref[...], xb.reshape(rows, d), (((1,), (1,)), ((), ())),
                                preferred_element_type=F32)


def _mix_call(yr, yn, gr, gn, x, g1, sh2, sc2, wpr, wpn, wo, wr):
    nb, n, d = x.shape
    bat = lambda i: (0, i, 0)
    const3 = lambda i: (0, 0, 0)
    const2 = lambda i: (0, 0)
    act = pl.BlockSpec((nb, TQ, d), bat)
    vec = pl.BlockSpec((nb, 1, d), const3)
    wsp = lambda w: pl.BlockSpec(w.shape, const2, pipeline_mode=pl.Buffered(1))
    return pl.pallas_call(
        _mix_kernel,
        grid=(n // TQ,),
        in_specs=[act, act, act, act, act, vec, vec, vec, wsp(wpr), wsp(wpn), wsp(wo), wsp(wr)],
        out_specs=[act, act, pl.BlockSpec((1, wr.shape[0], nb * TQ), lambda i: (i, 0, 0))],
        out_shape=[
            jax.ShapeDtypeStruct((nb, n, d), F32),
            jax.ShapeDtypeStruct((nb, n, d), BF16),
            jax.ShapeDtypeStruct((n // TQ, wr.shape[0], nb * TQ), F32),
        ],
        compiler_params=_cparams(("arbitrary",)),
        name="merge_out_norm_router",
    )(yr, yn, gr, gn, x, g1, sh2, sc2, wpr, wpn, wo, wr)


def _route_kernel(lg_ref, rank_ref, aff_ref, cnt_ref, *, cap):
    lg = lg_ref[0]
    n_e, n = lg.shape
    ex = jnp.exp(lg - jnp.max(lg, axis=0, keepdims=True))
    aff = ex / jnp.sum(ex, axis=0, keepdims=True)
    aff_ref[0] = aff

    def as_float(bits):
        return lax.bitcast_convert_type(bits, F32)

    thr = jnp.zeros((n_e, 1), jnp.int32)
    for bit in range(30, -1, -1):
        cand = thr | (1 << bit)
        cnt = jnp.sum((aff >= as_float(cand)).astype(F32), axis=1, keepdims=True)
        thr = jnp.where(cnt >= cap, cand, thr)
    thr = jnp.where(thr < MIN_NORMAL_BITS, 0, thr)
    gt = aff >= as_float(jnp.where(thr == 0, MIN_NORMAL_BITS, thr + 1))
    eq = (aff >= as_float(thr)) & jnp.logical_not(gt)
    need = cap - jnp.sum(gt.astype(F32), axis=1, keepdims=True)

    blk = ROUTE_CHUNK
    row = lax.broadcasted_iota(jnp.int32, (blk, blk), 0)
    col = lax.broadcasted_iota(jnp.int32, (blk, blk), 1)
    upper = (row <= col).astype(BF16)

    def cumsum_tokens(mask):
        off = jnp.zeros((n_e, 1), F32)
        outs, offs = [], []
        for c in range(n // blk):
            x = mask[:, c * blk:(c + 1) * blk].astype(BF16)
            cs = jnp.dot(x, upper, preferred_element_type=F32) + off
            offs.append(off)
            outs.append(cs)
            off = cs[:, blk - 1:blk]
        offs.append(off)
        return jnp.concatenate(outs, axis=1), offs

    cum_eq, _ = cumsum_tokens(eq)
    sel = gt | (eq & ((cum_eq - eq.astype(F32)) < need))
    cum_sel, offs = cumsum_tokens(sel)
    rank_ref[0] = jnp.where(sel, cum_sel - 1.0, -1.0).astype(jnp.int32)
    pad = jnp.zeros((n_e, LANES - len(offs)), F32)
    cnt_ref[0] = jnp.concatenate(offs + [pad], axis=1).astype(jnp.int32)


def _route_call(lg_t, cap):
    nb, n_e, n = lg_t.shape
    blk3 = lambda b: (b, 0, 0)
    return pl.pallas_call(
        functools.partial(_route_kernel, cap=cap),
        grid=(nb,),
        in_specs=[pl.BlockSpec((1, n_e, n), blk3)],
        out_specs=[pl.BlockSpec((1, n_e, n), blk3), pl.BlockSpec((1, n_e, n), blk3),
                   pl.BlockSpec((1, n_e, LANES), blk3)],
        out_shape=[jax.ShapeDtypeStruct((nb, n_e, n), jnp.int32), jax.ShapeDtypeStruct((nb, n_e, n), F32),
                   jax.ShapeDtypeStruct((nb, n_e, LANES), jnp.int32)],
        compiler_params=_cparams(("arbitrary",)),
        name="route_select",
    )(lg_t)


def _slot_windows(cnt_ref, b, c, n_e, n_chunk, cap):
    wins = []
    for e in range(n_e):
        base = (b * n_e + e) * (n_chunk + 1) + c
        lo = cnt_ref[base]
        hi = cnt_ref[base + 1]
        lo_al = jnp.minimum((lo // SLOT_ALIGN) * SLOT_ALIGN, cap - ROUTE_WIN)
        extra = jnp.maximum((hi - lo_al + ROUTE_WIN - 1) // ROUTE_WIN - 1, 0)
        wins.append((pl.multiple_of(lo_al, SLOT_ALIGN), extra))
    return wins


def _one_hot_t(rank_row, start, first=None):
    slot = start + lax.broadcasted_iota(jnp.int32, (ROUTE_WIN, rank_row.shape[1]), 0)
    hit = rank_row == slot
    if first is not None:
        hit = hit & (slot >= first)
    return hit


def _spill_window(k, lo_al, cap):
    first = lo_al + k * ROUTE_WIN
    start = pl.multiple_of(jnp.minimum(first, cap - ROUTE_WIN), SLOT_ALIGN)
    return start, first


def _dispatch_kernel(cnt_ref, x_ref, rank_ref, aff_ref, xe_ref, g_ref, *, cap, n_chunk):
    b = pl.program_id(0)
    c = pl.program_id(1)
    n_e = rank_ref.shape[1]

    @pl.when(c == 0)
    def _():
        xe_ref[...] = jnp.zeros_like(xe_ref)
        g_ref[...] = jnp.zeros_like(g_ref)

    x = x_ref[0]
    wins = _slot_windows(cnt_ref, b, c, n_e, n_chunk, cap)
    hots = [_one_hot_t(rank_ref[0, e:e + 1, :], wins[e][0]) for e in range(n_e)]
    stacked = jnp.concatenate([h.astype(BF16) for h in hots], axis=0)
    rows = jnp.dot(stacked, x, preferred_element_type=F32)
    for e in range(n_e):
        sl = pl.ds(wins[e][0], ROUTE_WIN)
        xe_ref[e, sl, :] = xe_ref[e, sl, :] + rows[e * ROUTE_WIN:(e + 1) * ROUTE_WIN].astype(BF16)
        g_ref[e, sl, :] = g_ref[e, sl, :] + jnp.sum(jnp.where(hots[e], aff_ref[0, e:e + 1, :], 0.0),
                                                    axis=1, keepdims=True)

    @pl.when(sum(extra for _, extra in wins) > 0)
    def _():
        for e in range(n_e):
            lo_al, extra = wins[e]

            def spill(k, carry, e=e, lo_al=lo_al):
                start, first = _spill_window(k, lo_al, cap)
                hot = _one_hot_t(rank_ref[0, e:e + 1, :], start, first)
                sl2 = pl.ds(start, ROUTE_WIN)
                xe_ref[e, sl2, :] = xe_ref[e, sl2, :] + jnp.dot(hot.astype(BF16), x_ref[0],
                                                               preferred_element_type=F32).astype(BF16)
                g_ref[e, sl2, :] = g_ref[e, sl2, :] + jnp.sum(jnp.where(hot, aff_ref[0, e:e + 1, :], 0.0),
                                                              axis=1, keepdims=True)
                return carry

            lax.fori_loop(1, extra + 1, spill, 0)


def _dispatch_call(cnt_flat, xn2, rank_t, aff_t, cap):
    nb, n, d = xn2.shape
    n_e = rank_t.shape[1]
    n_chunk = n // ROUTE_CHUNK
    grid_spec = pltpu.PrefetchScalarGridSpec(
        num_scalar_prefetch=1,
        grid=(nb, n_chunk),
        in_specs=[
            pl.BlockSpec((1, ROUTE_CHUNK, d), lambda b, c, cnt: (b, c, 0)),
            pl.BlockSpec((1, n_e, ROUTE_CHUNK), lambda b, c, cnt: (b, 0, c)),
            pl.BlockSpec((1, n_e, ROUTE_CHUNK), lambda b, c, cnt: (b, 0, c)),
        ],
        out_specs=[
            pl.BlockSpec((n_e, cap, d), lambda b, c, cnt: (0, b, 0)),
            pl.BlockSpec((n_e, cap, 1), lambda b, c, cnt: (0, b, 0)),
        ],
    )
    return pl.pallas_call(
        functools.partial(_dispatch_kernel, cap=cap, n_chunk=n_chunk),
        grid_spec=grid_spec,
        out_shape=[jax.ShapeDtypeStruct((n_e, nb * cap, d), BF16), jax.ShapeDtypeStruct((n_e, nb * cap, 1), F32)],
        compiler_params=_cparams(("arbitrary", "arbitrary")),
        name="moe_dispatch",
    )(cnt_flat, xn2, rank_t, aff_t)


def _ffn_kernel(xe_ref, g_ref, wg_ref, wu_ref, wd_ref, o_ref, acc_s):
    f = pl.program_id(1)
    m = pl.program_id(2)
    tm = xe_ref.shape[1]
    xe = xe_ref[0]
    h1 = jnp.dot(xe, wg_ref[0].astype(BF16), preferred_element_type=F32)
    h2 = jnp.dot(xe, wu_ref[0].astype(BF16), preferred_element_type=F32)
    hid = (h1 * _sigmoid(h1) * h2).astype(BF16)
    part = jnp.dot(hid, wd_ref[0].astype(BF16), preferred_element_type=F32)
    rows = pl.ds(pl.multiple_of(m * tm, tm), tm)
    last = pl.num_programs(1) - 1

    @pl.when(f == 0)
    def _():
        acc_s[rows, :] = part

    @pl.when((f > 0) & (f < last))
    def _():
        acc_s[rows, :] += part

    @pl.when(f == last)
    def _():
        o_ref[0] = ((acc_s[rows, :] + part) * g_ref[0]).astype(BF16)


def _ffn_call(xe, g, w_gate, w_up, w_down):
    e, m, d = xe.shape
    dff = w_gate.shape[2]
    tm = min(FFN_TM, m)
    n_f = dff // FFN_TF
    assert n_f >= 2 and m % tm == 0
    out_row = lambda fi, mi: jnp.where(fi == n_f - 1, mi, 0)
    return pl.pallas_call(
        _ffn_kernel,
        grid=(e, n_f, m // tm),
        in_specs=[
            pl.BlockSpec((1, tm, d), lambda ei, fi, mi: (ei, mi, 0)),
            pl.BlockSpec((1, tm, 1), lambda ei, fi, mi: (ei, mi, 0)),
            pl.BlockSpec((1, d, FFN_TF), lambda ei, fi, mi: (ei, 0, fi)),
            pl.BlockSpec((1, d, FFN_TF), lambda ei, fi, mi: (ei, 0, fi)),
            pl.BlockSpec((1, FFN_TF, d), lambda ei, fi, mi: (ei, fi, 0)),
        ],
        out_specs=pl.BlockSpec((1, tm, d), lambda ei, fi, mi: (ei, out_row(fi, mi), 0)),
        out_shape=jax.ShapeDtypeStruct((e, m, d), BF16),
        scratch_shapes=[pltpu.VMEM((m, d), F32)],
        compiler_params=_cparams(("arbitrary", "arbitrary", "arbitrary")),
        name="expert_ffn",
    )(xe, g, w_gate, w_up, w_down)


def _combine_kernel(cnt_ref, ye_ref, rank_ref, x1_ref, g2_ref, fn_ref, o_ref, acc_s, *, cap, n_chunk):
    b = pl.program_id(0)
    c = pl.program_id(1)
    n_e = rank_ref.shape[1]
    tn_dims = (((0,), (0,)), ((), ()))
    wins = _slot_windows(cnt_ref, b, c, n_e, n_chunk, cap)
    hots = [_one_hot_t(rank_ref[0, e:e + 1, :], wins[e][0]).astype(BF16) for e in range(n_e)]
    ys = [ye_ref[e, pl.ds(wins[e][0], ROUTE_WIN), :] for e in range(n_e)]
    moe = lax.dot_general(jnp.concatenate(hots, axis=0), jnp.concatenate(ys, axis=0), tn_dims,
                          preferred_element_type=F32)

    def finish(m):
        x = x1_ref[0] + g2_ref[0] * m
        o_ref[0] = _rms(x) * fn_ref[0]

    any_spill = sum(extra for _, extra in wins) > 0

    @pl.when(jnp.logical_not(any_spill))
    def _():
        finish(moe)

    @pl.when(any_spill)
    def _():
        acc_s[...] = moe
        for e in range(n_e):
            lo_al, extra = wins[e]

            def spill(k, carry, e=e, lo_al=lo_al):
                start, first = _spill_window(k, lo_al, cap)
                hot = _one_hot_t(rank_ref[0, e:e + 1, :], start, first).astype(BF16)
                acc_s[...] += lax.dot_general(hot, ye_ref[e, pl.ds(start, ROUTE_WIN), :], tn_dims,
                                              preferred_element_type=F32)
                return carry

            lax.fori_loop(1, extra + 1, spill, 0)
        finish(acc_s[...])


def _combine_call(cnt_flat, ye, rank_t, x1, g2, fnorm, cap):
    nb, n, d = x1.shape
    n_e = rank_t.shape[1]
    n_chunk = n // ROUTE_CHUNK
    grid_spec = pltpu.PrefetchScalarGridSpec(
        num_scalar_prefetch=1,
        grid=(nb, n_chunk),
        in_specs=[
            pl.BlockSpec((n_e, cap, d), lambda b, c, cnt: (0, b, 0)),
            pl.BlockSpec((1, n_e, ROUTE_CHUNK), lambda b, c, cnt: (b, 0, c)),
            pl.BlockSpec((1, ROUTE_CHUNK, d), lambda b, c, cnt: (b, c, 0)),
            pl.BlockSpec((1, 1, d), lambda b, c, cnt: (b, 0, 0)),
            pl.BlockSpec((1, 1, d), lambda b, c, cnt: (0, 0, 0)),
        ],
        out_specs=pl.BlockSpec((1, ROUTE_CHUNK, d), lambda b, c, cnt: (b, c, 0)),
        scratch_shapes=[pltpu.VMEM((ROUTE_CHUNK, d), F32)],
    )
    return pl.pallas_call(
        functools.partial(_combine_kernel, cap=cap, n_chunk=n_chunk),
        grid_spec=grid_spec,
        out_shape=jax.ShapeDtypeStruct((nb, n, d), F32),
        compiler_params=_cparams(("arbitrary", "arbitrary")),
        name="moe_combine_final_norm",
    )(cnt_flat, ye, rank_t, x1, g2, fnorm.reshape(1, 1, d))


def kernel(x, c, ctx, c_ctx, w_mod, b_mod, w_in, b_in, conv_w, conv_b, lru_wa, lru_ba, lru_wi, lru_bi,
           lru_lambda, na_rpb, w_proj_rnn, w_proj_na, w_out, w_router, w_exp_gate, w_exp_up, w_exp_down,
           final_norm):
    nb, n, d = x.shape
    l = ctx.shape[1]
    assert w_mod.shape[0] == 1, "single-layer problem"
    assert nb == SUBLANES and n % LRU_TN == 0 and l % LRU_TN == 0 and d % LRU_SLAB == 0
    lyr = 0

    cc = jnp.concatenate([c, c_ctx[None], jnp.zeros((2 * SUBLANES - nb - 1, d), F32)], axis=0)
    mod = _mod_call(cc, w_mod[lyr], b_mod[lyr])
    sh1, sc1, g1, sh2, sc2, g2 = [m[:nb, None, :] for m in jnp.split(mod, 6, axis=-1)]
    csh1, csc1 = [jnp.broadcast_to(m[nb:nb + 1, None, :], (nb, 1, d)) for m in jnp.split(mod, 6, axis=-1)[:2]]

    w_in_b = w_in[lyr].astype(BF16)
    b_in_r = b_in[lyr].reshape(1, -1)
    tables = _rope_tables(n)
    gy, xr_t, q, k, v, sgr, sgn = _inproj_call(
        x, sh1, sc1, w_in_b, b_in_r, tables,
        ("gelu", "tok_major", "rope_q", "rope_k", "bf16", "sigmoid", "sigmoid"), tuple(range(7)), "in_proj_latent",
        col_blocked=(3, 4))
    xrc_t, kc, vc = _inproj_call(
        ctx, csh1, csc1, w_in_b, b_in_r, tables, ("tok_major", "bf16", "bf16"), (1, 3, 4), "in_proj_context")

    gw = [_lru_gate_weights(lru_wa[lyr, dr], lru_ba[lyr, dr], lru_wi[lyr, dr], lru_bi[lyr, dr], lru_lambda[lyr, dr])
          for dr in range(2)]
    zeros_h = jnp.zeros((nb, d), F32)
    _, hc_f = _lru_call(xrc_t, conv_w[lyr], conv_b[lyr], *gw[0], zeros_h, reverse=False, name="rglru_ctx_fwd")
    _, hc_b = _lru_call(xrc_t, conv_w[lyr], conv_b[lyr], *gw[1], zeros_h, reverse=True, name="rglru_ctx_bwd")
    hf_t, _ = _lru_call(xr_t, conv_w[lyr], conv_b[lyr], *gw[0], hc_f, reverse=False, name="rglru_fwd")
    y_rnn, _ = _lru_call(xr_t, conv_w[lyr], conv_b[lyr], *gw[1], hc_b, hf_t, gy, reverse=True, name="rglru_bwd")

    y_na = _na_call(q, k, v, kc, vc, _na_bias_table(na_rpb[lyr]))

    x1, xn2, lg_tiles = _mix_call(y_rnn, y_na, sgr, sgn, x, g1, sh2, sc2,
                                  w_proj_rnn[lyr].astype(BF16), w_proj_na[lyr].astype(BF16),
                                  w_out[lyr].astype(BF16), w_router[lyr].T.astype(BF16))
    n_e = w_router.shape[2]
    lg_t = lg_tiles.reshape(n // TQ, n_e, nb, TQ).transpose(2, 1, 0, 3).reshape(nb, n_e, n)

    cap = EC_CAPACITY * n // N_EXPERTS
    n_chunk = n // ROUTE_CHUNK
    assert cap % SLOT_ALIGN == 0 and cap >= ROUTE_WIN and n % ROUTE_CHUNK == 0
    rank_t, aff_t, cnt = _route_call(lg_t, cap)
    cnt_flat = cnt[:, :, :n_chunk + 1].reshape(-1)
    xe, ge = _dispatch_call(cnt_flat, xn2, rank_t, aff_t, cap)
    ye = _ffn_call(xe, ge, w_exp_gate[lyr], w_exp_up[lyr], w_exp_down[lyr])
    return _combine_call(cnt_flat, ye, rank_t, x1, g2, final_norm, cap)
```

```python
import functools

import numpy as np
import jax
import jax.numpy as jnp
from jax import lax
from jax.experimental import pallas as pl
from jax.experimental.pallas import tpu as pltpu

F32 = jnp.float32
BF16 = jnp.bfloat16

GRID_W = 64
N_HEADS = 16
HEAD_DIM = 64
N_LRU_BLOCKS = 16
CONV_W = 4
LRU_C = 8.0
WIN_R = 8
WIN_C = 16
ROPE_BASE = 10000.0
N_EXPERTS = 16
EC_CAPACITY = 2
EPS = 1e-6
NEG_INF = -1e30

LANES = 128
SUBLANES = 8
VMEM_LIMIT = 56 * 1024 * 1024

TQ = 64
LRU_SLAB = 256
LRU_TN = 128
FFN_TM = 512
FFN_TF = 1024
NA_CHUNK = 2
ROUTE_CHUNK = 256
ROUTE_WIN = 64
SLOT_ALIGN = 16
MIN_NORMAL_BITS = 0x00800000
LOG2_E = 1.4426950408889634


def _cparams(sem):
    return pltpu.CompilerParams(dimension_semantics=sem, vmem_limit_bytes=VMEM_LIMIT)


def _sigmoid(x):
    return 0.5 * jnp.tanh(0.5 * x) + 0.5


def _rms(x):
    return x * lax.rsqrt(jnp.mean(x * x, axis=-1, keepdims=True) + EPS)


def _mod_kernel(c_ref, w_ref, b_ref, o_ref):
    c = c_ref[...]
    s = c * jax.nn.sigmoid(c)
    o_ref[...] = jnp.dot(s, w_ref[...], preferred_element_type=F32) + b_ref[...]


def _mod_call(cc, w_mod, b_mod):
    rows, d = cc.shape
    n_out = w_mod.shape[1]
    tn = 1024
    return pl.pallas_call(
        _mod_kernel,
        grid=(n_out // tn,),
        in_specs=[
            pl.BlockSpec((rows, d), lambda j: (0, 0)),
            pl.BlockSpec((d, tn), lambda j: (0, j)),
            pl.BlockSpec((1, tn), lambda j: (0, j)),
        ],
        out_specs=pl.BlockSpec((rows, tn), lambda j: (0, j)),
        out_shape=jax.ShapeDtypeStruct((rows, n_out), F32),
        compiler_params=_cparams(("arbitrary",)),
        name="adaln_mod",
    )(cc, w_mod, b_mod.reshape(1, n_out))


def _rope(z, cos, sa, sb):
    outs = []
    for g in range(z.shape[1] // LANES):
        zg = z[:, g * LANES:(g + 1) * LANES]
        outs.append(zg * cos + pltpu.roll(zg, 16, 1) * sa + pltpu.roll(zg, LANES - 16, 1) * sb)
    return jnp.concatenate(outs, axis=1)


def _inproj_kernel(*refs, kinds, groups):
    n_out = len(kinds)
    x_ref, sh_ref, sc_ref, w_ref, b_ref, cos_ref, sa_ref, sb_ref = refs[:8]
    out_refs = refs[8:8 + n_out]
    nb, tq, d = x_ref.shape
    rows = nb * tq
    xn = _rms(x_ref[...]) * (1.0 + sc_ref[...]) + sh_ref[...]
    xb = xn.reshape(rows, d).astype(BF16)

    def tiled(t_ref):
        return jnp.broadcast_to(t_ref[...][None], (nb, tq, LANES)).reshape(rows, LANES)

    for g, kind, o_ref in zip(groups, kinds, out_refs):
        z = jnp.dot(xb, w_ref[:, g * d:(g + 1) * d], preferred_element_type=F32) + b_ref[:, g * d:(g + 1) * d]
        if kind == "gelu":
            o_ref[...] = jax.nn.gelu(z, approximate=True).astype(BF16).reshape(nb, tq, d)
        elif kind == "tok_major":
            o_ref[...] = jnp.swapaxes(z.reshape(nb, tq, d), 0, 1).reshape(rows, d).astype(BF16)
        elif kind == "rope_q":
            r = _rope(z * (HEAD_DIM ** -0.5), tiled(cos_ref), tiled(sa_ref), tiled(sb_ref))
            o_ref[...] = r.astype(BF16).reshape(nb, tq, d)
        elif kind == "rope_k":
            r = _rope(z, tiled(cos_ref), tiled(sa_ref), tiled(sb_ref))
            o_ref[...] = r.astype(BF16).reshape(o_ref.shape)
        elif kind == "bf16":
            o_ref[...] = z.astype(BF16).reshape(o_ref.shape)
        elif kind == "sigmoid":
            o_ref[...] = _sigmoid(z).astype(BF16).reshape(nb, tq, d)
        else:
            raise ValueError(kind)


def _inproj_call(x, sh, sc, w, b, tables, kinds, groups, name, col_blocked=()):
    nb, n, d = x.shape
    assert len(groups) == len(kinds) and w.shape[0] == d and max(groups) * d < w.shape[1]
    cos, sa, sb = tables
    bat = lambda i: (0, i, 0)
    const3 = lambda i: (0, 0, 0)
    const2 = lambda i: (0, 0)
    out_shapes, out_specs = [], []
    for kind in kinds:
        if kind == "tok_major":
            out_shapes.append(jax.ShapeDtypeStruct((n * nb, d), BF16))
            out_specs.append(pl.BlockSpec((TQ * nb, d), lambda i: (i, 0)))
        elif len(out_shapes) in col_blocked:
            assert TQ == GRID_W
            n_cb = GRID_W // WIN_C
            out_shapes.append(jax.ShapeDtypeStruct((nb, n_cb, n // n_cb, d), BF16))
            out_specs.append(pl.BlockSpec((nb, n_cb, WIN_C, d), lambda i: (0, 0, i, 0)))
        else:
            out_shapes.append(jax.ShapeDtypeStruct((nb, n, d), BF16))
            out_specs.append(pl.BlockSpec((nb, TQ, d), bat))
    rows_per_table = cos.shape[0] // TQ
    return pl.pallas_call(
        functools.partial(_inproj_kernel, kinds=kinds, groups=groups),
        grid=(n // TQ,),
        in_specs=[
            pl.BlockSpec((nb, TQ, d), bat),
            pl.BlockSpec(sh.shape, const3),
            pl.BlockSpec(sc.shape, const3),
            pl.BlockSpec(w.shape, const2, pipeline_mode=pl.Buffered(1)),
            pl.BlockSpec(b.shape, const2),
            pl.BlockSpec((TQ, LANES), lambda i: (i % rows_per_table, 0)),
            pl.BlockSpec((TQ, LANES), lambda i: (i % rows_per_table, 0)),
            pl.BlockSpec((TQ, LANES), lambda i: (i % rows_per_table, 0)),
        ],
        out_specs=out_specs,
        out_shape=out_shapes,
        compiler_params=_cparams(("arbitrary",)),
        name=name,
    )(x, sh, sc, w, b, cos, sa, sb)


def _rope_tables(n):
    t = np.arange(n)
    row = (t // GRID_W).astype(np.float32)
    col = (t % GRID_W).astype(np.float32)
    n_freq = HEAD_DIM // 4
    inv = (np.float32(ROPE_BASE) ** (-np.arange(n_freq, dtype=np.float32) / np.float32(n_freq))).astype(np.float32)
    ang_r = (row[:, None] * inv).astype(np.float32)
    ang_c = (col[:, None] * inv).astype(np.float32)
    zero = np.zeros_like(ang_r)
    cos_h = np.concatenate([np.cos(ang_r), np.cos(ang_r), np.cos(ang_c), np.cos(ang_c)], axis=1)
    sa_h = np.concatenate([zero, np.sin(ang_r), zero, np.sin(ang_c)], axis=1)
    sb_h = np.concatenate([-np.sin(ang_r), zero, -np.sin(ang_c), zero], axis=1)
    rep = LANES // HEAD_DIM
    return tuple(jnp.asarray(np.tile(a.astype(np.float32), (1, rep))) for a in (cos_h, sa_h, sb_h))


def _lru_kernel(*refs, tn, nt, reverse, fuse_out):
    if fuse_out:
        (xp_ref, xc_ref, xn_ref, cw_ref, cb_ref, wg_ref, bg_ref, lam_ref, h0_ref, hf_ref, gy_ref,
         out_ref, hfin_ref, a_s, b_s, hcar_s) = refs
    else:
        (xp_ref, xc_ref, xn_ref, cw_ref, cb_ref, wg_ref, bg_ref, lam_ref, h0_ref,
         out_ref, hfin_ref, a_s, b_s, hcar_s) = refs
    i = pl.program_id(0)
    ti = (nt - 1 - i) if reverse else i
    nb, d = hcar_s.shape

    @pl.when(i == 0)
    def _():
        hcar_s[...] = h0_ref[...]

    def tokens(ref):
        return ref[...].astype(F32).reshape(ref.shape[0] // nb, nb, d)

    x = tokens(xc_ref)
    xp = jnp.where(ti == 0, 0.0, tokens(xp_ref))
    xnx = jnp.where(ti == nt - 1, 0.0, tokens(xn_ref)[:1])
    ext = jnp.concatenate([xp, x, xnx], axis=0)
    xc = cb_ref[...]
    for k in range(CONV_W):
        xc = xc + ext[k:k + tn] * cw_ref[k]
    rows = tn * nb
    xc2 = xc.reshape(rows, d)
    for s in range(d // LRU_SLAB):
        sl = slice(s * LRU_SLAB, (s + 1) * LRU_SLAB)
        xs = xc2[:, sl]
        half_gates = jnp.dot(xs.astype(BF16), wg_ref[s], preferred_element_type=F32) + bg_ref[s]
        t_r = jnp.tanh(half_gates[:, :LRU_SLAB])
        ig = 0.5 * jnp.tanh(half_gates[:, LRU_SLAB:]) + 0.5
        z = -lam_ref[s]
        softplus = jnp.maximum(z, 0.0) + jnp.log1p(jnp.exp(-jnp.abs(z)))
        k = (-0.5 * LRU_C * LOG2_E) * softplus
        a = jnp.exp2(t_r * k + k)
        b = jnp.sqrt(1.0 - a * a) * (ig * xs)
        a_s[:, :, sl] = a.reshape(tn, nb, LRU_SLAB)
        b_s[:, :, sl] = b.reshape(tn, nb, LRU_SLAB)

    unroll = 8

    def step(j, h):
        for k in range(unroll):
            jj = j * unroll + k
            t = (tn - 1 - jj) if reverse else jj
            h = a_s[t] * h + b_s[t]
            a_s[t] = h
        return h

    h = lax.fori_loop(0, tn // unroll, step, hcar_s[...])
    hcar_s[...] = h
    hfin_ref[...] = h
    if fuse_out:
        hs = jnp.swapaxes(tokens(hf_ref) + a_s[...], 0, 1)
        out_ref[...] = (hs * gy_ref[...].astype(F32)).astype(BF16)
    else:
        out_ref[...] = a_s[...].reshape(tn * nb, d).astype(BF16)


def _lru_call(xr_t, conv_w, conv_b, wg, bg, lam, h0, hf=None, gy=None, *, reverse, name):
    nb, d = h0.shape
    n = xr_t.shape[0] // nb
    tn = LRU_TN
    nt = n // tn
    halo = 2
    fuse_out = hf is not None
    tile = (lambda i: nt - 1 - i) if reverse else (lambda i: i)
    n_slab = d // LRU_SLAB
    in_specs = [
        pl.BlockSpec((halo * nb, d), lambda i: (jnp.maximum(tile(i) * (tn // halo) - 1, 0), 0)),
        pl.BlockSpec((tn * nb, d), lambda i: (tile(i), 0)),
        pl.BlockSpec((halo * nb, d), lambda i: (jnp.minimum((tile(i) + 1) * (tn // halo), n // halo - 1), 0)),
        pl.BlockSpec((CONV_W, 1, d), lambda i: (0, 0, 0)),
        pl.BlockSpec((1, d), lambda i: (0, 0)),
        pl.BlockSpec((n_slab, LRU_SLAB, 2 * LRU_SLAB), lambda i: (0, 0, 0)),
        pl.BlockSpec((n_slab, 1, 2 * LRU_SLAB), lambda i: (0, 0, 0)),
        pl.BlockSpec((n_slab, 1, LRU_SLAB), lambda i: (0, 0, 0)),
        pl.BlockSpec((nb, d), lambda i: (0, 0)),
    ]
    args = [xr_t, xr_t, xr_t, conv_w.reshape(CONV_W, 1, d), conv_b.reshape(1, d), wg, bg, lam, h0]
    if fuse_out:
        in_specs += [
            pl.BlockSpec((tn * nb, d), lambda i: (tile(i), 0)),
            pl.BlockSpec((nb, tn, d), lambda i: (0, tile(i), 0)),
        ]
        args += [hf, gy]
        out_shape0 = jax.ShapeDtypeStruct((nb, n, d), BF16)
        out_spec0 = pl.BlockSpec((nb, tn, d), lambda i: (0, tile(i), 0))
    else:
        out_shape0 = jax.ShapeDtypeStruct((n * nb, d), BF16)
        out_spec0 = pl.BlockSpec((tn * nb, d), lambda i: (tile(i), 0))
    return pl.pallas_call(
        functools.partial(_lru_kernel, tn=tn, nt=nt, reverse=reverse, fuse_out=fuse_out),
        grid=(nt,),
        in_specs=in_specs,
        out_specs=[out_spec0, pl.BlockSpec((nb, d), lambda i: (0, 0))],
        out_shape=[out_shape0, jax.ShapeDtypeStruct((nb, d), F32)],
        scratch_shapes=[
            pltpu.VMEM((tn, nb, d), F32),
            pltpu.VMEM((tn, nb, d), F32),
            pltpu.VMEM((nb, d), F32),
        ],
        compiler_params=_cparams(("arbitrary",)),
        name=name,
    )(*args)


def _lru_gate_weights(wa, ba, wi, bi, lam):
    n_blk, blk, _ = wa.shape
    d = n_blk * blk
    n_slab = d // LRU_SLAB
    per = LRU_SLAB // blk
    eye = jnp.eye(per, dtype=F32)

    def dense_slabs(w):
        return jnp.einsum("saij,ac->saicj", w.reshape(n_slab, per, blk, blk), eye).reshape(n_slab, LRU_SLAB, LRU_SLAB)

    wg = (0.5 * jnp.concatenate([dense_slabs(wa), dense_slabs(wi)], axis=2).astype(BF16)).astype(BF16)
    bg = 0.5 * jnp.concatenate([ba.reshape(n_slab, 1, LRU_SLAB), bi.reshape(n_slab, 1, LRU_SLAB)], axis=2)
    return wg, bg, lam.reshape(n_slab, 1, LRU_SLAB)


def _na_blocks(g):
    lo = min(max(8 * g - WIN_C // 2, 0) // WIN_C, GRID_W // WIN_C - 2)
    return (lo, lo + 1)


def _na_kernel(q_ref, k_ref, v_ref, kc_ref, vc_ref, tab_ref, o_ref, s_s, sc_s, p_s, pc_s, inv_s, *, rows):
    win = WIN_R * GRID_W
    ch = NA_CHUNK
    n_chunk = rows // ch
    qh = 2 * GRID_W
    n_cb = GRID_W // WIN_C
    blk = WIN_R * WIN_C
    grp = SUBLANES
    lane = lax.broadcasted_iota(jnp.int32, (GRID_W, LANES), 1)
    first_head = lane < HEAD_DIM
    nt_dims = (((1,), (1,)), ((), ()))

    def row_start(r):
        return pl.multiple_of(r * GRID_W, GRID_W)

    def window(ref, rs):
        run = pl.ds(pl.multiple_of(rs * WIN_C, WIN_C), blk)
        return jnp.concatenate([ref[0, cb, run, :] for cb in range(n_cb)], axis=0)

    def groups():
        for h in range(2):
            for g in range(GRID_W // grp):
                yield h, g * grp, _na_blocks(g)

    p_s[...] = jnp.zeros_like(p_s)

    def stage1(c, slot):
        qs_all = []
        for u in range(ch):
            r = c * ch + u
            rs = jnp.clip(r - WIN_R // 2, 0, rows - WIN_R)
            q = q_ref[0, pl.ds(row_start(r), GRID_W), :]
            zero = jnp.zeros_like(q)
            qs = jnp.concatenate([jnp.where(first_head, q, zero), jnp.where(first_head, zero, q)], axis=0)
            qs_all.append(qs)
            s = lax.dot_general(qs, window(k_ref, rs), nt_dims, preferred_element_type=F32)
            d0 = rs - r + (WIN_R - 1)
            for h, q0, cbs in groups():
                for cb in cbs:
                    rw = slice(h * GRID_W + q0, h * GRID_W + q0 + grp)
                    ln = slice(cb * blk, (cb + 1) * blk)
                    s_s[slot, u, rw, ln] = s[rw, ln] + tab_ref[0, h, d0, q0:q0 + grp, ln]
        sc_s[slot] = lax.dot_general(jnp.concatenate(qs_all, axis=0), kc_ref[0], nt_dims,
                                     preferred_element_type=F32)

    def stage2(slot):
        for u in range(ch):
            for r0 in range(0, qh, 2 * grp):
                parts = {}
                pcs, invs = [], []
                for half in range(2):
                    ra = r0 + half * grp
                    cbs = _na_blocks((ra % GRID_W) // grp)
                    sl = [s_s[slot, u, ra:ra + grp, cb * blk:(cb + 1) * blk] for cb in cbs]
                    sc = sc_s[slot, u * qh + ra:u * qh + ra + grp, :]
                    m = jnp.maximum(jnp.max(jnp.maximum(sl[0], sl[1]), axis=1, keepdims=True),
                                    jnp.max(sc, axis=1, keepdims=True))
                    ps = [jnp.exp(x - m) for x in sl]
                    pc = jnp.exp(sc - m)
                    den = jnp.sum(ps[0] + ps[1], axis=1, keepdims=True) + jnp.sum(pc, axis=1, keepdims=True)
                    for cb, pb in zip(cbs, ps):
                        parts[(half, cb)] = pb
                    pcs.append(pc)
                    invs.append(jnp.broadcast_to(1.0 / den, (grp, LANES)))
                zero = jnp.zeros((grp, blk), F32)
                for cb in sorted({cb for _, cb in parts}):
                    both = jnp.concatenate([parts.get((0, cb), zero), parts.get((1, cb), zero)], axis=0)
                    p_s[slot, u, r0:r0 + 2 * grp, cb * blk:(cb + 1) * blk] = both.astype(BF16)
                pc_s[slot, u * qh + r0:u * qh + r0 + 2 * grp, :] = jnp.concatenate(pcs, axis=0).astype(BF16)
                inv_s[slot, u, r0:r0 + 2 * grp, :] = jnp.concatenate(invs, axis=0)

    def stage3(c, slot):
        oc = jnp.dot(pc_s[slot], vc_ref[0], preferred_element_type=F32)
        for u in range(ch):
            r = c * ch + u
            rs = jnp.clip(r - WIN_R // 2, 0, rows - WIN_R)
            o = (jnp.dot(p_s[slot, u], window(v_ref, rs), preferred_element_type=F32)
                 + oc[u * qh:(u + 1) * qh]) * inv_s[slot, u]
            out = jnp.where(first_head, o[:GRID_W], o[GRID_W:])
            o_ref[0, pl.ds(row_start(r), GRID_W), :] = out.astype(BF16)

    stage1(0, 0)
    stage2(0)
    stage1(1, 1)

    def body(j, carry):
        slot = j % 2
        stage3(j - 2, slot)
        stage2(1 - slot)
        stage1(j, slot)
        return carry

    lax.fori_loop(2, n_chunk, body, 0)
    last = n_chunk % 2
    stage3(n_chunk - 2, last)
    stage2(1 - last)
    stage3(n_chunk - 1, 1 - last)


def _na_call(q, k, v, kc, vc, tab):
    nb, n, d = q.shape
    l = kc.shape[1]
    rows = n // GRID_W
    n_cb = GRID_W // WIN_C
    assert k.shape == v.shape == (nb, n_cb, n // n_cb, d)
    kv_spec = pl.BlockSpec((1, n_cb, n // n_cb, LANES), lambda b, p: (b, 0, 0, p))
    n_pair = d // LANES
    win = WIN_R * GRID_W
    qh = 2 * GRID_W
    assert rows % NA_CHUNK == 0 and rows // NA_CHUNK >= 2
    blk = lambda b, p: (b, 0, p)
    return pl.pallas_call(
        functools.partial(_na_kernel, rows=rows),
        grid=(nb, n_pair),
        in_specs=[
            pl.BlockSpec((1, n, LANES), blk),
            kv_spec,
            kv_spec,
            pl.BlockSpec((1, l, LANES), blk),
            pl.BlockSpec((1, l, LANES), blk),
            pl.BlockSpec((1,) + tab.shape[1:], lambda b, p: (p, 0, 0, 0, 0)),
        ],
        out_specs=pl.BlockSpec((1, n, LANES), blk),
        out_shape=jax.ShapeDtypeStruct((nb, n, d), BF16),
        scratch_shapes=[
            pltpu.VMEM((2, NA_CHUNK, qh, win), F32),
            pltpu.VMEM((2, NA_CHUNK * qh, l), F32),
            pltpu.VMEM((2, NA_CHUNK, qh, win), BF16),
            pltpu.VMEM((2, NA_CHUNK * qh, l), BF16),
            pltpu.VMEM((2, NA_CHUNK, qh, LANES), F32),
        ],
        compiler_params=_cparams(("arbitrary", "arbitrary")),
        name="neighbourhood_attention",
    )(q, k, v, kc, vc, tab)


def _na_bias_table(rpb):
    qc = np.arange(GRID_W)
    kc = np.arange(GRID_W)
    n_cb = GRID_W // WIN_C
    cstart = np.clip(qc - WIN_C // 2, 0, GRID_W - WIN_C)
    ok = (kc[None, :] >= cstart[:, None]) & (kc[None, :] < cstart[:, None] + WIN_C)
    dc = np.clip(kc[None, :] - qc[:, None], -(WIN_C - 1), WIN_C - 1) + WIN_C - 1
    expand = (dc[None] == np.arange(2 * WIN_C - 1)[:, None, None]).astype(np.float32)
    expand = expand.reshape(2 * WIN_C - 1, GRID_W, n_cb, WIN_C)
    mask = np.where(ok, 0.0, NEG_INF).astype(np.float32).reshape(GRID_W, n_cb, 1, WIN_C)
    h = rpb.shape[0]
    rows = jnp.stack([rpb[:, d0:d0 + WIN_R].astype(F32) for d0 in range(WIN_R)], axis=1)
    tab = jnp.einsum("hdjx,xqbc->hdqbjc", rows, jnp.asarray(expand), precision=lax.Precision.HIGHEST)
    tab = tab + jnp.asarray(mask)
    return tab.reshape(h // 2, 2, WIN_R, GRID_W, WIN_R * GRID_W)


def _mix_kernel(yr_ref, yn_ref, gr_ref, gn_ref, x_ref, g1_ref, sh2_ref, sc2_ref,
                wpr_ref, wpn_ref, wo_ref, wr_ref, x1_ref, xn2_ref, lg_ref):
    nb, tq, d = x_ref.shape
    rows = nb * tq
    pr = jnp.dot(yr_ref[...].reshape(rows, d), wpr_ref[...], preferred_element_type=F32)
    pn = jnp.dot(yn_ref[...].reshape(rows, d), wpn_ref[...], preferred_element_type=F32)
    mix = gr_ref[...].reshape(rows, d).astype(F32) * pr + gn_ref[...].reshape(rows, d).astype(F32) * pn
    o = jnp.dot(mix.astype(BF16), wo_ref[...], preferred_element_type=F32)
    x1 = x_ref[...] + g1_ref[...] * o.reshape(nb, tq, d)
    x1_ref[...] = x1
    xb = (_rms(x1) * (1.0 + sc2_ref[...]) + sh2_ref[...]).astype(BF16)
    xn2_ref[...] = xb
    lg_ref[0] = lax.dot_general(wr_ref[...], xb.reshape(rows, d), (((1,), (1,)), ((), ())),
                                preferred_element_type=F32)


def _mix_call(yr, yn, gr, gn, x, g1, sh2, sc2, wpr, wpn, wo, wr):
    nb, n, d = x.shape
    bat = lambda i: (0, i, 0)
    const3 = lambda i: (0, 0, 0)
    const2 = lambda i: (0, 0)
    act = pl.BlockSpec((nb, TQ, d), bat)
    vec = pl.BlockSpec((nb, 1, d), const3)
    wsp = lambda w: pl.BlockSpec(w.shape, const2, pipeline_mode=pl.Buffered(1))
    return pl.pallas_call(
        _mix_kernel,
        grid=(n // TQ,),
        in_specs=[act, act, act, act, act, vec, vec, vec, wsp(wpr), wsp(wpn), wsp(wo), wsp(wr)],
        out_specs=[act, act, pl.BlockSpec((1, wr.shape[0], nb * TQ), lambda i: (i, 0, 0))],
        out_shape=[
            jax.ShapeDtypeStruct((nb, n, d), F32),
            jax.ShapeDtypeStruct((nb, n, d), BF16),
            jax.ShapeDtypeStruct((n // TQ, wr.shape[0], nb * TQ), F32),
        ],
        compiler_params=_cparams(("arbitrary",)),
        name="merge_out_norm_router",
    )(yr, yn, gr, gn, x, g1, sh2, sc2, wpr, wpn, wo, wr)


def _route_kernel(lg_ref, rank_ref, aff_ref, cnt_ref, *, cap):
    lg = lg_ref[0]
    n_e, n = lg.shape
    ex = jnp.exp(lg - jnp.max(lg, axis=0, keepdims=True))
    aff = ex / jnp.sum(ex, axis=0, keepdims=True)
    aff_ref[0] = aff

    def as_float(bits):
        return lax.bitcast_convert_type(bits, F32)

    thr = jnp.zeros((n_e, 1), jnp.int32)
    for bit in range(30, -1, -1):
        cand = thr | (1 << bit)
        cnt = jnp.sum((aff >= as_float(cand)).astype(F32), axis=1, keepdims=True)
        thr = jnp.where(cnt >= cap, cand, thr)
    thr = jnp.where(thr < MIN_NORMAL_BITS, 0, thr)
    gt = aff >= as_float(jnp.where(thr == 0, MIN_NORMAL_BITS, thr + 1))
    eq = (aff >= as_float(thr)) & jnp.logical_not(gt)
    need = cap - jnp.sum(gt.astype(F32), axis=1, keepdims=True)

    blk = ROUTE_CHUNK
    row = lax.broadcasted_iota(jnp.int32, (blk, blk), 0)
    col = lax.broadcasted_iota(jnp.int32, (blk, blk), 1)
    upper = (row <= col).astype(BF16)

    def cumsum_tokens(mask):
        off = jnp.zeros((n_e, 1), F32)
        outs, offs = [], []
        for c in range(n // blk):
            x = mask[:, c * blk:(c + 1) * blk].astype(BF16)
            cs = jnp.dot(x, upper, preferred_element_type=F32) + off
            offs.append(off)
            outs.append(cs)
            off = cs[:, blk - 1:blk]
        offs.append(off)
        return jnp.concatenate(outs, axis=1), offs

    cum_eq, _ = cumsum_tokens(eq)
    sel = gt | (eq & ((cum_eq - eq.astype(F32)) < need))
    cum_sel, offs = cumsum_tokens(sel)
    rank_ref[0] = jnp.where(sel, cum_sel - 1.0, -1.0).astype(jnp.int32)
    pad = jnp.zeros((n_e, LANES - len(offs)), F32)
    cnt_ref[0] = jnp.concatenate(offs + [pad], axis=1).astype(jnp.int32)


def _route_call(lg_t, cap):
    nb, n_e, n = lg_t.shape
    blk3 = lambda b: (b, 0, 0)
    return pl.pallas_call(
        functools.partial(_route_kernel, cap=cap),
        grid=(nb,),
        in_specs=[pl.BlockSpec((1, n_e, n), blk3)],
        out_specs=[pl.BlockSpec((1, n_e, n), blk3), pl.BlockSpec((1, n_e, n), blk3),
                   pl.BlockSpec((1, n_e, LANES), blk3)],
        out_shape=[jax.ShapeDtypeStruct((nb, n_e, n), jnp.int32), jax.ShapeDtypeStruct((nb, n_e, n), F32),
                   jax.ShapeDtypeStruct((nb, n_e, LANES), jnp.int32)],
        compiler_params=_cparams(("arbitrary",)),
        name="route_select",
    )(lg_t)


def _slot_windows(cnt_ref, b, c, n_e, n_chunk, cap):
    wins = []
    for e in range(n_e):
        base = (b * n_e + e) * (n_chunk + 1) + c
        lo = cnt_ref[base]
        hi = cnt_ref[base + 1]
        lo_al = jnp.minimum((lo // SLOT_ALIGN) * SLOT_ALIGN, cap - ROUTE_WIN)
        extra = jnp.maximum((hi - lo_al + ROUTE_WIN - 1) // ROUTE_WIN - 1, 0)
        wins.append((pl.multiple_of(lo_al, SLOT_ALIGN), extra))
    return wins


def _one_hot_t(rank_row, start, first=None):
    slot = start + lax.broadcasted_iota(jnp.int32, (ROUTE_WIN, rank_row.shape[1]), 0)
    hit = rank_row == slot
    if first is not None:
        hit = hit & (slot >= first)
    return hit


def _spill_window(k, lo_al, cap):
    first = lo_al + k * ROUTE_WIN
    start = pl.multiple_of(jnp.minimum(first, cap - ROUTE_WIN), SLOT_ALIGN)
    return start, first


def _dispatch_kernel(cnt_ref, x_ref, rank_ref, aff_ref, xe_ref, g_ref, *, cap, n_chunk):
    b = pl.program_id(0)
    c = pl.program_id(1)
    n_e = rank_ref.shape[1]

    @pl.when(c == 0)
    def _():
        xe_ref[...] = jnp.zeros_like(xe_ref)
        g_ref[...] = jnp.zeros_like(g_ref)

    x = x_ref[0]
    wins = _slot_windows(cnt_ref, b, c, n_e, n_chunk, cap)
    hots = [_one_hot_t(rank_ref[0, e:e + 1, :], wins[e][0]) for e in range(n_e)]
    stacked = jnp.concatenate([h.astype(BF16) for h in hots], axis=0)
    rows = jnp.dot(stacked, x, preferred_element_type=F32)
    for e in range(n_e):
        sl = pl.ds(wins[e][0], ROUTE_WIN)
        xe_ref[e, sl, :] = xe_ref[e, sl, :] + rows[e * ROUTE_WIN:(e + 1) * ROUTE_WIN].astype(BF16)
        g_ref[e, sl, :] = g_ref[e, sl, :] + jnp.sum(jnp.where(hots[e], aff_ref[0, e:e + 1, :], 0.0),
                                                    axis=1, keepdims=True)

    @pl.when(sum(extra for _, extra in wins) > 0)
    def _():
        for e in range(n_e):
            lo_al, extra = wins[e]

            def spill(k, carry, e=e, lo_al=lo_al):
                start, first = _spill_window(k, lo_al, cap)
                hot = _one_hot_t(rank_ref[0, e:e + 1, :], start, first)
                sl2 = pl.ds(start, ROUTE_WIN)
                xe_ref[e, sl2, :] = xe_ref[e, sl2, :] + jnp.dot(hot.astype(BF16), x_ref[0],
                                                               preferred_element_type=F32).astype(BF16)
                g_ref[e, sl2, :] = g_ref[e, sl2, :] + jnp.sum(jnp.where(hot, aff_ref[0, e:e + 1, :], 0.0),
                                                              axis=1, keepdims=True)
                return carry

            lax.fori_loop(1, extra + 1, spill, 0)


def _dispatch_call(cnt_flat, xn2, rank_t, aff_t, cap):
    nb, n, d = xn2.shape
    n_e = rank_t.shape[1]
    n_chunk = n // ROUTE_CHUNK
    grid_spec = pltpu.PrefetchScalarGridSpec(
        num_scalar_prefetch=1,
        grid=(nb, n_chunk),
        in_specs=[
            pl.BlockSpec((1, ROUTE_CHUNK, d), lambda b, c, cnt: (b, c, 0)),
            pl.BlockSpec((1, n_e, ROUTE_CHUNK), lambda b, c, cnt: (b, 0, c)),
            pl.BlockSpec((1, n_e, ROUTE_CHUNK), lambda b, c, cnt: (b, 0, c)),
        ],
        out_specs=[
            pl.BlockSpec((n_e, cap, d), lambda b, c, cnt: (0, b, 0)),
            pl.BlockSpec((n_e, cap, 1), lambda b, c, cnt: (0, b, 0)),
        ],
    )
    return pl.pallas_call(
        functools.partial(_dispatch_kernel, cap=cap, n_chunk=n_chunk),
        grid_spec=grid_spec,
        out_shape=[jax.ShapeDtypeStruct((n_e, nb * cap, d), BF16), jax.ShapeDtypeStruct((n_e, nb * cap, 1), F32)],
        compiler_params=_cparams(("arbitrary", "arbitrary")),
        name="moe_dispatch",
    )(cnt_flat, xn2, rank_t, aff_t)


def _ffn_kernel(xe_ref, g_ref, wg_ref, wu_ref, wd_ref, o_ref, acc_s):
    f = pl.program_id(1)
    m = pl.program_id(2)
    tm = xe_ref.shape[1]
    xe = xe_ref[0]
    h1 = jnp.dot(xe, wg_ref[0].astype(BF16), preferred_element_type=F32)
    h2 = jnp.dot(xe, wu_ref[0].astype(BF16), preferred_element_type=F32)
    hid = (h1 * _sigmoid(h1) * h2).astype(BF16)
    part = jnp.dot(hid, wd_ref[0].astype(BF16), preferred_element_type=F32)
    rows = pl.ds(pl.multiple_of(m * tm, tm), tm)
    last = pl.num_programs(1) - 1

    @pl.when(f == 0)
    def _():
        acc_s[rows, :] = part

    @pl.when((f > 0) & (f < last))
    def _():
        acc_s[rows, :] += part

    @pl.when(f == last)
    def _():
        o_ref[0] = ((acc_s[rows, :] + part) * g_ref[0]).astype(BF16)


def _ffn_call(xe, g, w_gate, w_up, w_down):
    e, m, d = xe.shape
    dff = w_gate.shape[2]
    tm = min(FFN_TM, m)
    n_f = dff // FFN_TF
    assert n_f >= 2 and m % tm == 0
    out_row = lambda fi, mi: jnp.where(fi == n_f - 1, mi, 0)
    return pl.pallas_call(
        _ffn_kernel,
        grid=(e, n_f, m // tm),
        in_specs=[
            pl.BlockSpec((1, tm, d), lambda ei, fi, mi: (ei, mi, 0)),
            pl.BlockSpec((1, tm, 1), lambda ei, fi, mi: (ei, mi, 0)),
            pl.BlockSpec((1, d, FFN_TF), lambda ei, fi, mi: (ei, 0, fi)),
            pl.BlockSpec((1, d, FFN_TF), lambda ei, fi, mi: (ei, 0, fi)),
            pl.BlockSpec((1, FFN_TF, d), lambda ei, fi, mi: (ei, fi, 0)),
        ],
        out_specs=pl.BlockSpec((1, tm, d), lambda ei, fi, mi: (ei, out_row(fi, mi), 0)),
        out_shape=jax.ShapeDtypeStruct((e, m, d), BF16),
        scratch_shapes=[pltpu.VMEM((m, d), F32)],
        compiler_params=_cparams(("arbitrary", "arbitrary", "arbitrary")),
        name="expert_ffn",
    )(xe, g, w_gate, w_up, w_down)


def _combine_kernel(cnt_ref, ye_ref, rank_ref, x1_ref, g2_ref, fn_ref, o_ref, acc_s, *, cap, n_chunk):
    b = pl.program_id(0)
    c = pl.program_id(1)
    n_e = rank_ref.shape[1]
    tn_dims = (((0,), (0,)), ((), ()))
    wins = _slot_windows(cnt_ref, b, c, n_e, n_chunk, cap)
    hots = [_one_hot_t(rank_ref[0, e:e + 1, :], wins[e][0]).astype(BF16) for e in range(n_e)]
    ys = [ye_ref[e, pl.ds(wins[e][0], ROUTE_WIN), :] for e in range(n_e)]
    moe = lax.dot_general(jnp.concatenate(hots, axis=0), jnp.concatenate(ys, axis=0), tn_dims,
                          preferred_element_type=F32)

    def finish(m):
        x = x1_ref[0] + g2_ref[0] * m
        o_ref[0] = _rms(x) * fn_ref[0]

    any_spill = sum(extra for _, extra in wins) > 0

    @pl.when(jnp.logical_not(any_spill))
    def _():
        finish(moe)

    @pl.when(any_spill)
    def _():
        acc_s[...] = moe
        for e in range(n_e):
            lo_al, extra = wins[e]

            def spill(k, carry, e=e, lo_al=lo_al):
                start, first = _spill_window(k, lo_al, cap)
                hot = _one_hot_t(rank_ref[0, e:e + 1, :], start, first).astype(BF16)
                acc_s[...] += lax.dot_general(hot, ye_ref[e, pl.ds(start, ROUTE_WIN), :], tn_dims,
                                              preferred_element_type=F32)
                return carry

            lax.fori_loop(1, extra + 1, spill, 0)
        finish(acc_s[...])


def _combine_call(cnt_flat, ye, rank_t, x1, g2, fnorm, cap):
    nb, n, d = x1.shape
    n_e = rank_t.shape[1]
    n_chunk = n // ROUTE_CHUNK
    grid_spec = pltpu.PrefetchScalarGridSpec(
        num_scalar_prefetch=1,
        grid=(nb, n_chunk),
        in_specs=[
            pl.BlockSpec((n_e, cap, d), lambda b, c, cnt: (0, b, 0)),
            pl.BlockSpec((1, n_e, ROUTE_CHUNK), lambda b, c, cnt: (b, 0, c)),
            pl.BlockSpec((1, ROUTE_CHUNK, d), lambda b, c, cnt: (b, c, 0)),
            pl.BlockSpec((1, 1, d), lambda b, c, cnt: (b, 0, 0)),
            pl.BlockSpec((1, 1, d), lambda b, c, cnt: (0, 0, 0)),
        ],
        out_specs=pl.BlockSpec((1, ROUTE_CHUNK, d), lambda b, c, cnt: (b, c, 0)),
        scratch_shapes=[pltpu.VMEM((ROUTE_CHUNK, d), F32)],
    )
    return pl.pallas_call(
        functools.partial(_combine_kernel, cap=cap, n_chunk=n_chunk),
        grid_spec=grid_spec,
        out_shape=jax.ShapeDtypeStruct((nb, n, d), F32),
        compiler_params=_cparams(("arbitrary", "arbitrary")),
        name="moe_combine_final_norm",
    )(cnt_flat, ye, rank_t, x1, g2, fnorm.reshape(1, 1, d))


def kernel(x, c, ctx, c_ctx, w_mod, b_mod, w_in, b_in, conv_w, conv_b, lru_wa, lru_ba, lru_wi, lru_bi,
           lru_lambda, na_rpb, w_proj_rnn, w_proj_na, w_out, w_router, w_exp_gate, w_exp_up, w_exp_down,
           final_norm):
    nb, n, d = x.shape
    l = ctx.shape[1]
    assert w_mod.shape[0] == 1, "single-layer problem"
    assert nb == SUBLANES and n % LRU_TN == 0 and l % LRU_TN == 0 and d % LRU_SLAB == 0
    lyr = 0

    cc = jnp.concatenate([c, c_ctx[None], jnp.zeros((2 * SUBLANES - nb - 1, d), F32)], axis=0)
    mod = _mod_call(cc, w_mod[lyr], b_mod[lyr])
    sh1, sc1, g1, sh2, sc2, g2 = [m[:nb, None, :] for m in jnp.split(mod, 6, axis=-1)]
    csh1, csc1 = [jnp.broadcast_to(m[nb:nb + 1, None, :], (nb, 1, d)) for m in jnp.split(mod, 6, axis=-1)[:2]]

    w_in_b = w_in[lyr].astype(BF16)
    b_in_r = b_in[lyr].reshape(1, -1)
    tables = _rope_tables(n)
    gy, xr_t, q, k, v, sgr, sgn = _inproj_call(
        x, sh1, sc1, w_in_b, b_in_r, tables,
        ("gelu", "tok_major", "rope_q", "rope_k", "bf16", "sigmoid", "sigmoid"), tuple(range(7)), "in_proj_latent",
        col_blocked=(3, 4))
    xrc_t, kc, vc = _inproj_call(
        ctx, csh1, csc1, w_in_b, b_in_r, tables, ("tok_major", "bf16", "bf16"), (1, 3, 4), "in_proj_context")

    gw = [_lru_gate_weights(lru_wa[lyr, dr], lru_ba[lyr, dr], lru_wi[lyr, dr], lru_bi[lyr, dr], lru_lambda[lyr, dr])
          for dr in range(2)]
    zeros_h = jnp.zeros((nb, d), F32)
    _, hc_f = _lru_call(xrc_t, conv_w[lyr], conv_b[lyr], *gw[0], zeros_h, reverse=False, name="rglru_ctx_fwd")
    _, hc_b = _lru_call(xrc_t, conv_w[lyr], conv_b[lyr], *gw[1], zeros_h, reverse=True, name="rglru_ctx_bwd")
    hf_t, _ = _lru_call(xr_t, conv_w[lyr], conv_b[lyr], *gw[0], hc_f, reverse=False, name="rglru_fwd")
    y_rnn, _ = _lru_call(xr_t, conv_w[lyr], conv_b[lyr], *gw[1], hc_b, hf_t, gy, reverse=True, name="rglru_bwd")

    y_na = _na_call(q, k, v, kc, vc, _na_bias_table(na_rpb[lyr]))

    x1, xn2, lg_tiles = _mix_call(y_rnn, y_na, sgr, sgn, x, g1, sh2, sc2,
                                  w_proj_rnn[lyr].astype(BF16), w_proj_na[lyr].astype(BF16),
                                  w_out[lyr].astype(BF16), w_router[lyr].T.astype(BF16))
    n_e = w_router.shape[2]
    lg_t = lg_tiles.reshape(n // TQ, n_e, nb, TQ).transpose(2, 1, 0, 3).reshape(nb, n_e, n)

    cap = EC_CAPACITY * n // N_EXPERTS
    n_chunk = n // ROUTE_CHUNK
    assert cap % SLOT_ALIGN == 0 and cap >= ROUTE_WIN and n % ROUTE_CHUNK == 0
    rank_t, aff_t, cnt = _route_call(lg_t, cap)
    cnt_flat = cnt[:, :, :n_chunk + 1].reshape(-1)
    xe, ge = _dispatch_call(cnt_flat, xn2, rank_t, aff_t, cap)
    ye = _ffn_call(xe, ge, w_exp_gate[lyr], w_exp_up[lyr], w_exp_down[lyr])
    return _combine_call(cnt_flat, ye, rank_t, x1, g2, final_norm, cap)
```

```python
import functools

import numpy as np
import jax
import jax.numpy as jnp
from jax import lax
from jax.experimental import pallas as pl
from jax.experimental.pallas import tpu as pltpu

F32 = jnp.float32
BF16 = jnp.bfloat16

GRID_W = 64
N_HEADS = 16
HEAD_DIM = 64
N_LRU_BLOCKS = 16
CONV_W = 4
LRU_C = 8.0
WIN_R = 8
WIN_C = 16
ROPE_BASE = 10000.0
N_EXPERTS = 16
EC_CAPACITY = 2
EPS = 1e-6
NEG_INF = -1e30

LANES = 128
SUBLANES = 8
VMEM_LIMIT = 56 * 1024 * 1024

TQ = 128
LRU_SLAB = 256
LRU_TN = 128
FFN_TM = 2048
FFN_TF = 512
FFN_TN = 256
NA_CHUNK = 2
ROUTE_CHUNK = 256
ROUTE_WIN = 64
SLOT_ALIGN = 16
MIN_NORMAL_BITS = 0x00800000
LOG2_E = 1.4426950408889634


def _cparams(sem):
    return pltpu.CompilerParams(dimension_semantics=sem, vmem_limit_bytes=VMEM_LIMIT)


def _sigmoid(x):
    return 0.5 * jnp.tanh(0.5 * x) + 0.5


def _rms(x):
    return x * lax.rsqrt(jnp.mean(x * x, axis=-1, keepdims=True) + EPS)


def _mod_kernel(c_ref, w_ref, b_ref, o_ref):
    c = c_ref[...]
    s = c * jax.nn.sigmoid(c)
    o_ref[...] = jnp.dot(s, w_ref[...], preferred_element_type=F32) + b_ref[...]


def _mod_call(cc, w_mod, b_mod):
    rows, d = cc.shape
    n_out = w_mod.shape[1]
    tn = 1024
    return pl.pallas_call(
        _mod_kernel,
        grid=(n_out // tn,),
        in_specs=[
            pl.BlockSpec((rows, d), lambda j: (0, 0)),
            pl.BlockSpec((d, tn), lambda j: (0, j)),
            pl.BlockSpec((1, tn), lambda j: (0, j)),
        ],
        out_specs=pl.BlockSpec((rows, tn), lambda j: (0, j)),
        out_shape=jax.ShapeDtypeStruct((rows, n_out), F32),
        compiler_params=_cparams(("arbitrary",)),
        name="adaln_mod",
    )(cc, w_mod, b_mod.reshape(1, n_out))


def _rope(z, cos, sa, sb):
    outs = []
    for g in range(z.shape[1] // LANES):
        zg = z[:, g * LANES:(g + 1) * LANES]
        outs.append(zg * cos + pltpu.roll(zg, 16, 1) * sa + pltpu.roll(zg, LANES - 16, 1) * sb)
    return jnp.concatenate(outs, axis=1)


def _inproj_kernel(*refs, kinds, groups):
    n_out = len(kinds)
    x_ref, sh_ref, sc_ref, w_ref, b_ref, cos_ref, sa_ref, sb_ref = refs[:8]
    out_refs = refs[8:8 + n_out]
    nb, tq, d = x_ref.shape
    rows = nb * tq
    xn = _rms(x_ref[...]) * (1.0 + sc_ref[...]) + sh_ref[...]
    xb = xn.reshape(rows, d).astype(BF16)

    def tiled(t_ref):
        return jnp.broadcast_to(t_ref[...][None], (nb, tq, LANES)).reshape(rows, LANES)

    def store_tokens(o_ref, val):
        if o_ref.ndim == 3:
            o_ref[...] = val.reshape(nb, tq, d)
        else:
            n_cb = o_ref.shape[1]
            v5 = val.reshape(nb, tq // GRID_W, n_cb, WIN_C, d)
            for gr in range(tq // GRID_W):
                o_ref[:, :, gr * WIN_C:(gr + 1) * WIN_C, :] = v5[:, gr]

    for g, kind, o_ref in zip(groups, kinds, out_refs):
        z = jnp.dot(xb, w_ref[:, g * d:(g + 1) * d], preferred_element_type=F32) + b_ref[:, g * d:(g + 1) * d]
        if kind == "gelu":
            o_ref[...] = jax.nn.gelu(z, approximate=True).astype(BF16).reshape(nb, tq, d)
        elif kind == "tok_major":
            o_ref[...] = jnp.swapaxes(z.reshape(nb, tq, d), 0, 1).reshape(rows, d).astype(BF16)
        elif kind == "rope_q":
            r = _rope(z * (HEAD_DIM ** -0.5), tiled(cos_ref), tiled(sa_ref), tiled(sb_ref))
            o_ref[...] = r.astype(BF16).reshape(nb, tq, d)
        elif kind == "rope_k":
            r = _rope(z, tiled(cos_ref), tiled(sa_ref), tiled(sb_ref))
            store_tokens(o_ref, r.astype(BF16))
        elif kind == "bf16":
            store_tokens(o_ref, z.astype(BF16))
        elif kind == "sigmoid":
            o_ref[...] = _sigmoid(z).astype(BF16).reshape(nb, tq, d)
        else:
            raise ValueError(kind)


def _inproj_call(x, sh, sc, w, b, tables, kinds, groups, name, col_blocked=()):
    nb, n, d = x.shape
    assert len(groups) == len(kinds) and w.shape[0] == d and max(groups) * d < w.shape[1]
    cos, sa, sb = tables
    bat = lambda i: (0, i, 0)
    const3 = lambda i: (0, 0, 0)
    const2 = lambda i: (0, 0)
    out_shapes, out_specs = [], []
    for kind in kinds:
        if kind == "tok_major":
            out_shapes.append(jax.ShapeDtypeStruct((n * nb, d), BF16))
            out_specs.append(pl.BlockSpec((TQ * nb, d), lambda i: (i, 0)))
        elif len(out_shapes) in col_blocked:
            assert TQ % GRID_W == 0
            n_cb = GRID_W // WIN_C
            out_shapes.append(jax.ShapeDtypeStruct((nb, n_cb, n // n_cb, d), BF16))
            out_specs.append(pl.BlockSpec((nb, n_cb, TQ // n_cb, d), lambda i: (0, 0, i, 0)))
        else:
            out_shapes.append(jax.ShapeDtypeStruct((nb, n, d), BF16))
            out_specs.append(pl.BlockSpec((nb, TQ, d), bat))
    rows_per_table = cos.shape[0] // TQ
    return pl.pallas_call(
        functools.partial(_inproj_kernel, kinds=kinds, groups=groups),
        grid=(n // TQ,),
        in_specs=[
            pl.BlockSpec((nb, TQ, d), bat),
            pl.BlockSpec(sh.shape, const3),
            pl.BlockSpec(sc.shape, const3),
            pl.BlockSpec(w.shape, const2, pipeline_mode=pl.Buffered(1)),
            pl.BlockSpec(b.shape, const2),
            pl.BlockSpec((TQ, LANES), lambda i: (i % rows_per_table, 0)),
            pl.BlockSpec((TQ, LANES), lambda i: (i % rows_per_table, 0)),
            pl.BlockSpec((TQ, LANES), lambda i: (i % rows_per_table, 0)),
        ],
        out_specs=out_specs,
        out_shape=out_shapes,
        compiler_params=_cparams(("arbitrary",)),
        name=name,
    )(x, sh, sc, w, b, cos, sa, sb)


def _rope_tables(n):
    t = np.arange(n)
    row = (t // GRID_W).astype(np.float32)
    col = (t % GRID_W).astype(np.float32)
    n_freq = HEAD_DIM // 4
    inv = (np.float32(ROPE_BASE) ** (-np.arange(n_freq, dtype=np.float32) / np.float32(n_freq))).astype(np.float32)
    ang_r = (row[:, None] * inv).astype(np.float32)
    ang_c = (col[:, None] * inv).astype(np.float32)
    zero = np.zeros_like(ang_r)
    cos_h = np.concatenate([np.cos(ang_r), np.cos(ang_r), np.cos(ang_c), np.cos(ang_c)], axis=1)
    sa_h = np.concatenate([zero, np.sin(ang_r), zero, np.sin(ang_c)], axis=1)
    sb_h = np.concatenate([-np.sin(ang_r), zero, -np.sin(ang_c), zero], axis=1)
    rep = LANES // HEAD_DIM
    return tuple(jnp.asarray(np.tile(a.astype(np.float32), (1, rep))) for a in (cos_h, sa_h, sb_h))


def _lru_kernel(*refs, tn, nt, reverse, fuse_out):
    if fuse_out:
        (xp_ref, xc_ref, xn_ref, cw_ref, cb_ref, wg_ref, bg_ref, lam_ref, h0_ref, hf_ref, gy_ref,
         out_ref, hfin_ref, a_s, b_s, hcar_s) = refs
    else:
        (xp_ref, xc_ref, xn_ref, cw_ref, cb_ref, wg_ref, bg_ref, lam_ref, h0_ref,
         out_ref, hfin_ref, a_s, b_s, hcar_s) = refs
    i = pl.program_id(0)
    ti = (nt - 1 - i) if reverse else i
    nb, d = hcar_s.shape

    @pl.when(i == 0)
    def _():
        hcar_s[...] = h0_ref[...]

    def tokens(ref):
        return ref[...].astype(F32).reshape(ref.shape[0] // nb, nb, d)

    x = tokens(xc_ref)
    xp = jnp.where(ti == 0, 0.0, tokens(xp_ref))
    xnx = jnp.where(ti == nt - 1, 0.0, tokens(xn_ref)[:1])
    ext = jnp.concatenate([xp, x, xnx], axis=0)
    xc = cb_ref[...]
    for k in range(CONV_W):
        xc = xc + ext[k:k + tn] * cw_ref[k]
    rows = tn * nb
    xc2 = xc.reshape(rows, d)
    for s in range(d // LRU_SLAB):
        sl = slice(s * LRU_SLAB, (s + 1) * LRU_SLAB)
        xs = xc2[:, sl]
        half_gates = jnp.dot(xs.astype(BF16), wg_ref[s], preferred_element_type=F32) + bg_ref[s]
        t_r = jnp.tanh(half_gates[:, :LRU_SLAB])
        ig = 0.5 * jnp.tanh(half_gates[:, LRU_SLAB:]) + 0.5
        z = -lam_ref[s]
        softplus = jnp.maximum(z, 0.0) + jnp.log1p(jnp.exp(-jnp.abs(z)))
        k = (-0.5 * LRU_C * LOG2_E) * softplus
        a = jnp.exp2(t_r * k + k)
        b = jnp.sqrt(1.0 - a * a) * (ig * xs)
        a_s[:, :, sl] = a.reshape(tn, nb, LRU_SLAB)
        b_s[:, :, sl] = b.reshape(tn, nb, LRU_SLAB)

    unroll = 8

    def step(j, h):
        for k in range(unroll):
            jj = j * unroll + k
            t = (tn - 1 - jj) if reverse else jj
            h = a_s[t] * h + b_s[t]
            a_s[t] = h
        return h

    h = lax.fori_loop(0, tn // unroll, step, hcar_s[...])
    hcar_s[...] = h
    hfin_ref[...] = h
    if fuse_out:
        hs = jnp.swapaxes(tokens(hf_ref) + a_s[...], 0, 1)
        out_ref[...] = (hs * gy_ref[...].astype(F32)).astype(BF16)
    else:
        out_ref[...] = a_s[...].reshape(tn * nb, d).astype(BF16)


def _lru_call(xr_t, conv_w, conv_b, wg, bg, lam, h0, hf=None, gy=None, *, reverse, name):
    nb, d = h0.shape
    n = xr_t.shape[0] // nb
    tn = LRU_TN
    nt = n // tn
    halo = 2
    fuse_out = hf is not None
    tile = (lambda i: nt - 1 - i) if reverse else (lambda i: i)
    n_slab = d // LRU_SLAB
    in_specs = [
        pl.BlockSpec((halo * nb, d), lambda i: (jnp.maximum(tile(i) * (tn // halo) - 1, 0), 0)),
        pl.BlockSpec((tn * nb, d), lambda i: (tile(i), 0)),
        pl.BlockSpec((halo * nb, d), lambda i: (jnp.minimum((tile(i) + 1) * (tn // halo), n // halo - 1), 0)),
        pl.BlockSpec((CONV_W, 1, d), lambda i: (0, 0, 0)),
        pl.BlockSpec((1, d), lambda i: (0, 0)),
        pl.BlockSpec((n_slab, LRU_SLAB, 2 * LRU_SLAB), lambda i: (0, 0, 0)),
        pl.BlockSpec((n_slab, 1, 2 * LRU_SLAB), lambda i: (0, 0, 0)),
        pl.BlockSpec((n_slab, 1, LRU_SLAB), lambda i: (0, 0, 0)),
        pl.BlockSpec((nb, d), lambda i: (0, 0)),
    ]
    args = [xr_t, xr_t, xr_t, conv_w.reshape(CONV_W, 1, d), conv_b.reshape(1, d), wg, bg, lam, h0]
    if fuse_out:
        in_specs += [
            pl.BlockSpec((tn * nb, d), lambda i: (tile(i), 0)),
            pl.BlockSpec((nb, tn, d), lambda i: (0, tile(i), 0)),
        ]
        args += [hf, gy]
        out_shape0 = jax.ShapeDtypeStruct((nb, n, d), BF16)
        out_spec0 = pl.BlockSpec((nb, tn, d), lambda i: (0, tile(i), 0))
    else:
        out_shape0 = jax.ShapeDtypeStruct((n * nb, d), BF16)
        out_spec0 = pl.BlockSpec((tn * nb, d), lambda i: (tile(i), 0))
    return pl.pallas_call(
        functools.partial(_lru_kernel, tn=tn, nt=nt, reverse=reverse, fuse_out=fuse_out),
        grid=(nt,),
        in_specs=in_specs,
        out_specs=[out_spec0, pl.BlockSpec((nb, d), lambda i: (0, 0))],
        out_shape=[out_shape0, jax.ShapeDtypeStruct((nb, d), F32)],
        scratch_shapes=[
            pltpu.VMEM((tn, nb, d), F32),
            pltpu.VMEM((tn, nb, d), F32),
            pltpu.VMEM((nb, d), F32),
        ],
        compiler_params=_cparams(("arbitrary",)),
        name=name,
    )(*args)


def _lru_gate_weights(wa, ba, wi, bi, lam):
    n_blk, blk, _ = wa.shape
    d = n_blk * blk
    n_slab = d // LRU_SLAB
    per = LRU_SLAB // blk
    eye = jnp.eye(per, dtype=F32)

    def dense_slabs(w):
        return jnp.einsum("saij,ac->saicj", w.reshape(n_slab, per, blk, blk), eye).reshape(n_slab, LRU_SLAB, LRU_SLAB)

    wg = (0.5 * jnp.concatenate([dense_slabs(wa), dense_slabs(wi)], axis=2).astype(BF16)).astype(BF16)
    bg = 0.5 * jnp.concatenate([ba.reshape(n_slab, 1, LRU_SLAB), bi.reshape(n_slab, 1, LRU_SLAB)], axis=2)
    return wg, bg, lam.reshape(n_slab, 1, LRU_SLAB)


def _na_blocks(g):
    lo = min(max(8 * g - WIN_C // 2, 0) // WIN_C, GRID_W // WIN_C - 2)
    return (lo, lo + 1)


def _na_kernel(q_ref, k_ref, v_ref, kc_ref, vc_ref, tab_ref, o_ref, s_s, sc_s, p_s, pc_s, inv_s, *, rows):
    win = WIN_R * GRID_W
    ch = NA_CHUNK
    n_chunk = rows // ch
    qh = 2 * GRID_W
    n_cb = GRID_W // WIN_C
    blk = WIN_R * WIN_C
    grp = SUBLANES
    lane = lax.broadcasted_iota(jnp.int32, (GRID_W, LANES), 1)
    first_head = lane < HEAD_DIM
    nt_dims = (((1,), (1,)), ((), ()))

    def row_start(r):
        return pl.multiple_of(r * GRID_W, GRID_W)

    def window(ref, rs):
        run = pl.ds(pl.multiple_of(rs * WIN_C, WIN_C), blk)
        return jnp.concatenate([ref[0, cb, run, :] for cb in range(n_cb)], axis=0)

    def groups():
        for h in range(2):
            for g in range(GRID_W // grp):
                yield h, g * grp, _na_blocks(g)

    p_s[...] = jnp.zeros_like(p_s)

    def stage1(c, slot):
        qs_all = []
        for u in range(ch):
            r = c * ch + u
            rs = jnp.clip(r - WIN_R // 2, 0, rows - WIN_R)
            q = q_ref[0, pl.ds(row_start(r), GRID_W), :]
            zero = jnp.zeros_like(q)
            qs = jnp.concatenate([jnp.where(first_head, q, zero), jnp.where(first_head, zero, q)], axis=0)
            qs_all.append(qs)
            s = lax.dot_general(qs, window(k_ref, rs), nt_dims, preferred_element_type=F32)
            d0 = rs - r + (WIN_R - 1)
            for h, q0, cbs in groups():
                for cb in cbs:
                    rw = slice(h * GRID_W + q0, h * GRID_W + q0 + grp)
                    ln = slice(cb * blk, (cb + 1) * blk)
                    s_s[slot, u, rw, ln] = s[rw, ln] + tab_ref[0, h, d0, q0:q0 + grp, ln]
        sc_s[slot] = lax.dot_general(jnp.concatenate(qs_all, axis=0), kc_ref[0], nt_dims,
                                     preferred_element_type=F32)

    def stage2(slot):
        for u in range(ch):
            for r0 in range(0, qh, 2 * grp):
                parts = {}
                pcs, invs = [], []
                for half in range(2):
                    ra = r0 + half * grp
                    cbs = _na_blocks((ra % GRID_W) // grp)
                    sl = [s_s[slot, u, ra:ra + grp, cb * blk:(cb + 1) * blk] for cb in cbs]
                    sc = sc_s[slot, u * qh + ra:u * qh + ra + grp, :]
                    m = jnp.maximum(jnp.max(jnp.maximum(sl[0], sl[1]), axis=1, keepdims=True),
                                    jnp.max(sc, axis=1, keepdims=True))
                    ps = [jnp.exp(x - m) for x in sl]
                    pc = jnp.exp(sc - m)
                    den = jnp.sum(ps[0] + ps[1], axis=1, keepdims=True) + jnp.sum(pc, axis=1, keepdims=True)
                    for cb, pb in zip(cbs, ps):
                        parts[(half, cb)] = pb
                    pcs.append(pc)
                    invs.append(jnp.broadcast_to(1.0 / den, (grp, LANES)))
                zero = jnp.zeros((grp, blk), F32)
                for cb in sorted({cb for _, cb in parts}):
                    both = jnp.concatenate([parts.get((0, cb), zero), parts.get((1, cb), zero)], axis=0)
                    p_s[slot, u, r0:r0 + 2 * grp, cb * blk:(cb + 1) * blk] = both.astype(BF16)
                pc_s[slot, u * qh + r0:u * qh + r0 + 2 * grp, :] = jnp.concatenate(pcs, axis=0).astype(BF16)
                inv_s[slot, u, r0:r0 + 2 * grp, :] = jnp.concatenate(invs, axis=0)

    def stage3(c, slot):
        oc = jnp.dot(pc_s[slot], vc_ref[0], preferred_element_type=F32)
        for u in range(ch):
            r = c * ch + u
            rs = jnp.clip(r - WIN_R // 2, 0, rows - WIN_R)
            o = (jnp.dot(p_s[slot, u], window(v_ref, rs), preferred_element_type=F32)
                 + oc[u * qh:(u + 1) * qh]) * inv_s[slot, u]
            out = jnp.where(first_head, o[:GRID_W], o[GRID_W:])
            o_ref[0, pl.ds(row_start(r), GRID_W), :] = out.astype(BF16)

    stage1(0, 0)
    stage2(0)
    stage1(1, 1)

    def body(j, carry):
        slot = j % 2
        stage3(j - 2, slot)
        stage2(1 - slot)
        stage1(j, slot)
        return carry

    lax.fori_loop(2, n_chunk, body, 0)
    last = n_chunk % 2
    stage3(n_chunk - 2, last)
    stage2(1 - last)
    stage3(n_chunk - 1, 1 - last)


def _na_call(q, k, v, kc, vc, tab):
    nb, n, d = q.shape
    l = kc.shape[1]
    rows = n // GRID_W
    n_cb = GRID_W // WIN_C
    assert k.shape == v.shape == (nb, n_cb, n // n_cb, d)
    kv_spec = pl.BlockSpec((1, n_cb, n // n_cb, LANES), lambda b, p: (b, 0, 0, p))
    n_pair = d // LANES
    win = WIN_R * GRID_W
    qh = 2 * GRID_W
    assert rows % NA_CHUNK == 0 and rows // NA_CHUNK >= 2
    blk = lambda b, p: (b, 0, p)
    return pl.pallas_call(
        functools.partial(_na_kernel, rows=rows),
        grid=(nb, n_pair),
        in_specs=[
            pl.BlockSpec((1, n, LANES), blk),
            kv_spec,
            kv_spec,
            pl.BlockSpec((1, l, LANES), blk),
            pl.BlockSpec((1, l, LANES), blk),
            pl.BlockSpec((1,) + tab.shape[1:], lambda b, p: (p, 0, 0, 0, 0)),
        ],
        out_specs=pl.BlockSpec((1, n, LANES), blk),
        out_shape=jax.ShapeDtypeStruct((nb, n, d), BF16),
        scratch_shapes=[
            pltpu.VMEM((2, NA_CHUNK, qh, win), F32),
            pltpu.VMEM((2, NA_CHUNK * qh, l), F32),
            pltpu.VMEM((2, NA_CHUNK, qh, win), BF16),
            pltpu.VMEM((2, NA_CHUNK * qh, l), BF16),
            pltpu.VMEM((2, NA_CHUNK, qh, LANES), F32),
        ],
        compiler_params=_cparams(("arbitrary", "arbitrary")),
        name="neighbourhood_attention",
    )(q, k, v, kc, vc, tab)


def _na_bias_table(rpb):
    qc = np.arange(GRID_W)
    kc = np.arange(GRID_W)
    n_cb = GRID_W // WIN_C
    cstart = np.clip(qc - WIN_C // 2, 0, GRID_W - WIN_C)
    ok = (kc[None, :] >= cstart[:, None]) & (kc[None, :] < cstart[:, None] + WIN_C)
    dc = np.clip(kc[None, :] - qc[:, None], -(WIN_C - 1), WIN_C - 1) + WIN_C - 1
    expand = (dc[None] == np.arange(2 * WIN_C - 1)[:, None, None]).astype(np.float32)
    expand = expand.reshape(2 * WIN_C - 1, GRID_W, n_cb, WIN_C)
    mask = np.where(ok, 0.0, NEG_INF).astype(np.float32).reshape(GRID_W, n_cb, 1, WIN_C)
    h = rpb.shape[0]
    rows = jnp.stack([rpb[:, d0:d0 + WIN_R].astype(F32) for d0 in range(WIN_R)], axis=1)
    tab = jnp.einsum("hdjx,xqbc->hdqbjc", rows, jnp.asarray(expand), precision=lax.Precision.HIGHEST)
    tab = tab + jnp.asarray(mask)
    return tab.reshape(h // 2, 2, WIN_R, GRID_W, WIN_R * GRID_W)


def _mix_kernel(yr_ref, yn_ref, gr_ref, gn_ref, x_ref, g1_ref, sh2_ref, sc2_ref,
                wpr_ref, wpn_ref, wo_ref, wr_ref, x1_ref, xn2_ref, lg_ref):
    nb, tq, d = x_ref.shape
    rows = nb * tq
    pr = jnp.dot(yr_ref[...].reshape(rows, d), wpr_ref[...], preferred_element_type=F32)
    pn = jnp.dot(yn_ref[...].reshape(rows, d), wpn_ref[...], preferred_element_type=F32)
    mix = gr_ref[...].reshape(rows, d).astype(F32) * pr + gn_ref[...].reshape(rows, d).astype(F32) * pn
    o = jnp.dot(mix.astype(BF16), wo_ref[...], preferred_element_type=F32)
    x1 = x_ref[...] + g1_ref[...] * o.reshape(nb, tq, d)
    x1_ref[...] = x1
    xb = (_rms(x1) * (1.0 + sc2_ref[...]) + sh2_ref[...]).astype(BF16)
    xn2_ref[...] = xb
    lg_ref[0] = lax.dot_general(wr_ref[...], xb.reshape(rows, d), (((1,), (1,)), ((), ())),
                                preferred_element_type=F32)


def _mix_call(yr, yn, gr, gn, x, g1, sh2, sc2, wpr, wpn, wo, wr):
    nb, n, d = x.shape
    bat = lambda i: (0, i, 0)
    const3 = lambda i: (0, 0, 0)
    const2 = lambda i: (0, 0)
    act = pl.BlockSpec((nb, TQ, d), bat)
    vec = pl.BlockSpec((nb, 1, d), const3)
    wsp = lambda w: pl.BlockSpec(w.shape, const2, pipeline_mode=pl.Buffered(1))
    return pl.pallas_call(
        _mix_kernel,
        grid=(n // TQ,),
        in_specs=[act, act, act, act, act, vec, vec, vec, wsp(wpr), wsp(wpn), wsp(wo), wsp(wr)],
        out_specs=[act, act, pl.BlockSpec((1, wr.shape[0], nb * TQ), lambda i: (i, 0, 0))],
        out_shape=[
            jax.ShapeDtypeStruct((nb, n, d), F32),
            jax.ShapeDtypeStruct((nb, n, d), BF16),
            jax.ShapeDtypeStruct((n // TQ, wr.shape[0], nb * TQ), F32),
        ],
        compiler_params=_cparams(("arbitrary",)),
        name="merge_out_norm_router",
    )(yr, yn, gr, gn, x, g1, sh2, sc2, wpr, wpn, wo, wr)


def _route_kernel(lg_ref, rank_ref, aff_ref, cnt_ref, *, cap):
    lg = lg_ref[0]
    n_e, n = lg.shape
    ex = jnp.exp(lg - jnp.max(lg, axis=0, keepdims=True))
    aff = ex / jnp.sum(ex, axis=0, keepdims=True)
    aff_ref[0] = aff

    def as_float(bits):
        return lax.bitcast_convert_type(bits, F32)

    thr = jnp.zeros((n_e, 1), jnp.int32)
    for bit in range(30, -1, -1):
        cand = thr | (1 << bit)
        cnt = jnp.sum((aff >= as_float(cand)).astype(F32), axis=1, keepdims=True)
        thr = jnp.where(cnt >= cap, cand, thr)
    thr = jnp.where(thr < MIN_NORMAL_BITS, 0, thr)
    gt = aff >= as_float(jnp.where(thr == 0, MIN_NORMAL_BITS, thr + 1))
    eq = (aff >= as_float(thr)) & jnp.logical_not(gt)
    need = cap - jnp.sum(gt.astype(F32), axis=1, keepdims=True)

    blk = ROUTE_CHUNK
    row = lax.broadcasted_iota(jnp.int32, (blk, blk), 0)
    col = lax.broadcasted_iota(jnp.int32, (blk, blk), 1)
    upper = (row <= col).astype(BF16)

    def cumsum_tokens(mask):
        off = jnp.zeros((n_e, 1), F32)
        outs, offs = [], []
        for c in range(n // blk):
            x = mask[:, c * blk:(c + 1) * blk].astype(BF16)
            cs = jnp.dot(x, upper, preferred_element_type=F32) + off
            offs.append(off)
            outs.append(cs)
            off = cs[:, blk - 1:blk]
        offs.append(off)
        return jnp.concatenate(outs, axis=1), offs

    cum_eq, _ = cumsum_tokens(eq)
    sel = gt | (eq & ((cum_eq - eq.astype(F32)) < need))
    cum_sel, offs = cumsum_tokens(sel)
    rank_ref[0] = jnp.where(sel, cum_sel - 1.0, -1.0).astype(jnp.int32)
    pad = jnp.zeros((n_e, LANES - len(offs)), F32)
    cnt_ref[0] = jnp.concatenate(offs + [pad], axis=1).astype(jnp.int32)


def _route_call(lg_t, cap):
    nb, n_e, n = lg_t.shape
    blk3 = lambda b: (b, 0, 0)
    return pl.pallas_call(
        functools.partial(_route_kernel, cap=cap),
        grid=(nb,),
        in_specs=[pl.BlockSpec((1, n_e, n), blk3)],
        out_specs=[pl.BlockSpec((1, n_e, n), blk3), pl.BlockSpec((1, n_e, n), blk3),
                   pl.BlockSpec((1, n_e, LANES), blk3)],
        out_shape=[jax.ShapeDtypeStruct((nb, n_e, n), jnp.int32), jax.ShapeDtypeStruct((nb, n_e, n), F32),
                   jax.ShapeDtypeStruct((nb, n_e, LANES), jnp.int32)],
        compiler_params=_cparams(("arbitrary",)),
        name="route_select",
    )(lg_t)


def _slot_windows(cnt_ref, b, c, n_e, n_chunk, cap):
    wins = []
    for e in range(n_e):
        base = (b * n_e + e) * (n_chunk + 1) + c
        lo = cnt_ref[base]
        hi = cnt_ref[base + 1]
        lo_al = jnp.minimum((lo // SLOT_ALIGN) * SLOT_ALIGN, cap - ROUTE_WIN)
        extra = jnp.maximum((hi - lo_al + ROUTE_WIN - 1) // ROUTE_WIN - 1, 0)
        wins.append((pl.multiple_of(lo_al, SLOT_ALIGN), extra))
    return wins


def _one_hot_t(rank_row, start, first=None):
    slot = start + lax.broadcasted_iota(jnp.int32, (ROUTE_WIN, rank_row.shape[1]), 0)
    hit = rank_row == slot
    if first is not None:
        hit = hit & (slot >= first)
    return hit


def _spill_window(k, lo_al, cap):
    first = lo_al + k * ROUTE_WIN
    start = pl.multiple_of(jnp.minimum(first, cap - ROUTE_WIN), SLOT_ALIGN)
    return start, first


def _dispatch_kernel(cnt_ref, x_ref, rank_ref, aff_ref, xe_ref, g_ref, *, cap, n_chunk):
    b = pl.program_id(0)
    c = pl.program_id(1)
    n_e = rank_ref.shape[1]

    @pl.when(c == 0)
    def _():
        xe_ref[...] = jnp.zeros_like(xe_ref)
        g_ref[...] = jnp.zeros_like(g_ref)

    x = x_ref[0]
    wins = _slot_windows(cnt_ref, b, c, n_e, n_chunk, cap)
    hots = [_one_hot_t(rank_ref[0, e:e + 1, :], wins[e][0]) for e in range(n_e)]
    stacked = jnp.concatenate([h.astype(BF16) for h in hots], axis=0)
    rows = jnp.dot(stacked, x, preferred_element_type=F32)
    for e in range(n_e):
        sl = pl.ds(wins[e][0], ROUTE_WIN)
        xe_ref[e, sl, :] = xe_ref[e, sl, :] + rows[e * ROUTE_WIN:(e + 1) * ROUTE_WIN].astype(BF16)
        g_ref[e, sl, :] = g_ref[e, sl, :] + jnp.sum(jnp.where(hots[e], aff_ref[0, e:e + 1, :], 0.0),
                                                    axis=1, keepdims=True)

    @pl.when(sum(extra for _, extra in wins) > 0)
    def _():
        for e in range(n_e):
            lo_al, extra = wins[e]

            def spill(k, carry, e=e, lo_al=lo_al):
                start, first = _spill_window(k, lo_al, cap)
                hot = _one_hot_t(rank_ref[0, e:e + 1, :], start, first)
                sl2 = pl.ds(start, ROUTE_WIN)
                xe_ref[e, sl2, :] = xe_ref[e, sl2, :] + jnp.dot(hot.astype(BF16), x_ref[0],
                                                               preferred_element_type=F32).astype(BF16)
                g_ref[e, sl2, :] = g_ref[e, sl2, :] + jnp.sum(jnp.where(hot, aff_ref[0, e:e + 1, :], 0.0),
                                                              axis=1, keepdims=True)
                return carry

            lax.fori_loop(1, extra + 1, spill, 0)


def _dispatch_call(cnt_flat, xn2, rank_t, aff_t, cap):
    nb, n, d = xn2.shape
    n_e = rank_t.shape[1]
    n_chunk = n // ROUTE_CHUNK
    grid_spec = pltpu.PrefetchScalarGridSpec(
        num_scalar_prefetch=1,
        grid=(nb, n_chunk),
        in_specs=[
            pl.BlockSpec((1, ROUTE_CHUNK, d), lambda b, c, cnt: (b, c, 0)),
            pl.BlockSpec((1, n_e, ROUTE_CHUNK), lambda b, c, cnt: (b, 0, c)),
            pl.BlockSpec((1, n_e, ROUTE_CHUNK), lambda b, c, cnt: (b, 0, c)),
        ],
        out_specs=[
            pl.BlockSpec((n_e, cap, d), lambda b, c, cnt: (0, b, 0)),
            pl.BlockSpec((n_e, cap, 1), lambda b, c, cnt: (0, b, 0)),
        ],
    )
    return pl.pallas_call(
        functools.partial(_dispatch_kernel, cap=cap, n_chunk=n_chunk),
        grid_spec=grid_spec,
        out_shape=[jax.ShapeDtypeStruct((n_e, nb * cap, d), BF16), jax.ShapeDtypeStruct((n_e, nb * cap, 1), F32)],
        compiler_params=_cparams(("arbitrary", "arbitrary")),
        name="moe_dispatch",
    )(cnt_flat, xn2, rank_t, aff_t)


def _ffn_kernel(xe_ref, g_ref, wg_ref, wu_ref, wd_ref, o_ref, acc_s):
    f = pl.program_id(2)
    last = pl.num_programs(2) - 1
    d = o_ref.shape[2]
    xe = xe_ref[0]
    h1 = jnp.dot(xe, wg_ref[0].astype(BF16), preferred_element_type=F32)
    h2 = jnp.dot(xe, wu_ref[0].astype(BF16), preferred_element_type=F32)
    hid = (h1 * _sigmoid(h1) * h2).astype(BF16)
    wd = wd_ref[0].astype(BF16)

    def chunks():
        for c in range(d // FFN_TN):
            cols = slice(c * FFN_TN, (c + 1) * FFN_TN)
            yield cols, jnp.dot(hid, wd[:, cols], preferred_element_type=F32)

    @pl.when(f == 0)
    def _():
        for cols, part in chunks():
            acc_s[:, cols] = part

    @pl.when((f > 0) & (f < last))
    def _():
        for cols, part in chunks():
            acc_s[:, cols] += part

    @pl.when(f == last)
    def _():
        for cols, part in chunks():
            o_ref[0, :, cols] = ((acc_s[:, cols] + part) * g_ref[0]).astype(BF16)


def _ffn_call(xe, g, w_gate, w_up, w_down):
    e, m, d = xe.shape
    dff = w_gate.shape[2]
    tm = min(FFN_TM, m)
    assert dff // FFN_TF >= 2 and m % tm == 0 and d % FFN_TN == 0
    return pl.pallas_call(
        _ffn_kernel,
        grid=(e, m // tm, dff // FFN_TF),
        in_specs=[
            pl.BlockSpec((1, tm, d), lambda ei, mi, fi: (ei, mi, 0)),
            pl.BlockSpec((1, tm, 1), lambda ei, mi, fi: (ei, mi, 0)),
            pl.BlockSpec((1, d, FFN_TF), lambda ei, mi, fi: (ei, 0, fi)),
            pl.BlockSpec((1, d, FFN_TF), lambda ei, mi, fi: (ei, 0, fi)),
            pl.BlockSpec((1, FFN_TF, d), lambda ei, mi, fi: (ei, fi, 0)),
        ],
        out_specs=pl.BlockSpec((1, tm, d), lambda ei, mi, fi: (ei, mi, 0)),
        out_shape=jax.ShapeDtypeStruct((e, m, d), BF16),
        scratch_shapes=[pltpu.VMEM((tm, d), F32)],
        compiler_params=_cparams(("arbitrary", "arbitrary", "arbitrary")),
        name="expert_ffn",
    )(xe, g, w_gate, w_up, w_down)


def _combine_kernel(cnt_ref, ye_ref, rank_ref, x1_ref, g2_ref, fn_ref, o_ref, acc_s, *, cap, n_chunk):
    b = pl.program_id(0)
    c = pl.program_id(1)
    n_e = rank_ref.shape[1]
    tn_dims = (((0,), (0,)), ((), ()))
    wins = _slot_windows(cnt_ref, b, c, n_e, n_chunk, cap)
    hots = [_one_hot_t(rank_ref[0, e:e + 1, :], wins[e][0]).astype(BF16) for e in range(n_e)]
    ys = [ye_ref[e, pl.ds(wins[e][0], ROUTE_WIN), :] for e in range(n_e)]
    moe = lax.dot_general(jnp.concatenate(hots, axis=0), jnp.concatenate(ys, axis=0), tn_dims,
                          preferred_element_type=F32)

    def finish(m):
        x = x1_ref[0] + g2_ref[0] * m
        o_ref[0] = _rms(x) * fn_ref[0]

    any_spill = sum(extra for _, extra in wins) > 0

    @pl.when(jnp.logical_not(any_spill))
    def _():
        finish(moe)

    @pl.when(any_spill)
    def _():
        acc_s[...] = moe
        for e in range(n_e):
            lo_al, extra = wins[e]

            def spill(k, carry, e=e, lo_al=lo_al):
                start, first = _spill_window(k, lo_al, cap)
                hot = _one_hot_t(rank_ref[0, e:e + 1, :], start, first).astype(BF16)
                acc_s[...] += lax.dot_general(hot, ye_ref[e, pl.ds(start, ROUTE_WIN), :], tn_dims,
                                              preferred_element_type=F32)
                return carry

            lax.fori_loop(1, extra + 1, spill, 0)
        finish(acc_s[...])


def _combine_call(cnt_flat, ye, rank_t, x1, g2, fnorm, cap):
    nb, n, d = x1.shape
    n_e = rank_t.shape[1]
    n_chunk = n // ROUTE_CHUNK
    grid_spec = pltpu.PrefetchScalarGridSpec(
        num_scalar_prefetch=1,
        grid=(nb, n_chunk),
        in_specs=[
            pl.BlockSpec((n_e, cap, d), lambda b, c, cnt: (0, b, 0)),
            pl.BlockSpec((1, n_e, ROUTE_CHUNK), lambda b, c, cnt: (b, 0, c)),
            pl.BlockSpec((1, ROUTE_CHUNK, d), lambda b, c, cnt: (b, c, 0)),
            pl.BlockSpec((1, 1, d), lambda b, c, cnt: (b, 0, 0)),
            pl.BlockSpec((1, 1, d), lambda b, c, cnt: (0, 0, 0)),
        ],
        out_specs=pl.BlockSpec((1, ROUTE_CHUNK, d), lambda b, c, cnt: (b, c, 0)),
        scratch_shapes=[pltpu.VMEM((ROUTE_CHUNK, d), F32)],
    )
    return pl.pallas_call(
        functools.partial(_combine_kernel, cap=cap, n_chunk=n_chunk),
        grid_spec=grid_spec,
        out_shape=jax.ShapeDtypeStruct((nb, n, d), F32),
        compiler_params=_cparams(("arbitrary", "arbitrary")),
        name="moe_combine_final_norm",
    )(cnt_flat, ye, rank_t, x1, g2, fnorm.reshape(1, 1, d))


def kernel(x, c, ctx, c_ctx, w_mod, b_mod, w_in, b_in, conv_w, conv_b, lru_wa, lru_ba, lru_wi, lru_bi,
           lru_lambda, na_rpb, w_proj_rnn, w_proj_na, w_out, w_router, w_exp_gate, w_exp_up, w_exp_down,
           final_norm):
    nb, n, d = x.shape
    l = ctx.shape[1]
    assert w_mod.shape[0] == 1, "single-layer problem"
    assert nb == SUBLANES and n % LRU_TN == 0 and l % LRU_TN == 0 and d % LRU_SLAB == 0
    lyr = 0

    cc = jnp.concatenate([c, c_ctx[None], jnp.zeros((2 * SUBLANES - nb - 1, d), F32)], axis=0)
    mod = _mod_call(cc, w_mod[lyr], b_mod[lyr])
    sh1, sc1, g1, sh2, sc2, g2 = [m[:nb, None, :] for m in jnp.split(mod, 6, axis=-1)]
    csh1, csc1 = [jnp.broadcast_to(m[nb:nb + 1, None, :], (nb, 1, d)) for m in jnp.split(mod, 6, axis=-1)[:2]]

    w_in_b = w_in[lyr].astype(BF16)
    b_in_r = b_in[lyr].reshape(1, -1)
    tables = _rope_tables(n)
    gy, xr_t, q, k, v, sgr, sgn = _inproj_call(
        x, sh1, sc1, w_in_b, b_in_r, tables,
        ("gelu", "tok_major", "rope_q", "rope_k", "bf16", "sigmoid", "sigmoid"), tuple(range(7)), "in_proj_latent",
        col_blocked=(3, 4))
    xrc_t, kc, vc = _inproj_call(
        ctx, csh1, csc1, w_in_b, b_in_r, tables, ("tok_major", "bf16", "bf16"), (1, 3, 4), "in_proj_context")

    gw = [_lru_gate_weights(lru_wa[lyr, dr], lru_ba[lyr, dr], lru_wi[lyr, dr], lru_bi[lyr, dr], lru_lambda[lyr, dr])
          for dr in range(2)]
    zeros_h = jnp.zeros((nb, d), F32)
    _, hc_f = _lru_call(xrc_t, conv_w[lyr], conv_b[lyr], *gw[0], zeros_h, reverse=False, name="rglru_ctx_fwd")
    _, hc_b = _lru_call(xrc_t, conv_w[lyr], conv_b[lyr], *gw[1], zeros_h, reverse=True, name="rglru_ctx_bwd")
    hf_t, _ = _lru_call(xr_t, conv_w[lyr], conv_b[lyr], *gw[0], hc_f, reverse=False, name="rglru_fwd")
    y_rnn, _ = _lru_call(xr_t, conv_w[lyr], conv_b[lyr], *gw[1], hc_b, hf_t, gy, reverse=True, name="rglru_bwd")

    y_na = _na_call(q, k, v, kc, vc, _na_bias_table(na_rpb[lyr]))

    x1, xn2, lg_tiles = _mix_call(y_rnn, y_na, sgr, sgn, x, g1, sh2, sc2,
                                  w_proj_rnn[lyr].astype(BF16), w_proj_na[lyr].astype(BF16),
                                  w_out[lyr].astype(BF16), w_router[lyr].T.astype(BF16))
    n_e = w_router.shape[2]
    lg_t = lg_tiles.reshape(n // TQ, n_e, nb, TQ).transpose(2, 1, 0, 3).reshape(nb, n_e, n)

    cap = EC_CAPACITY * n // N_EXPERTS
    n_chunk = n // ROUTE_CHUNK
    assert cap % SLOT_ALIGN == 0 and cap >= ROUTE_WIN and n % ROUTE_CHUNK == 0
    rank_t, aff_t, cnt = _route_call(lg_t, cap)
    cnt_flat = cnt[:, :, :n_chunk + 1].reshape(-1)
    xe, ge = _dispatch_call(cnt_flat, xn2, rank_t, aff_t, cap)
    ye = _ffn_call(xe, ge, w_exp_gate[lyr], w_exp_up[lyr], w_exp_down[lyr])
    return _combine_call(cnt_flat, ye, rank_t, x1, g2, final_norm, cap)
```

```python
import functools

import numpy as np
import jax
import jax.numpy as jnp
from jax import lax
from jax.experimental import pallas as pl
from jax.experimental.pallas import tpu as pltpu

F32 = jnp.float32
BF16 = jnp.bfloat16

GRID_W = 64
N_HEADS = 16
HEAD_DIM = 64
N_LRU_BLOCKS = 16
CONV_W = 4
LRU_C = 8.0
WIN_R = 8
WIN_C = 16
ROPE_BASE = 10000.0
N_EXPERTS = 16
EC_CAPACITY = 2
EPS = 1e-6
NEG_INF = -1e30

LANES = 128
SUBLANES = 8
VMEM_LIMIT = 56 * 1024 * 1024

TQ = 128
LRU_SLAB = 256
LRU_TN = 128
FFN_TM = 2048
FFN_TF = 512
FFN_TN = 256
NA_CHUNK = 2
ROUTE_CHUNK = 256
ROUTE_WIN = 64
SLOT_ALIGN = 16
MIN_NORMAL_BITS = 0x00800000
LOG2_E = 1.4426950408889634


def _cparams(sem):
    return pltpu.CompilerParams(dimension_semantics=sem, vmem_limit_bytes=VMEM_LIMIT)


def _sigmoid(x):
    return 0.5 * jnp.tanh(0.5 * x) + 0.5


def _rms(x):
    return x * lax.rsqrt(jnp.mean(x * x, axis=-1, keepdims=True) + EPS)


def _mod_kernel(c_ref, w_ref, b_ref, o_ref):
    c = c_ref[...]
    s = c * jax.nn.sigmoid(c)
    o_ref[...] = jnp.dot(s, w_ref[...], preferred_element_type=F32) + b_ref[...]


def _mod_call(cc, w_mod, b_mod):
    rows, d = cc.shape
    n_out = w_mod.shape[1]
    tn = 1024
    return pl.pallas_call(
        _mod_kernel,
        grid=(n_out // tn,),
        in_specs=[
            pl.BlockSpec((rows, d), lambda j: (0, 0)),
            pl.BlockSpec((d, tn), lambda j: (0, j)),
            pl.BlockSpec((1, tn), lambda j: (0, j)),
        ],
        out_specs=pl.BlockSpec((rows, tn), lambda j: (0, j)),
        out_shape=jax.ShapeDtypeStruct((rows, n_out), F32),
        compiler_params=_cparams(("arbitrary",)),
        name="adaln_mod",
    )(cc, w_mod, b_mod.reshape(1, n_out))


def _rope(z, cos, sa, sb):
    outs = []
    for g in range(z.shape[1] // LANES):
        zg = z[:, g * LANES:(g + 1) * LANES]
        outs.append(zg * cos + pltpu.roll(zg, 16, 1) * sa + pltpu.roll(zg, LANES - 16, 1) * sb)
    return jnp.concatenate(outs, axis=1)


def _inproj_kernel(*refs, kinds, groups):
    n_out = len(kinds)
    x_ref, sh_ref, sc_ref, w_ref, b_ref, cos_ref, sa_ref, sb_ref = refs[:8]
    out_refs = refs[8:8 + n_out]
    nb, tq, d = x_ref.shape
    rows = nb * tq
    xn = _rms(x_ref[...]) * (1.0 + sc_ref[...]) + sh_ref[...]
    xb = xn.reshape(rows, d).astype(BF16)

    def tiled(t_ref):
        return jnp.broadcast_to(t_ref[...][None], (nb, tq, LANES)).reshape(rows, LANES)

    def store_tokens(o_ref, val):
        if o_ref.ndim == 3:
            o_ref[...] = val.reshape(nb, tq, d)
            return
        for p in range(d // LANES):
            vp = val[:, p * LANES:(p + 1) * LANES]
            if o_ref.ndim == 4:
                o_ref[:, p] = vp.reshape(nb, tq, LANES)
            else:
                n_cb = o_ref.shape[2]
                v5 = vp.reshape(nb, tq // GRID_W, n_cb, WIN_C, LANES)
                for gr in range(tq // GRID_W):
                    o_ref[:, p, :, gr * WIN_C:(gr + 1) * WIN_C, :] = v5[:, gr]

    for g, kind, o_ref in zip(groups, kinds, out_refs):
        z = jnp.dot(xb, w_ref[:, g * d:(g + 1) * d], preferred_element_type=F32) + b_ref[:, g * d:(g + 1) * d]
        if kind == "gelu":
            o_ref[...] = jax.nn.gelu(z, approximate=True).astype(BF16).reshape(nb, tq, d)
        elif kind == "tok_major":
            o_ref[...] = jnp.swapaxes(z.reshape(nb, tq, d), 0, 1).reshape(rows, d).astype(BF16)
        elif kind == "rope_q":
            r = _rope(z * (HEAD_DIM ** -0.5), tiled(cos_ref), tiled(sa_ref), tiled(sb_ref))
            store_tokens(o_ref, r.astype(BF16))
        elif kind == "rope_k":
            r = _rope(z, tiled(cos_ref), tiled(sa_ref), tiled(sb_ref))
            store_tokens(o_ref, r.astype(BF16))
        elif kind == "bf16":
            store_tokens(o_ref, z.astype(BF16))
        elif kind == "sigmoid":
            o_ref[...] = _sigmoid(z).astype(BF16).reshape(nb, tq, d)
        else:
            raise ValueError(kind)


def _inproj_call(x, sh, sc, w, b, tables, kinds, groups, name, pair_major=(), col_blocked=()):
    nb, n, d = x.shape
    assert len(groups) == len(kinds) and w.shape[0] == d and max(groups) * d < w.shape[1]
    cos, sa, sb = tables
    bat = lambda i: (0, i, 0)
    const3 = lambda i: (0, 0, 0)
    const2 = lambda i: (0, 0)
    out_shapes, out_specs = [], []
    for kind in kinds:
        if kind == "tok_major":
            out_shapes.append(jax.ShapeDtypeStruct((n * nb, d), BF16))
            out_specs.append(pl.BlockSpec((TQ * nb, d), lambda i: (i, 0)))
        elif len(out_shapes) in col_blocked:
            assert TQ % GRID_W == 0
            n_cb = GRID_W // WIN_C
            out_shapes.append(jax.ShapeDtypeStruct((nb, d // LANES, n_cb, n // n_cb, LANES), BF16))
            out_specs.append(pl.BlockSpec((nb, d // LANES, n_cb, TQ // n_cb, LANES), lambda i: (0, 0, 0, i, 0)))
        elif len(out_shapes) in pair_major:
            out_shapes.append(jax.ShapeDtypeStruct((nb, d // LANES, n, LANES), BF16))
            out_specs.append(pl.BlockSpec((nb, d // LANES, TQ, LANES), lambda i: (0, 0, i, 0)))
        else:
            out_shapes.append(jax.ShapeDtypeStruct((nb, n, d), BF16))
            out_specs.append(pl.BlockSpec((nb, TQ, d), bat))
    rows_per_table = cos.shape[0] // TQ
    return pl.pallas_call(
        functools.partial(_inproj_kernel, kinds=kinds, groups=groups),
        grid=(n // TQ,),
        in_specs=[
            pl.BlockSpec((nb, TQ, d), bat),
            pl.BlockSpec(sh.shape, const3),
            pl.BlockSpec(sc.shape, const3),
            pl.BlockSpec(w.shape, const2, pipeline_mode=pl.Buffered(1)),
            pl.BlockSpec(b.shape, const2),
            pl.BlockSpec((TQ, LANES), lambda i: (i % rows_per_table, 0)),
            pl.BlockSpec((TQ, LANES), lambda i: (i % rows_per_table, 0)),
            pl.BlockSpec((TQ, LANES), lambda i: (i % rows_per_table, 0)),
        ],
        out_specs=out_specs,
        out_shape=out_shapes,
        compiler_params=_cparams(("arbitrary",)),
        name=name,
    )(x, sh, sc, w, b, cos, sa, sb)


def _rope_tables(n):
    t = np.arange(n)
    row = (t // GRID_W).astype(np.float32)
    col = (t % GRID_W).astype(np.float32)
    n_freq = HEAD_DIM // 4
    inv = (np.float32(ROPE_BASE) ** (-np.arange(n_freq, dtype=np.float32) / np.float32(n_freq))).astype(np.float32)
    ang_r = (row[:, None] * inv).astype(np.float32)
    ang_c = (col[:, None] * inv).astype(np.float32)
    zero = np.zeros_like(ang_r)
    cos_h = np.concatenate([np.cos(ang_r), np.cos(ang_r), np.cos(ang_c), np.cos(ang_c)], axis=1)
    sa_h = np.concatenate([zero, np.sin(ang_r), zero, np.sin(ang_c)], axis=1)
    sb_h = np.concatenate([-np.sin(ang_r), zero, -np.sin(ang_c), zero], axis=1)
    rep = LANES // HEAD_DIM
    return tuple(jnp.asarray(np.tile(a.astype(np.float32), (1, rep))) for a in (cos_h, sa_h, sb_h))


def _lru_kernel(*refs, tn, nt, reverse, fuse_out):
    if fuse_out:
        (xp_ref, xc_ref, xn_ref, cw_ref, cb_ref, wg_ref, bg_ref, lam_ref, h0_ref, hf_ref, gy_ref,
         out_ref, hfin_ref, a_s, b_s, hcar_s) = refs
    else:
        (xp_ref, xc_ref, xn_ref, cw_ref, cb_ref, wg_ref, bg_ref, lam_ref, h0_ref,
         out_ref, hfin_ref, a_s, b_s, hcar_s) = refs
    i = pl.program_id(0)
    ti = (nt - 1 - i) if reverse else i
    nb, d = hcar_s.shape

    @pl.when(i == 0)
    def _():
        hcar_s[...] = h0_ref[...]

    def tokens(ref):
        return ref[...].astype(F32).reshape(ref.shape[0] // nb, nb, d)

    x = tokens(xc_ref)
    xp = jnp.where(ti == 0, 0.0, tokens(xp_ref))
    xnx = jnp.where(ti == nt - 1, 0.0, tokens(xn_ref)[:1])
    ext = jnp.concatenate([xp, x, xnx], axis=0)
    xc = cb_ref[...]
    for k in range(CONV_W):
        xc = xc + ext[k:k + tn] * cw_ref[k]
    rows = tn * nb
    xc2 = xc.reshape(rows, d)
    for s in range(d // LRU_SLAB):
        sl = slice(s * LRU_SLAB, (s + 1) * LRU_SLAB)
        xs = xc2[:, sl]
        half_gates = jnp.dot(xs.astype(BF16), wg_ref[s], preferred_element_type=F32) + bg_ref[s]
        t_r = jnp.tanh(half_gates[:, :LRU_SLAB])
        ig = 0.5 * jnp.tanh(half_gates[:, LRU_SLAB:]) + 0.5
        z = -lam_ref[s]
        softplus = jnp.maximum(z, 0.0) + jnp.log1p(jnp.exp(-jnp.abs(z)))
        k = (-0.5 * LRU_C * LOG2_E) * softplus
        a = jnp.exp2(t_r * k + k)
        b = jnp.sqrt(1.0 - a * a) * (ig * xs)
        a_s[:, :, sl] = a.reshape(tn, nb, LRU_SLAB)
        b_s[:, :, sl] = b.reshape(tn, nb, LRU_SLAB)

    unroll = 8

    def step(j, h):
        for k in range(unroll):
            jj = j * unroll + k
            t = (tn - 1 - jj) if reverse else jj
            h = a_s[t] * h + b_s[t]
            a_s[t] = h
        return h

    h = lax.fori_loop(0, tn // unroll, step, hcar_s[...])
    hcar_s[...] = h
    hfin_ref[...] = h
    if fuse_out:
        hs = jnp.swapaxes(tokens(hf_ref) + a_s[...], 0, 1)
        out_ref[...] = (hs * gy_ref[...].astype(F32)).astype(BF16)
    else:
        out_ref[...] = a_s[...].reshape(tn * nb, d).astype(BF16)


def _lru_call(xr_t, conv_w, conv_b, wg, bg, lam, h0, hf=None, gy=None, *, reverse, name):
    nb, d = h0.shape
    n = xr_t.shape[0] // nb
    tn = LRU_TN
    nt = n // tn
    halo = 2
    fuse_out = hf is not None
    tile = (lambda i: nt - 1 - i) if reverse else (lambda i: i)
    n_slab = d // LRU_SLAB
    in_specs = [
        pl.BlockSpec((halo * nb, d), lambda i: (jnp.maximum(tile(i) * (tn // halo) - 1, 0), 0)),
        pl.BlockSpec((tn * nb, d), lambda i: (tile(i), 0)),
        pl.BlockSpec((halo * nb, d), lambda i: (jnp.minimum((tile(i) + 1) * (tn // halo), n // halo - 1), 0)),
        pl.BlockSpec((CONV_W, 1, d), lambda i: (0, 0, 0)),
        pl.BlockSpec((1, d), lambda i: (0, 0)),
        pl.BlockSpec((n_slab, LRU_SLAB, 2 * LRU_SLAB), lambda i: (0, 0, 0)),
        pl.BlockSpec((n_slab, 1, 2 * LRU_SLAB), lambda i: (0, 0, 0)),
        pl.BlockSpec((n_slab, 1, LRU_SLAB), lambda i: (0, 0, 0)),
        pl.BlockSpec((nb, d), lambda i: (0, 0)),
    ]
    args = [xr_t, xr_t, xr_t, conv_w.reshape(CONV_W, 1, d), conv_b.reshape(1, d), wg, bg, lam, h0]
    if fuse_out:
        in_specs += [
            pl.BlockSpec((tn * nb, d), lambda i: (tile(i), 0)),
            pl.BlockSpec((nb, tn, d), lambda i: (0, tile(i), 0)),
        ]
        args += [hf, gy]
        out_shape0 = jax.ShapeDtypeStruct((nb, n, d), BF16)
        out_spec0 = pl.BlockSpec((nb, tn, d), lambda i: (0, tile(i), 0))
    else:
        out_shape0 = jax.ShapeDtypeStruct((n * nb, d), BF16)
        out_spec0 = pl.BlockSpec((tn * nb, d), lambda i: (tile(i), 0))
    return pl.pallas_call(
        functools.partial(_lru_kernel, tn=tn, nt=nt, reverse=reverse, fuse_out=fuse_out),
        grid=(nt,),
        in_specs=in_specs,
        out_specs=[out_spec0, pl.BlockSpec((nb, d), lambda i: (0, 0))],
        out_shape=[out_shape0, jax.ShapeDtypeStruct((nb, d), F32)],
        scratch_shapes=[
            pltpu.VMEM((tn, nb, d), F32),
            pltpu.VMEM((tn, nb, d), F32),
            pltpu.VMEM((nb, d), F32),
        ],
        compiler_params=_cparams(("arbitrary",)),
        name=name,
    )(*args)


def _lru_gate_weights(wa, ba, wi, bi, lam):
    n_blk, blk, _ = wa.shape
    d = n_blk * blk
    n_slab = d // LRU_SLAB
    per = LRU_SLAB // blk
    eye = jnp.eye(per, dtype=F32)

    def dense_slabs(w):
        return jnp.einsum("saij,ac->saicj", w.reshape(n_slab, per, blk, blk), eye).reshape(n_slab, LRU_SLAB, LRU_SLAB)

    wg = (0.5 * jnp.concatenate([dense_slabs(wa), dense_slabs(wi)], axis=2).astype(BF16)).astype(BF16)
    bg = 0.5 * jnp.concatenate([ba.reshape(n_slab, 1, LRU_SLAB), bi.reshape(n_slab, 1, LRU_SLAB)], axis=2)
    return wg, bg, lam.reshape(n_slab, 1, LRU_SLAB)


def _na_blocks(g):
    lo = min(max(8 * g - WIN_C // 2, 0) // WIN_C, GRID_W // WIN_C - 2)
    return (lo, lo + 1)


def _na_kernel(q_ref, k_ref, v_ref, kc_ref, vc_ref, tab_ref, o_ref, s_s, sc_s, p_s, pc_s, inv_s, *, rows):
    win = WIN_R * GRID_W
    ch = NA_CHUNK
    n_chunk = rows // ch
    qh = 2 * GRID_W
    n_cb = GRID_W // WIN_C
    blk = WIN_R * WIN_C
    grp = SUBLANES
    lane = lax.broadcasted_iota(jnp.int32, (GRID_W, LANES), 1)
    first_head = lane < HEAD_DIM
    nt_dims = (((1,), (1,)), ((), ()))

    def row_start(r):
        return pl.multiple_of(r * GRID_W, GRID_W)

    def window(ref, rs):
        run = pl.ds(pl.multiple_of(rs * WIN_C, WIN_C), blk)
        return jnp.concatenate([ref[0, 0, cb, run, :] for cb in range(n_cb)], axis=0)

    def groups():
        for h in range(2):
            for g in range(GRID_W // grp):
                yield h, g * grp, _na_blocks(g)

    p_s[...] = jnp.zeros_like(p_s)

    def stage1(c, slot):
        qs_all = []
        for u in range(ch):
            r = c * ch + u
            rs = jnp.clip(r - WIN_R // 2, 0, rows - WIN_R)
            q = q_ref[0, 0, pl.ds(row_start(r), GRID_W), :]
            zero = jnp.zeros_like(q)
            qs = jnp.concatenate([jnp.where(first_head, q, zero), jnp.where(first_head, zero, q)], axis=0)
            qs_all.append(qs)
            s = lax.dot_general(qs, window(k_ref, rs), nt_dims, preferred_element_type=F32)
            d0 = rs - r + (WIN_R - 1)
            for h, q0, cbs in groups():
                for cb in cbs:
                    rw = slice(h * GRID_W + q0, h * GRID_W + q0 + grp)
                    ln = slice(cb * blk, (cb + 1) * blk)
                    s_s[slot, u, rw, ln] = s[rw, ln] + tab_ref[0, h, d0, q0:q0 + grp, ln]
        sc_s[slot] = lax.dot_general(jnp.concatenate(qs_all, axis=0), kc_ref[0, 0], nt_dims,
                                     preferred_element_type=F32)

    def stage2(slot):
        for u in range(ch):
            for r0 in range(0, qh, 2 * grp):
                parts = {}
                pcs, invs = [], []
                for half in range(2):
                    ra = r0 + half * grp
                    cbs = _na_blocks((ra % GRID_W) // grp)
                    sl = [s_s[slot, u, ra:ra + grp, cb * blk:(cb + 1) * blk] for cb in cbs]
                    sc = sc_s[slot, u * qh + ra:u * qh + ra + grp, :]
                    m = jnp.maximum(jnp.max(jnp.maximum(sl[0], sl[1]), axis=1, keepdims=True),
                                    jnp.max(sc, axis=1, keepdims=True))
                    ps = [jnp.exp(x - m) for x in sl]
                    pc = jnp.exp(sc - m)
                    den = jnp.sum(ps[0] + ps[1], axis=1, keepdims=True) + jnp.sum(pc, axis=1, keepdims=True)
                    for cb, pb in zip(cbs, ps):
                        parts[(half, cb)] = pb
                    pcs.append(pc)
                    invs.append(jnp.broadcast_to(1.0 / den, (grp, LANES)))
                zero = jnp.zeros((grp, blk), F32)
                for cb in sorted({cb for _, cb in parts}):
                    both = jnp.concatenate([parts.get((0, cb), zero), parts.get((1, cb), zero)], axis=0)
                    p_s[slot, u, r0:r0 + 2 * grp, cb * blk:(cb + 1) * blk] = both.astype(BF16)
                pc_s[slot, u * qh + r0:u * qh + r0 + 2 * grp, :] = jnp.concatenate(pcs, axis=0).astype(BF16)
                inv_s[slot, u, r0:r0 + 2 * grp, :] = jnp.concatenate(invs, axis=0)

    def stage3(c, slot):
        oc = jnp.dot(pc_s[slot], vc_ref[0, 0], preferred_element_type=F32)
        for u in range(ch):
            r = c * ch + u
            rs = jnp.clip(r - WIN_R // 2, 0, rows - WIN_R)
            o = (jnp.dot(p_s[slot, u], window(v_ref, rs), preferred_element_type=F32)
                 + oc[u * qh:(u + 1) * qh]) * inv_s[slot, u]
            out = jnp.where(first_head, o[:GRID_W], o[GRID_W:])
            o_ref[0, 0, pl.ds(row_start(r), GRID_W), :] = out.astype(BF16)

    stage1(0, 0)
    stage2(0)
    stage1(1, 1)

    def body(j, carry):
        slot = j % 2
        stage3(j - 2, slot)
        stage2(1 - slot)
        stage1(j, slot)
        return carry

    lax.fori_loop(2, n_chunk, body, 0)
    last = n_chunk % 2
    stage3(n_chunk - 2, last)
    stage2(1 - last)
    stage3(n_chunk - 1, 1 - last)


def _na_call(q, k, v, kc, vc, tab):
    nb, n_pair, n, _ = q.shape
    l = kc.shape[2]
    rows = n // GRID_W
    n_cb = GRID_W // WIN_C
    assert k.shape == v.shape == (nb, n_pair, n_cb, n // n_cb, LANES)
    win = WIN_R * GRID_W
    qh = 2 * GRID_W
    assert rows % NA_CHUNK == 0 and rows // NA_CHUNK >= 2
    tok_spec = pl.BlockSpec((1, 1, n, LANES), lambda b, p: (b, p, 0, 0))
    kv_spec = pl.BlockSpec((1, 1, n_cb, n // n_cb, LANES), lambda b, p: (b, p, 0, 0, 0))
    ctx_spec = pl.BlockSpec((1, 1, l, LANES), lambda b, p: (b, p, 0, 0))
    return pl.pallas_call(
        functools.partial(_na_kernel, rows=rows),
        grid=(nb, n_pair),
        in_specs=[tok_spec, kv_spec, kv_spec, ctx_spec, ctx_spec,
                  pl.BlockSpec((1,) + tab.shape[1:], lambda b, p: (p, 0, 0, 0, 0))],
        out_specs=tok_spec,
        out_shape=jax.ShapeDtypeStruct((nb, n_pair, n, LANES), BF16),
        scratch_shapes=[
            pltpu.VMEM((2, NA_CHUNK, qh, win), F32),
            pltpu.VMEM((2, NA_CHUNK * qh, l), F32),
            pltpu.VMEM((2, NA_CHUNK, qh, win), BF16),
            pltpu.VMEM((2, NA_CHUNK * qh, l), BF16),
            pltpu.VMEM((2, NA_CHUNK, qh, LANES), F32),
        ],
        compiler_params=_cparams(("arbitrary", "arbitrary")),
        name="neighbourhood_attention",
    )(q, k, v, kc, vc, tab)


def _na_bias_table(rpb):
    qc = np.arange(GRID_W)
    kc = np.arange(GRID_W)
    n_cb = GRID_W // WIN_C
    cstart = np.clip(qc - WIN_C // 2, 0, GRID_W - WIN_C)
    ok = (kc[None, :] >= cstart[:, None]) & (kc[None, :] < cstart[:, None] + WIN_C)
    dc = np.clip(kc[None, :] - qc[:, None], -(WIN_C - 1), WIN_C - 1) + WIN_C - 1
    expand = (dc[None] == np.arange(2 * WIN_C - 1)[:, None, None]).astype(np.float32)
    expand = expand.reshape(2 * WIN_C - 1, GRID_W, n_cb, WIN_C)
    mask = np.where(ok, 0.0, NEG_INF).astype(np.float32).reshape(GRID_W, n_cb, 1, WIN_C)
    h = rpb.shape[0]
    rows = jnp.stack([rpb[:, d0:d0 + WIN_R].astype(F32) for d0 in range(WIN_R)], axis=1)
    tab = jnp.einsum("hdjx,xqbc->hdqbjc", rows, jnp.asarray(expand), precision=lax.Precision.HIGHEST)
    tab = tab + jnp.asarray(mask)
    return tab.reshape(h // 2, 2, WIN_R, GRID_W, WIN_R * GRID_W)


def _mix_kernel(yr_ref, yn_ref, gr_ref, gn_ref, x_ref, g1_ref, sh2_ref, sc2_ref,
                wpr_ref, wpn_ref, wo_ref, wr_ref, x1_ref, xn2_ref, lg_ref):
    nb, tq, d = x_ref.shape
    rows = nb * tq
    pr = jnp.dot(yr_ref[...].reshape(rows, d), wpr_ref[...], preferred_element_type=F32)
    yn = jnp.concatenate([yn_ref[:, p].reshape(rows, LANES) for p in range(d // LANES)], axis=1)
    pn = jnp.dot(yn, wpn_ref[...], preferred_element_type=F32)
    mix = gr_ref[...].reshape(rows, d).astype(F32) * pr + gn_ref[...].reshape(rows, d).astype(F32) * pn
    o = jnp.dot(mix.astype(BF16), wo_ref[...], preferred_element_type=F32)
    x1 = x_ref[...] + g1_ref[...] * o.reshape(nb, tq, d)
    x1_ref[...] = x1
    xb = (_rms(x1) * (1.0 + sc2_ref[...]) + sh2_ref[...]).astype(BF16)
    xn2_ref[...] = xb
    lg_ref[0] = lax.dot_general(wr_ref[...], xb.reshape(rows, d), (((1,), (1,)), ((), ())),
                                preferred_element_type=F32)


def _mix_call(yr, yn, gr, gn, x, g1, sh2, sc2, wpr, wpn, wo, wr):
    nb, n, d = x.shape
    bat = lambda i: (0, i, 0)
    const3 = lambda i: (0, 0, 0)
    const2 = lambda i: (0, 0)
    act = pl.BlockSpec((nb, TQ, d), bat)
    vec = pl.BlockSpec((nb, 1, d), const3)
    wsp = lambda w: pl.BlockSpec(w.shape, const2, pipeline_mode=pl.Buffered(1))
    return pl.pallas_call(
        _mix_kernel,
        grid=(n // TQ,),
        in_specs=[act, pl.BlockSpec((nb, d // LANES, TQ, LANES), lambda i: (0, 0, i, 0)), act, act, act,
                  vec, vec, vec, wsp(wpr), wsp(wpn), wsp(wo), wsp(wr)],
        out_specs=[act, act, pl.BlockSpec((1, wr.shape[0], nb * TQ), lambda i: (i, 0, 0))],
        out_shape=[
            jax.ShapeDtypeStruct((nb, n, d), F32),
            jax.ShapeDtypeStruct((nb, n, d), BF16),
            jax.ShapeDtypeStruct((n // TQ, wr.shape[0], nb * TQ), F32),
        ],
        compiler_params=_cparams(("arbitrary",)),
        name="merge_out_norm_router",
    )(yr, yn, gr, gn, x, g1, sh2, sc2, wpr, wpn, wo, wr)


def _route_kernel(lg_ref, rank_ref, aff_ref, cnt_ref, *, cap):
    lg = lg_ref[0]
    n_e, n = lg.shape
    ex = jnp.exp(lg - jnp.max(lg, axis=0, keepdims=True))
    aff = ex / jnp.sum(ex, axis=0, keepdims=True)
    aff_ref[0] = aff

    def as_float(bits):
        return lax.bitcast_convert_type(bits, F32)

    thr = jnp.zeros((n_e, 1), jnp.int32)
    for bit in range(30, -1, -1):
        cand = thr | (1 << bit)
        cnt = jnp.sum((aff >= as_float(cand)).astype(F32), axis=1, keepdims=True)
        thr = jnp.where(cnt >= cap, cand, thr)
    thr = jnp.where(thr < MIN_NORMAL_BITS, 0, thr)
    gt = aff >= as_float(jnp.where(thr == 0, MIN_NORMAL_BITS, thr + 1))
    eq = (aff >= as_float(thr)) & jnp.logical_not(gt)
    need = cap - jnp.sum(gt.astype(F32), axis=1, keepdims=True)

    blk = ROUTE_CHUNK
    row = lax.broadcasted_iota(jnp.int32, (blk, blk), 0)
    col = lax.broadcasted_iota(jnp.int32, (blk, blk), 1)
    upper = (row <= col).astype(BF16)

    def cumsum_tokens(mask):
        off = jnp.zeros((n_e, 1), F32)
        outs, offs = [], []
        for c in range(n // blk):
            x = mask[:, c * blk:(c + 1) * blk].astype(BF16)
            cs = jnp.dot(x, upper, preferred_element_type=F32) + off
            offs.append(off)
            outs.append(cs)
            off = cs[:, blk - 1:blk]
        offs.append(off)
        return jnp.concatenate(outs, axis=1), offs

    cum_eq, _ = cumsum_tokens(eq)
    sel = gt | (eq & ((cum_eq - eq.astype(F32)) < need))
    cum_sel, offs = cumsum_tokens(sel)
    rank_ref[0] = jnp.where(sel, cum_sel - 1.0, -1.0).astype(jnp.int32)
    pad = jnp.zeros((n_e, LANES - len(offs)), F32)
    cnt_ref[0] = jnp.concatenate(offs + [pad], axis=1).astype(jnp.int32)


def _route_call(lg_t, cap):
    nb, n_e, n = lg_t.shape
    blk3 = lambda b: (b, 0, 0)
    return pl.pallas_call(
        functools.partial(_route_kernel, cap=cap),
        grid=(nb,),
        in_specs=[pl.BlockSpec((1, n_e, n), blk3)],
        out_specs=[pl.BlockSpec((1, n_e, n), blk3), pl.BlockSpec((1, n_e, n), blk3),
                   pl.BlockSpec((1, n_e, LANES), blk3)],
        out_shape=[jax.ShapeDtypeStruct((nb, n_e, n), jnp.int32), jax.ShapeDtypeStruct((nb, n_e, n), F32),
                   jax.ShapeDtypeStruct((nb, n_e, LANES), jnp.int32)],
        compiler_params=_cparams(("arbitrary",)),
        name="route_select",
    )(lg_t)


def _slot_windows(cnt_ref, b, c, n_e, n_chunk, cap):
    wins = []
    for e in range(n_e):
        base = (b * n_e + e) * (n_chunk + 1) + c
        lo = cnt_ref[base]
        hi = cnt_ref[base + 1]
        lo_al = jnp.minimum((lo // SLOT_ALIGN) * SLOT_ALIGN, cap - ROUTE_WIN)
        extra = jnp.maximum((hi - lo_al + ROUTE_WIN - 1) // ROUTE_WIN - 1, 0)
        wins.append((pl.multiple_of(lo_al, SLOT_ALIGN), extra))
    return wins


def _one_hot_t(rank_row, start, first=None):
    slot = start + lax.broadcasted_iota(jnp.int32, (ROUTE_WIN, rank_row.shape[1]), 0)
    hit = rank_row == slot
    if first is not None:
        hit = hit & (slot >= first)
    return hit


def _spill_window(k, lo_al, cap):
    first = lo_al + k * ROUTE_WIN
    start = pl.multiple_of(jnp.minimum(first, cap - ROUTE_WIN), SLOT_ALIGN)
    return start, first


def _dispatch_kernel(cnt_ref, x_ref, rank_ref, aff_ref, xe_ref, g_ref, *, cap, n_chunk):
    b = pl.program_id(0)
    c = pl.program_id(1)
    n_e = rank_ref.shape[1]

    @pl.when(c == 0)
    def _():
        xe_ref[...] = jnp.zeros_like(xe_ref)
        g_ref[...] = jnp.zeros_like(g_ref)

    x = x_ref[0]
    wins = _slot_windows(cnt_ref, b, c, n_e, n_chunk, cap)
    hots = [_one_hot_t(rank_ref[0, e:e + 1, :], wins[e][0]) for e in range(n_e)]
    stacked = jnp.concatenate([h.astype(BF16) for h in hots], axis=0)
    rows = jnp.dot(stacked, x, preferred_element_type=F32)
    for e in range(n_e):
        sl = pl.ds(wins[e][0], ROUTE_WIN)
        xe_ref[e, sl, :] = xe_ref[e, sl, :] + rows[e * ROUTE_WIN:(e + 1) * ROUTE_WIN].astype(BF16)
        g_ref[e, sl, :] = g_ref[e, sl, :] + jnp.sum(jnp.where(hots[e], aff_ref[0, e:e + 1, :], 0.0),
                                                    axis=1, keepdims=True)

    @pl.when(sum(extra for _, extra in wins) > 0)
    def _():
        for e in range(n_e):
            lo_al, extra = wins[e]

            def spill(k, carry, e=e, lo_al=lo_al):
                start, first = _spill_window(k, lo_al, cap)
                hot = _one_hot_t(rank_ref[0, e:e + 1, :], start, first)
                sl2 = pl.ds(start, ROUTE_WIN)
                xe_ref[e, sl2, :] = xe_ref[e, sl2, :] + jnp.dot(hot.astype(BF16), x_ref[0],
                                                               preferred_element_type=F32).astype(BF16)
                g_ref[e, sl2, :] = g_ref[e, sl2, :] + jnp.sum(jnp.where(hot, aff_ref[0, e:e + 1, :], 0.0),
                                                              axis=1, keepdims=True)
                return carry

            lax.fori_loop(1, extra + 1, spill, 0)


def _dispatch_call(cnt_flat, xn2, rank_t, aff_t, cap):
    nb, n, d = xn2.shape
    n_e = rank_t.shape[1]
    n_chunk = n // ROUTE_CHUNK
    grid_spec = pltpu.PrefetchScalarGridSpec(
        num_scalar_prefetch=1,
        grid=(nb, n_chunk),
        in_specs=[
            pl.BlockSpec((1, ROUTE_CHUNK, d), lambda b, c, cnt: (b, c, 0)),
            pl.BlockSpec((1, n_e, ROUTE_CHUNK), lambda b, c, cnt: (b, 0, c)),
            pl.BlockSpec((1, n_e, ROUTE_CHUNK), lambda b, c, cnt: (b, 0, c)),
        ],
        out_specs=[
            pl.BlockSpec((n_e, cap, d), lambda b, c, cnt: (0, b, 0)),
            pl.BlockSpec((n_e, cap, 1), lambda b, c, cnt: (0, b, 0)),
        ],
    )
    return pl.pallas_call(
        functools.partial(_dispatch_kernel, cap=cap, n_chunk=n_chunk),
        grid_spec=grid_spec,
        out_shape=[jax.ShapeDtypeStruct((n_e, nb * cap, d), BF16), jax.ShapeDtypeStruct((n_e, nb * cap, 1), F32)],
        compiler_params=_cparams(("arbitrary", "arbitrary")),
        name="moe_dispatch",
    )(cnt_flat, xn2, rank_t, aff_t)


def _ffn_kernel(xe_ref, g_ref, wg_ref, wu_ref, wd_ref, o_ref, acc_s):
    f = pl.program_id(2)
    last = pl.num_programs(2) - 1
    d = o_ref.shape[2]
    xe = xe_ref[0]
    h1 = jnp.dot(xe, wg_ref[0].astype(BF16), preferred_element_type=F32)
    h2 = jnp.dot(xe, wu_ref[0].astype(BF16), preferred_element_type=F32)
    hid = (h1 * _sigmoid(h1) * h2).astype(BF16)
    wd = wd_ref[0].astype(BF16)

    def chunks():
        for c in range(d // FFN_TN):
            cols = slice(c * FFN_TN, (c + 1) * FFN_TN)
            yield cols, jnp.dot(hid, wd[:, cols], preferred_element_type=F32)

    @pl.when(f == 0)
    def _():
        for cols, part in chunks():
            acc_s[:, cols] = part

    @pl.when((f > 0) & (f < last))
    def _():
        for cols, part in chunks():
            acc_s[:, cols] += part

    @pl.when(f == last)
    def _():
        for cols, part in chunks():
            o_ref[0, :, cols] = ((acc_s[:, cols] + part) * g_ref[0]).astype(BF16)


def _ffn_call(xe, g, w_gate, w_up, w_down):
    e, m, d = xe.shape
    dff = w_gate.shape[2]
    tm = min(FFN_TM, m)
    assert dff // FFN_TF >= 2 and m % tm == 0 and d % FFN_TN == 0
    return pl.pallas_call(
        _ffn_kernel,
        grid=(e, m // tm, dff // FFN_TF),
        in_specs=[
            pl.BlockSpec((1, tm, d), lambda ei, mi, fi: (ei, mi, 0)),
            pl.BlockSpec((1, tm, 1), lambda ei, mi, fi: (ei, mi, 0)),
            pl.BlockSpec((1, d, FFN_TF), lambda ei, mi, fi: (ei, 0, fi)),
            pl.BlockSpec((1, d, FFN_TF), lambda ei, mi, fi: (ei, 0, fi)),
            pl.BlockSpec((1, FFN_TF, d), lambda ei, mi, fi: (ei, fi, 0)),
        ],
        out_specs=pl.BlockSpec((1, tm, d), lambda ei, mi, fi: (ei, mi, 0)),
        out_shape=jax.ShapeDtypeStruct((e, m, d), BF16),
        scratch_shapes=[pltpu.VMEM((tm, d), F32)],
        compiler_params=_cparams(("arbitrary", "arbitrary", "arbitrary")),
        name="expert_ffn",
    )(xe, g, w_gate, w_up, w_down)


def _combine_kernel(cnt_ref, ye_ref, rank_ref, x1_ref, g2_ref, fn_ref, o_ref, acc_s, *, cap, n_chunk):
    b = pl.program_id(0)
    c = pl.program_id(1)
    n_e = rank_ref.shape[1]
    tn_dims = (((0,), (0,)), ((), ()))
    wins = _slot_windows(cnt_ref, b, c, n_e, n_chunk, cap)
    hots = [_one_hot_t(rank_ref[0, e:e + 1, :], wins[e][0]).astype(BF16) for e in range(n_e)]
    ys = [ye_ref[e, pl.ds(wins[e][0], ROUTE_WIN), :] for e in range(n_e)]
    moe = lax.dot_general(jnp.concatenate(hots, axis=0), jnp.concatenate(ys, axis=0), tn_dims,
                          preferred_element_type=F32)

    def finish(m):
        x = x1_ref[0] + g2_ref[0] * m
        o_ref[0] = _rms(x) * fn_ref[0]

    any_spill = sum(extra for _, extra in wins) > 0

    @pl.when(jnp.logical_not(any_spill))
    def _():
        finish(moe)

    @pl.when(any_spill)
    def _():
        acc_s[...] = moe
        for e in range(n_e):
            lo_al, extra = wins[e]

            def spill(k, carry, e=e, lo_al=lo_al):
                start, first = _spill_window(k, lo_al, cap)
                hot = _one_hot_t(rank_ref[0, e:e + 1, :], start, first).astype(BF16)
                acc_s[...] += lax.dot_general(hot, ye_ref[e, pl.ds(start, ROUTE_WIN), :], tn_dims,
                                              preferred_element_type=F32)
                return carry

            lax.fori_loop(1, extra + 1, spill, 0)
        finish(acc_s[...])


def _combine_call(cnt_flat, ye, rank_t, x1, g2, fnorm, cap):
    nb, n, d = x1.shape
    n_e = rank_t.shape[1]
    n_chunk = n // ROUTE_CHUNK
    grid_spec = pltpu.PrefetchScalarGridSpec(
        num_scalar_prefetch=1,
        grid=(nb, n_chunk),
        in_specs=[
            pl.BlockSpec((n_e, cap, d), lambda b, c, cnt: (0, b, 0)),
            pl.BlockSpec((1, n_e, ROUTE_CHUNK), lambda b, c, cnt: (b, 0, c)),
            pl.BlockSpec((1, ROUTE_CHUNK, d), lambda b, c, cnt: (b, c, 0)),
            pl.BlockSpec((1, 1, d), lambda b, c, cnt: (b, 0, 0)),
            pl.BlockSpec((1, 1, d), lambda b, c, cnt: (0, 0, 0)),
        ],
        out_specs=pl.BlockSpec((1, ROUTE_CHUNK, d), lambda b, c, cnt: (b, c, 0)),
        scratch_shapes=[pltpu.VMEM((ROUTE_CHUNK, d), F32)],
    )
    return pl.pallas_call(
        functools.partial(_combine_kernel, cap=cap, n_chunk=n_chunk),
        grid_spec=grid_spec,
        out_shape=jax.ShapeDtypeStruct((nb, n, d), F32),
        compiler_params=_cparams(("arbitrary", "arbitrary")),
        name="moe_combine_final_norm",
    )(cnt_flat, ye, rank_t, x1, g2, fnorm.reshape(1, 1, d))


def kernel(x, c, ctx, c_ctx, w_mod, b_mod, w_in, b_in, conv_w, conv_b, lru_wa, lru_ba, lru_wi, lru_bi,
           lru_lambda, na_rpb, w_proj_rnn, w_proj_na, w_out, w_router, w_exp_gate, w_exp_up, w_exp_down,
           final_norm):
    nb, n, d = x.shape
    l = ctx.shape[1]
    assert w_mod.shape[0] == 1, "single-layer problem"
    assert nb == SUBLANES and n % LRU_TN == 0 and l % LRU_TN == 0 and d % LRU_SLAB == 0
    lyr = 0

    cc = jnp.concatenate([c, c_ctx[None], jnp.zeros((2 * SUBLANES - nb - 1, d), F32)], axis=0)
    mod = _mod_call(cc, w_mod[lyr], b_mod[lyr])
    sh1, sc1, g1, sh2, sc2, g2 = [m[:nb, None, :] for m in jnp.split(mod, 6, axis=-1)]
    csh1, csc1 = [jnp.broadcast_to(m[nb:nb + 1, None, :], (nb, 1, d)) for m in jnp.split(mod, 6, axis=-1)[:2]]

    w_in_b = w_in[lyr].astype(BF16)
    b_in_r = b_in[lyr].reshape(1, -1)
    tables = _rope_tables(n)
    gy, xr_t, q, k, v, sgr, sgn = _inproj_call(
        x, sh1, sc1, w_in_b, b_in_r, tables,
        ("gelu", "tok_major", "rope_q", "rope_k", "bf16", "sigmoid", "sigmoid"), tuple(range(7)), "in_proj_latent",
        pair_major=(2,), col_blocked=(3, 4))
    xrc_t, kc, vc = _inproj_call(
        ctx, csh1, csc1, w_in_b, b_in_r, tables, ("tok_major", "bf16", "bf16"), (1, 3, 4), "in_proj_context",
        pair_major=(1, 2))

    gw = [_lru_gate_weights(lru_wa[lyr, dr], lru_ba[lyr, dr], lru_wi[lyr, dr], lru_bi[lyr, dr], lru_lambda[lyr, dr])
          for dr in range(2)]
    zeros_h = jnp.zeros((nb, d), F32)
    _, hc_f = _lru_call(xrc_t, conv_w[lyr], conv_b[lyr], *gw[0], zeros_h, reverse=False, name="rglru_ctx_fwd")
    _, hc_b = _lru_call(xrc_t, conv_w[lyr], conv_b[lyr], *gw[1], zeros_h, reverse=True, name="rglru_ctx_bwd")
    hf_t, _ = _lru_call(xr_t, conv_w[lyr], conv_b[lyr], *gw[0], hc_f, reverse=False, name="rglru_fwd")
    y_rnn, _ = _lru_call(xr_t, conv_w[lyr], conv_b[lyr], *gw[1], hc_b, hf_t, gy, reverse=True, name="rglru_bwd")

    y_na = _na_call(q, k, v, kc, vc, _na_bias_table(na_rpb[lyr]))

    x1, xn2, lg_tiles = _mix_call(y_rnn, y_na, sgr, sgn, x, g1, sh2, sc2,
                                  w_proj_rnn[lyr].astype(BF16), w_proj_na[lyr].astype(BF16),
                                  w_out[lyr].astype(BF16), w_router[lyr].T.astype(BF16))
    n_e = w_router.shape[2]
    lg_t = lg_tiles.reshape(n // TQ, n_e, nb, TQ).transpose(2, 1, 0, 3).reshape(nb, n_e, n)

    cap = EC_CAPACITY * n // N_EXPERTS
    n_chunk = n // ROUTE_CHUNK
    assert cap % SLOT_ALIGN == 0 and cap >= ROUTE_WIN and n % ROUTE_CHUNK == 0
    rank_t, aff_t, cnt = _route_call(lg_t, cap)
    cnt_flat = cnt[:, :, :n_chunk + 1].reshape(-1)
    xe, ge = _dispatch_call(cnt_flat, xn2, rank_t, aff_t, cap)
    ye = _ffn_call(xe, ge, w_exp_gate[lyr], w_exp_up[lyr], w_exp_down[lyr])
    return _combine_call(cnt_flat, ye, rank_t, x1, g2, final_norm, cap)
```

```python
import functools

import numpy as np
import jax
import jax.numpy as jnp
from jax import lax
from jax.experimental import pallas as pl
from jax.experimental.pallas import tpu as pltpu

F32 = jnp.float32
BF16 = jnp.bfloat16

GRID_W = 64
N_HEADS = 16
HEAD_DIM = 64
N_LRU_BLOCKS = 16
CONV_W = 4
LRU_C = 8.0
WIN_R = 8
WIN_C = 16
ROPE_BASE = 10000.0
N_EXPERTS = 16
EC_CAPACITY = 2
EPS = 1e-6
NEG_INF = -1e30

LANES = 128
SUBLANES = 8
VMEM_LIMIT = 56 * 1024 * 1024

TQ = 128
LRU_SLAB = 256
LRU_TN = 128
FFN_TM = 2048
FFN_TF = 512
FFN_TN = 256
NA_CHUNK = 2
ROUTE_CHUNK = 256
ROUTE_WIN = 64
SLOT_ALIGN = 16
MIN_NORMAL_BITS = 0x00800000
LOG2_E = 1.4426950408889634


def _cparams(sem):
    return pltpu.CompilerParams(dimension_semantics=sem, vmem_limit_bytes=VMEM_LIMIT)


def _sigmoid(x):
    return 0.5 * jnp.tanh(0.5 * x) + 0.5


def _rms(x):
    return x * lax.rsqrt(jnp.mean(x * x, axis=-1, keepdims=True) + EPS)


def _mod_kernel(c_ref, w_ref, b_ref, o_ref):
    c = c_ref[...]
    s = c * jax.nn.sigmoid(c)
    o_ref[...] = jnp.dot(s, w_ref[...], preferred_element_type=F32) + b_ref[...]


def _mod_call(cc, w_mod, b_mod):
    rows, d = cc.shape
    n_out = w_mod.shape[1]
    tn = 1024
    return pl.pallas_call(
        _mod_kernel,
        grid=(n_out // tn,),
        in_specs=[
            pl.BlockSpec((rows, d), lambda j: (0, 0)),
            pl.BlockSpec((d, tn), lambda j: (0, j)),
            pl.BlockSpec((1, tn), lambda j: (0, j)),
        ],
        out_specs=pl.BlockSpec((rows, tn), lambda j: (0, j)),
        out_shape=jax.ShapeDtypeStruct((rows, n_out), F32),
        compiler_params=_cparams(("arbitrary",)),
        name="adaln_mod",
    )(cc, w_mod, b_mod.reshape(1, n_out))


def _rope(z, cos, sa, sb):
    outs = []
    for g in range(z.shape[1] // LANES):
        zg = z[:, g * LANES:(g + 1) * LANES]
        outs.append(zg * cos + pltpu.roll(zg, 16, 1) * sa + pltpu.roll(zg, LANES - 16, 1) * sb)
    return jnp.concatenate(outs, axis=1)


def _inproj_kernel(*refs, kinds, groups):
    n_out = len(kinds)
    x_ref, sh_ref, sc_ref, w_ref, b_ref, cos_ref, sa_ref, sb_ref = refs[:8]
    out_refs = refs[8:8 + n_out]
    nb, tq, d = x_ref.shape
    rows = nb * tq
    xn = _rms(x_ref[...]) * (1.0 + sc_ref[...]) + sh_ref[...]
    xb = xn.reshape(rows, d).astype(BF16)

    def tiled(t_ref):
        return jnp.broadcast_to(t_ref[...][None], (nb, tq, LANES)).reshape(rows, LANES)

    def store_tokens(o_ref, val):
        if o_ref.ndim == 3:
            o_ref[...] = val.reshape(nb, tq, d)
            return
        for p in range(d // LANES):
            vp = val[:, p * LANES:(p + 1) * LANES]
            if o_ref.ndim == 4:
                o_ref[:, p] = vp.reshape(nb, tq, LANES)
            else:
                n_cb = o_ref.shape[2]
                v5 = vp.reshape(nb, tq // GRID_W, n_cb, WIN_C, LANES)
                for gr in range(tq // GRID_W):
                    o_ref[:, p, :, gr * WIN_C:(gr + 1) * WIN_C, :] = v5[:, gr]

    for g, kind, o_ref in zip(groups, kinds, out_refs):
        z = jnp.dot(xb, w_ref[:, g * d:(g + 1) * d], preferred_element_type=F32) + b_ref[:, g * d:(g + 1) * d]
        if kind == "gelu":
            o_ref[...] = jax.nn.gelu(z, approximate=True).astype(BF16).reshape(nb, tq, d)
        elif kind == "tok_major":
            o_ref[...] = jnp.swapaxes(z.reshape(nb, tq, d), 0, 1).reshape(rows, d).astype(BF16)
        elif kind == "rope_q":
            r = _rope(z * (HEAD_DIM ** -0.5), tiled(cos_ref), tiled(sa_ref), tiled(sb_ref))
            store_tokens(o_ref, r.astype(BF16))
        elif kind == "rope_k":
            r = _rope(z, tiled(cos_ref), tiled(sa_ref), tiled(sb_ref))
            store_tokens(o_ref, r.astype(BF16))
        elif kind == "bf16":
            store_tokens(o_ref, z.astype(BF16))
        elif kind == "sigmoid":
            o_ref[...] = _sigmoid(z).astype(BF16).reshape(nb, tq, d)
        else:
            raise ValueError(kind)


def _inproj_call(x, sh, sc, w, b, tables, kinds, groups, name, pair_major=(), col_blocked=()):
    nb, n, d = x.shape
    assert len(groups) == len(kinds) and w.shape[0] == d and max(groups) * d < w.shape[1]
    cos, sa, sb = tables
    bat = lambda i: (0, i, 0)
    const3 = lambda i: (0, 0, 0)
    const2 = lambda i: (0, 0)
    out_shapes, out_specs = [], []
    for kind in kinds:
        if kind == "tok_major":
            out_shapes.append(jax.ShapeDtypeStruct((n * nb, d), BF16))
            out_specs.append(pl.BlockSpec((TQ * nb, d), lambda i: (i, 0)))
        elif len(out_shapes) in col_blocked:
            assert TQ % GRID_W == 0
            n_cb = GRID_W // WIN_C
            out_shapes.append(jax.ShapeDtypeStruct((nb, d // LANES, n_cb, n // n_cb, LANES), BF16))
            out_specs.append(pl.BlockSpec((nb, d // LANES, n_cb, TQ // n_cb, LANES), lambda i: (0, 0, 0, i, 0)))
        elif len(out_shapes) in pair_major:
            out_shapes.append(jax.ShapeDtypeStruct((nb, d // LANES, n, LANES), BF16))
            out_specs.append(pl.BlockSpec((nb, d // LANES, TQ, LANES), lambda i: (0, 0, i, 0)))
        else:
            out_shapes.append(jax.ShapeDtypeStruct((nb, n, d), BF16))
            out_specs.append(pl.BlockSpec((nb, TQ, d), bat))
    rows_per_table = cos.shape[0] // TQ
    return pl.pallas_call(
        functools.partial(_inproj_kernel, kinds=kinds, groups=groups),
        grid=(n // TQ,),
        in_specs=[
            pl.BlockSpec((nb, TQ, d), bat),
            pl.BlockSpec(sh.shape, const3),
            pl.BlockSpec(sc.shape, const3),
            pl.BlockSpec(w.shape, const2, pipeline_mode=pl.Buffered(1)),
            pl.BlockSpec(b.shape, const2),
            pl.BlockSpec((TQ, LANES), lambda i: (i % rows_per_table, 0)),
            pl.BlockSpec((TQ, LANES), lambda i: (i % rows_per_table, 0)),
            pl.BlockSpec((TQ, LANES), lambda i: (i % rows_per_table, 0)),
        ],
        out_specs=out_specs,
        out_shape=out_shapes,
        compiler_params=_cparams(("arbitrary",)),
        name=name,
    )(x, sh, sc, w, b, cos, sa, sb)


def _rope_tables(n):
    t = np.arange(n)
    row = (t // GRID_W).astype(np.float32)
    col = (t % GRID_W).astype(np.float32)
    n_freq = HEAD_DIM // 4
    inv = (np.float32(ROPE_BASE) ** (-np.arange(n_freq, dtype=np.float32) / np.float32(n_freq))).astype(np.float32)
    ang_r = (row[:, None] * inv).astype(np.float32)
    ang_c = (col[:, None] * inv).astype(np.float32)
    zero = np.zeros_like(ang_r)
    cos_h = np.concatenate([np.cos(ang_r), np.cos(ang_r), np.cos(ang_c), np.cos(ang_c)], axis=1)
    sa_h = np.concatenate([zero, np.sin(ang_r), zero, np.sin(ang_c)], axis=1)
    sb_h = np.concatenate([-np.sin(ang_r), zero, -np.sin(ang_c), zero], axis=1)
    rep = LANES // HEAD_DIM
    return tuple(jnp.asarray(np.tile(a.astype(np.float32), (1, rep))) for a in (cos_h, sa_h, sb_h))


def _lru_kernel(*refs, tn, nt, reverse, fuse_out):
    if fuse_out:
        (xc_ref, wg_ref, bg_ref, lam_ref, h0_ref, hf_ref, gy_ref, out_ref, hfin_ref, a_s, b_s, hcar_s) = refs
    else:
        (xp_ref, xin_ref, xn_ref, cw_ref, cb_ref, wg_ref, bg_ref, lam_ref, h0_ref,
         out_ref, xc_ref, hfin_ref, a_s, b_s, hcar_s) = refs
    i = pl.program_id(0)
    ti = (nt - 1 - i) if reverse else i
    nb, d = hcar_s.shape

    @pl.when(i == 0)
    def _():
        hcar_s[...] = h0_ref[...]

    def tokens(ref):
        return ref[...].astype(F32).reshape(ref.shape[0] // nb, nb, d)

    if fuse_out:
        xc = tokens(xc_ref)
    else:
        x = tokens(xin_ref)
        xp = jnp.where(ti == 0, 0.0, tokens(xp_ref))
        xnx = jnp.where(ti == nt - 1, 0.0, tokens(xn_ref)[:1])
        ext = jnp.concatenate([xp, x, xnx], axis=0)
        xc = cb_ref[...]
        for k in range(CONV_W):
            xc = xc + ext[k:k + tn] * cw_ref[k]
        xc_ref[...] = xc.reshape(tn * nb, d).astype(BF16)
    rows = tn * nb
    xc2 = xc.reshape(rows, d)
    for s in range(d // LRU_SLAB):
        sl = slice(s * LRU_SLAB, (s + 1) * LRU_SLAB)
        xs = xc2[:, sl]
        half_gates = jnp.dot(xs.astype(BF16), wg_ref[s], preferred_element_type=F32) + bg_ref[s]
        t_r = jnp.tanh(half_gates[:, :LRU_SLAB])
        ig = 0.5 * jnp.tanh(half_gates[:, LRU_SLAB:]) + 0.5
        z = -lam_ref[s]
        softplus = jnp.maximum(z, 0.0) + jnp.log1p(jnp.exp(-jnp.abs(z)))
        k = (-0.5 * LRU_C * LOG2_E) * softplus
        a = jnp.exp2(t_r * k + k)
        b = jnp.sqrt(1.0 - a * a) * (ig * xs)
        a_s[:, :, sl] = a.reshape(tn, nb, LRU_SLAB)
        b_s[:, :, sl] = b.reshape(tn, nb, LRU_SLAB)

    unroll = 8

    def step(j, h):
        for k in range(unroll):
            jj = j * unroll + k
            t = (tn - 1 - jj) if reverse else jj
            h = a_s[t] * h + b_s[t]
            a_s[t] = h
        return h

    h = lax.fori_loop(0, tn // unroll, step, hcar_s[...])
    hcar_s[...] = h
    hfin_ref[...] = h
    if fuse_out:
        hs = jnp.swapaxes(tokens(hf_ref) + a_s[...], 0, 1)
        out_ref[...] = (hs * gy_ref[...].astype(F32)).astype(BF16)
    else:
        out_ref[...] = a_s[...].reshape(tn * nb, d).astype(BF16)


def _lru_call(x_t, wg, bg, lam, h0, conv=None, hf=None, gy=None, *, reverse, name):
    nb, d = h0.shape
    n = x_t.shape[0] // nb
    tn = LRU_TN
    nt = n // tn
    halo = 2
    fuse_out = conv is None
    tile = (lambda i: nt - 1 - i) if reverse else (lambda i: i)
    n_slab = d // LRU_SLAB
    tok_spec = pl.BlockSpec((tn * nb, d), lambda i: (tile(i), 0))
    gate_specs = [
        pl.BlockSpec((n_slab, LRU_SLAB, 2 * LRU_SLAB), lambda i: (0, 0, 0)),
        pl.BlockSpec((n_slab, 1, 2 * LRU_SLAB), lambda i: (0, 0, 0)),
        pl.BlockSpec((n_slab, 1, LRU_SLAB), lambda i: (0, 0, 0)),
        pl.BlockSpec((nb, d), lambda i: (0, 0)),
    ]
    fin_spec = pl.BlockSpec((nb, d), lambda i: (0, 0))
    fin_shape = jax.ShapeDtypeStruct((nb, d), F32)
    tok_shape = jax.ShapeDtypeStruct((n * nb, d), BF16)
    if fuse_out:
        in_specs = [tok_spec] + gate_specs + [tok_spec, pl.BlockSpec((nb, tn, d), lambda i: (0, tile(i), 0))]
        args = [x_t, wg, bg, lam, h0, hf, gy]
        out_specs = [pl.BlockSpec((nb, tn, d), lambda i: (0, tile(i), 0)), fin_spec]
        out_shape = [jax.ShapeDtypeStruct((nb, n, d), BF16), fin_shape]
    else:
        conv_w, conv_b = conv
        in_specs = [
            pl.BlockSpec((halo * nb, d), lambda i: (jnp.maximum(tile(i) * (tn // halo) - 1, 0), 0)),
            tok_spec,
            pl.BlockSpec((halo * nb, d), lambda i: (jnp.minimum((tile(i) + 1) * (tn // halo), n // halo - 1), 0)),
            pl.BlockSpec((CONV_W, 1, d), lambda i: (0, 0, 0)),
            pl.BlockSpec((1, d), lambda i: (0, 0)),
        ] + gate_specs
        args = [x_t, x_t, x_t, conv_w.reshape(CONV_W, 1, d), conv_b.reshape(1, d), wg, bg, lam, h0]
        out_specs = [tok_spec, tok_spec, fin_spec]
        out_shape = [tok_shape, tok_shape, fin_shape]
    return pl.pallas_call(
        functools.partial(_lru_kernel, tn=tn, nt=nt, reverse=reverse, fuse_out=fuse_out),
        grid=(nt,),
        in_specs=in_specs,
        out_specs=out_specs,
        out_shape=out_shape,
        scratch_shapes=[
            pltpu.VMEM((tn, nb, d), F32),
            pltpu.VMEM((tn, nb, d), F32),
            pltpu.VMEM((nb, d), F32),
        ],
        compiler_params=_cparams(("arbitrary",)),
        name=name,
    )(*args)


def _lru_gate_weights(wa, ba, wi, bi, lam):
    n_blk, blk, _ = wa.shape
    d = n_blk * blk
    n_slab = d // LRU_SLAB
    per = LRU_SLAB // blk
    eye = jnp.eye(per, dtype=F32)

    def dense_slabs(w):
        return jnp.einsum("saij,ac->saicj", w.reshape(n_slab, per, blk, blk), eye).reshape(n_slab, LRU_SLAB, LRU_SLAB)

    wg = (0.5 * jnp.concatenate([dense_slabs(wa), dense_slabs(wi)], axis=2).astype(BF16)).astype(BF16)
    bg = 0.5 * jnp.concatenate([ba.reshape(n_slab, 1, LRU_SLAB), bi.reshape(n_slab, 1, LRU_SLAB)], axis=2)
    return wg, bg, lam.reshape(n_slab, 1, LRU_SLAB)


def _na_blocks(g):
    lo = min(max(8 * g - WIN_C // 2, 0) // WIN_C, GRID_W // WIN_C - 2)
    return (lo, lo + 1)


def _na_kernel(q_ref, k_ref, v_ref, kc_ref, vc_ref, tab_ref, o_ref, s_s, sc_s, p_s, pc_s, inv_s, *, rows):
    win = WIN_R * GRID_W
    ch = NA_CHUNK
    n_chunk = rows // ch
    qh = 2 * GRID_W
    n_cb = GRID_W // WIN_C
    blk = WIN_R * WIN_C
    grp = SUBLANES
    lane = lax.broadcasted_iota(jnp.int32, (GRID_W, LANES), 1)
    first_head = lane < HEAD_DIM
    nt_dims = (((1,), (1,)), ((), ()))

    def row_start(r):
        return pl.multiple_of(r * GRID_W, GRID_W)

    def window(ref, rs):
        run = pl.ds(pl.multiple_of(rs * WIN_C, WIN_C), blk)
        return jnp.concatenate([ref[0, 0, cb, run, :] for cb in range(n_cb)], axis=0)

    def groups():
        for h in range(2):
            for g in range(GRID_W // grp):
                yield h, g * grp, _na_blocks(g)

    p_s[...] = jnp.zeros_like(p_s)

    def stage1(c, slot):
        qs_all = []
        for u in range(ch):
            r = c * ch + u
            rs = jnp.clip(r - WIN_R // 2, 0, rows - WIN_R)
            q = q_ref[0, 0, pl.ds(row_start(r), GRID_W), :]
            zero = jnp.zeros_like(q)
            qs = jnp.concatenate([jnp.where(first_head, q, zero), jnp.where(first_head, zero, q)], axis=0)
            qs_all.append(qs)
            s = lax.dot_general(qs, window(k_ref, rs), nt_dims, preferred_element_type=F32)
            d0 = rs - r + (WIN_R - 1)
            for h, q0, cbs in groups():
                for cb in cbs:
                    rw = slice(h * GRID_W + q0, h * GRID_W + q0 + grp)
                    ln = slice(cb * blk, (cb + 1) * blk)
                    s_s[slot, u, rw, ln] = s[rw, ln] + tab_ref[0, h, d0, q0:q0 + grp, ln]
        sc_s[slot] = lax.dot_general(jnp.concatenate(qs_all, axis=0), kc_ref[0, 0], nt_dims,
                                     preferred_element_type=F32)

    def stage2(slot):
        for u in range(ch):
            for r0 in range(0, qh, 2 * grp):
                parts = {}
                pcs, invs = [], []
                for half in range(2):
                    ra = r0 + half * grp
                    cbs = _na_blocks((ra % GRID_W) // grp)
                    sl = [s_s[slot, u, ra:ra + grp, cb * blk:(cb + 1) * blk] for cb in cbs]
                    sc = sc_s[slot, u * qh + ra:u * qh + ra + grp, :]
                    m = jnp.maximum(jnp.max(jnp.maximum(sl[0], sl[1]), axis=1, keepdims=True),
                                    jnp.max(sc, axis=1, keepdims=True))
                    ps = [jnp.exp(x - m) for x in sl]
                    pc = jnp.exp(sc - m)
                    den = jnp.sum(ps[0] + ps[1], axis=1, keepdims=True) + jnp.sum(pc, axis=1, keepdims=True)
                    for cb, pb in zip(cbs, ps):
                        parts[(half, cb)] = pb
                    pcs.append(pc)
                    invs.append(jnp.broadcast_to(1.0 / den, (grp, LANES)))
                zero = jnp.zeros((grp, blk), F32)
                for cb in sorted({cb for _, cb in parts}):
                    both = jnp.concatenate([parts.get((0, cb), zero), parts.get((1, cb), zero)], axis=0)
                    p_s[slot, u, r0:r0 + 2 * grp, cb * blk:(cb + 1) * blk] = both.astype(BF16)
                pc_s[slot, u * qh + r0:u * qh + r0 + 2 * grp, :] = jnp.concatenate(pcs, axis=0).astype(BF16)
                inv_s[slot, u, r0:r0 + 2 * grp, :] = jnp.concatenate(invs, axis=0)

    def stage3(c, slot):
        oc = jnp.dot(pc_s[slot], vc_ref[0, 0], preferred_element_type=F32)
        for u in range(ch):
            r = c * ch + u
            rs = jnp.clip(r - WIN_R // 2, 0, rows - WIN_R)
            o = (jnp.dot(p_s[slot, u], window(v_ref, rs), preferred_element_type=F32)
                 + oc[u * qh:(u + 1) * qh]) * inv_s[slot, u]
            out = jnp.where(first_head, o[:GRID_W], o[GRID_W:])
            o_ref[0, 0, pl.ds(row_start(r), GRID_W), :] = out.astype(BF16)

    stage1(0, 0)
    stage2(0)
    stage1(1, 1)

    def body(j, carry):
        slot = j % 2
        stage3(j - 2, slot)
        stage2(1 - slot)
        stage1(j, slot)
        return carry

    lax.fori_loop(2, n_chunk, body, 0)
    last = n_chunk % 2
    stage3(n_chunk - 2, last)
    stage2(1 - last)
    stage3(n_chunk - 1, 1 - last)


def _na_call(q, k, v, kc, vc, tab):
    nb, n_pair, n, _ = q.shape
    l = kc.shape[2]
    rows = n // GRID_W
    n_cb = GRID_W // WIN_C
    assert k.shape == v.shape == (nb, n_pair, n_cb, n // n_cb, LANES)
    win = WIN_R * GRID_W
    qh = 2 * GRID_W
    assert rows % NA_CHUNK == 0 and rows // NA_CHUNK >= 2
    tok_spec = pl.BlockSpec((1, 1, n, LANES), lambda b, p: (b, p, 0, 0))
    kv_spec = pl.BlockSpec((1, 1, n_cb, n // n_cb, LANES), lambda b, p: (b, p, 0, 0, 0))
    ctx_spec = pl.BlockSpec((1, 1, l, LANES), lambda b, p: (b, p, 0, 0))
    return pl.pallas_call(
        functools.partial(_na_kernel, rows=rows),
        grid=(nb, n_pair),
        in_specs=[tok_spec, kv_spec, kv_spec, ctx_spec, ctx_spec,
                  pl.BlockSpec((1,) + tab.shape[1:], lambda b, p: (p, 0, 0, 0, 0))],
        out_specs=tok_spec,
        out_shape=jax.ShapeDtypeStruct((nb, n_pair, n, LANES), BF16),
        scratch_shapes=[
            pltpu.VMEM((2, NA_CHUNK, qh, win), F32),
            pltpu.VMEM((2, NA_CHUNK * qh, l), F32),
            pltpu.VMEM((2, NA_CHUNK, qh, win), BF16),
            pltpu.VMEM((2, NA_CHUNK * qh, l), BF16),
            pltpu.VMEM((2, NA_CHUNK, qh, LANES), F32),
        ],
        compiler_params=_cparams(("arbitrary", "arbitrary")),
        name="neighbourhood_attention",
    )(q, k, v, kc, vc, tab)


def _na_bias_table(rpb):
    qc = np.arange(GRID_W)
    kc = np.arange(GRID_W)
    n_cb = GRID_W // WIN_C
    cstart = np.clip(qc - WIN_C // 2, 0, GRID_W - WIN_C)
    ok = (kc[None, :] >= cstart[:, None]) & (kc[None, :] < cstart[:, None] + WIN_C)
    dc = np.clip(kc[None, :] - qc[:, None], -(WIN_C - 1), WIN_C - 1) + WIN_C - 1
    expand = (dc[None] == np.arange(2 * WIN_C - 1)[:, None, None]).astype(np.float32)
    expand = expand.reshape(2 * WIN_C - 1, GRID_W, n_cb, WIN_C)
    mask = np.where(ok, 0.0, NEG_INF).astype(np.float32).reshape(GRID_W, n_cb, 1, WIN_C)
    h = rpb.shape[0]
    rows = jnp.stack([rpb[:, d0:d0 + WIN_R].astype(F32) for d0 in range(WIN_R)], axis=1)
    tab = jnp.einsum("hdjx,xqbc->hdqbjc", rows, jnp.asarray(expand), precision=lax.Precision.HIGHEST)
    tab = tab + jnp.asarray(mask)
    return tab.reshape(h // 2, 2, WIN_R, GRID_W, WIN_R * GRID_W)


def _mix_kernel(yr_ref, yn_ref, gr_ref, gn_ref, x_ref, g1_ref, sh2_ref, sc2_ref,
                wpr_ref, wpn_ref, wo_ref, wr_ref, x1_ref, xn2_ref, lg_ref):
    nb, tq, d = x_ref.shape
    rows = nb * tq
    pr = jnp.dot(yr_ref[...].reshape(rows, d), wpr_ref[...], preferred_element_type=F32)
    yn = jnp.concatenate([yn_ref[:, p].reshape(rows, LANES) for p in range(d // LANES)], axis=1)
    pn = jnp.dot(yn, wpn_ref[...], preferred_element_type=F32)
    mix = gr_ref[...].reshape(rows, d).astype(F32) * pr + gn_ref[...].reshape(rows, d).astype(F32) * pn
    o = jnp.dot(mix.astype(BF16), wo_ref[...], preferred_element_type=F32)
    x1 = x_ref[...] + g1_ref[...] * o.reshape(nb, tq, d)
    x1_ref[...] = x1
    xb = (_rms(x1) * (1.0 + sc2_ref[...]) + sh2_ref[...]).astype(BF16)
    xn2_ref[...] = xb
    lg_ref[0] = lax.dot_general(wr_ref[...], xb.reshape(rows, d), (((1,), (1,)), ((), ())),
                                preferred_element_type=F32)


def _mix_call(yr, yn, gr, gn, x, g1, sh2, sc2, wpr, wpn, wo, wr):
    nb, n, d = x.shape
    bat = lambda i: (0, i, 0)
    const3 = lambda i: (0, 0, 0)
    const2 = lambda i: (0, 0)
    act = pl.BlockSpec((nb, TQ, d), bat)
    vec = pl.BlockSpec((nb, 1, d), const3)
    wsp = lambda w: pl.BlockSpec(w.shape, const2, pipeline_mode=pl.Buffered(1))
    return pl.pallas_call(
        _mix_kernel,
        grid=(n // TQ,),
        in_specs=[act, pl.BlockSpec((nb, d // LANES, TQ, LANES), lambda i: (0, 0, i, 0)), act, act, act,
                  vec, vec, vec, wsp(wpr), wsp(wpn), wsp(wo), wsp(wr)],
        out_specs=[act, act, pl.BlockSpec((1, wr.shape[0], nb * TQ), lambda i: (i, 0, 0))],
        out_shape=[
            jax.ShapeDtypeStruct((nb, n, d), F32),
            jax.ShapeDtypeStruct((nb, n, d), BF16),
            jax.ShapeDtypeStruct((n // TQ, wr.shape[0], nb * TQ), F32),
        ],
        compiler_params=_cparams(("arbitrary",)),
        name="merge_out_norm_router",
    )(yr, yn, gr, gn, x, g1, sh2, sc2, wpr, wpn, wo, wr)


def _route_kernel(lg_ref, rank_ref, aff_ref, cnt_ref, *, cap):
    lg = lg_ref[0]
    n_e, n = lg.shape
    ex = jnp.exp(lg - jnp.max(lg, axis=0, keepdims=True))
    aff = ex / jnp.sum(ex, axis=0, keepdims=True)
    aff_ref[0] = aff

    def as_float(bits):
        return lax.bitcast_convert_type(bits, F32)

    thr = jnp.zeros((n_e, 1), jnp.int32)
    for bit in range(30, -1, -1):
        cand = thr | (1 << bit)
        cnt = jnp.sum((aff >= as_float(cand)).astype(F32), axis=1, keepdims=True)
        thr = jnp.where(cnt >= cap, cand, thr)
    thr = jnp.where(thr < MIN_NORMAL_BITS, 0, thr)
    gt = aff >= as_float(jnp.where(thr == 0, MIN_NORMAL_BITS, thr + 1))
    eq = (aff >= as_float(thr)) & jnp.logical_not(gt)
    need = cap - jnp.sum(gt.astype(F32), axis=1, keepdims=True)

    blk = ROUTE_CHUNK
    row = lax.broadcasted_iota(jnp.int32, (blk, blk), 0)
    col = lax.broadcasted_iota(jnp.int32, (blk, blk), 1)
    upper = (row <= col).astype(BF16)

    def cumsum_tokens(mask):
        off = jnp.zeros((n_e, 1), F32)
        outs, offs = [], []
        for c in range(n // blk):
            x = mask[:, c * blk:(c + 1) * blk].astype(BF16)
            cs = jnp.dot(x, upper, preferred_element_type=F32) + off
            offs.append(off)
            outs.append(cs)
            off = cs[:, blk - 1:blk]
        offs.append(off)
        return jnp.concatenate(outs, axis=1), offs

    cum_eq, _ = cumsum_tokens(eq)
    sel = gt | (eq & ((cum_eq - eq.astype(F32)) < need))
    cum_sel, offs = cumsum_tokens(sel)
    rank_ref[0] = jnp.where(sel, cum_sel - 1.0, -1.0).astype(jnp.int32)
    pad = jnp.zeros((n_e, LANES - len(offs)), F32)
    cnt_ref[0] = jnp.concatenate(offs + [pad], axis=1).astype(jnp.int32)


def _route_call(lg_t, cap):
    nb, n_e, n = lg_t.shape
    blk3 = lambda b: (b, 0, 0)
    return pl.pallas_call(
        functools.partial(_route_kernel, cap=cap),
        grid=(nb,),
        in_specs=[pl.BlockSpec((1, n_e, n), blk3)],
        out_specs=[pl.BlockSpec((1, n_e, n), blk3), pl.BlockSpec((1, n_e, n), blk3),
                   pl.BlockSpec((1, n_e, LANES), blk3)],
        out_shape=[jax.ShapeDtypeStruct((nb, n_e, n), jnp.int32), jax.ShapeDtypeStruct((nb, n_e, n), F32),
                   jax.ShapeDtypeStruct((nb, n_e, LANES), jnp.int32)],
        compiler_params=_cparams(("arbitrary",)),
        name="route_select",
    )(lg_t)


def _slot_windows(cnt_ref, b, c, n_e, n_chunk, cap):
    wins = []
    for e in range(n_e):
        base = (b * n_e + e) * (n_chunk + 1) + c
        lo = cnt_ref[base]
        hi = cnt_ref[base + 1]
        lo_al = jnp.minimum((lo // SLOT_ALIGN) * SLOT_ALIGN, cap - ROUTE_WIN)
        extra = jnp.maximum((hi - lo_al + ROUTE_WIN - 1) // ROUTE_WIN - 1, 0)
        wins.append((pl.multiple_of(lo_al, SLOT_ALIGN), extra))
    return wins


def _one_hot_t(rank_row, start, first=None):
    slot = start + lax.broadcasted_iota(jnp.int32, (ROUTE_WIN, rank_row.shape[1]), 0)
    hit = rank_row == slot
    if first is not None:
        hit = hit & (slot >= first)
    return hit


def _spill_window(k, lo_al, cap):
    first = lo_al + k * ROUTE_WIN
    start = pl.multiple_of(jnp.minimum(first, cap - ROUTE_WIN), SLOT_ALIGN)
    return start, first


def _dispatch_kernel(cnt_ref, x_ref, rank_ref, aff_ref, xe_ref, g_ref, *, cap, n_chunk):
    b = pl.program_id(0)
    c = pl.program_id(1)
    n_e = rank_ref.shape[1]

    @pl.when(c == 0)
    def _():
        xe_ref[...] = jnp.zeros_like(xe_ref)
        g_ref[...] = jnp.zeros_like(g_ref)

    x = x_ref[0]
    wins = _slot_windows(cnt_ref, b, c, n_e, n_chunk, cap)
    hots = [_one_hot_t(rank_ref[0, e:e + 1, :], wins[e][0]) for e in range(n_e)]
    stacked = jnp.concatenate([h.astype(BF16) for h in hots], axis=0)
    rows = jnp.dot(stacked, x, preferred_element_type=F32)
    for e in range(n_e):
        sl = pl.ds(wins[e][0], ROUTE_WIN)
        xe_ref[e, sl, :] = xe_ref[e, sl, :] + rows[e * ROUTE_WIN:(e + 1) * ROUTE_WIN].astype(BF16)
        g_ref[e, sl, :] = g_ref[e, sl, :] + jnp.sum(jnp.where(hots[e], aff_ref[0, e:e + 1, :], 0.0),
                                                    axis=1, keepdims=True)

    @pl.when(sum(extra for _, extra in wins) > 0)
    def _():
        for e in range(n_e):
            lo_al, extra = wins[e]

            def spill(k, carry, e=e, lo_al=lo_al):
                start, first = _spill_window(k, lo_al, cap)
                hot = _one_hot_t(rank_ref[0, e:e + 1, :], start, first)
                sl2 = pl.ds(start, ROUTE_WIN)
                xe_ref[e, sl2, :] = xe_ref[e, sl2, :] + jnp.dot(hot.astype(BF16), x_ref[0],
                                                               preferred_element_type=F32).astype(BF16)
                g_ref[e, sl2, :] = g_ref[e, sl2, :] + jnp.sum(jnp.where(hot, aff_ref[0, e:e + 1, :], 0.0),
                                                              axis=1, keepdims=True)
                return carry

            lax.fori_loop(1, extra + 1, spill, 0)


def _dispatch_call(cnt_flat, xn2, rank_t, aff_t, cap):
    nb, n, d = xn2.shape
    n_e = rank_t.shape[1]
    n_chunk = n // ROUTE_CHUNK
    grid_spec = pltpu.PrefetchScalarGridSpec(
        num_scalar_prefetch=1,
        grid=(nb, n_chunk),
        in_specs=[
            pl.BlockSpec((1, ROUTE_CHUNK, d), lambda b, c, cnt: (b, c, 0)),
            pl.BlockSpec((1, n_e, ROUTE_CHUNK), lambda b, c, cnt: (b, 0, c)),
            pl.BlockSpec((1, n_e, ROUTE_CHUNK), lambda b, c, cnt: (b, 0, c)),
        ],
        out_specs=[
            pl.BlockSpec((n_e, cap, d), lambda b, c, cnt: (0, b, 0)),
            pl.BlockSpec((n_e, cap, 1), lambda b, c, cnt: (0, b, 0)),
        ],
    )
    return pl.pallas_call(
        functools.partial(_dispatch_kernel, cap=cap, n_chunk=n_chunk),
        grid_spec=grid_spec,
        out_shape=[jax.ShapeDtypeStruct((n_e, nb * cap, d), BF16), jax.ShapeDtypeStruct((n_e, nb * cap, 1), F32)],
        compiler_params=_cparams(("arbitrary", "arbitrary")),
        name="moe_dispatch",
    )(cnt_flat, xn2, rank_t, aff_t)


def _ffn_kernel(xe_ref, g_ref, wg_ref, wu_ref, wd_ref, o_ref, acc_s):
    f = pl.program_id(2)
    last = pl.num_programs(2) - 1
    d = o_ref.shape[2]
    xe = xe_ref[0]
    h1 = jnp.dot(xe, wg_ref[0].astype(BF16), preferred_element_type=F32)
    h2 = jnp.dot(xe, wu_ref[0].astype(BF16), preferred_element_type=F32)
    hid = (h1 * _sigmoid(h1) * h2).astype(BF16)
    wd = wd_ref[0].astype(BF16)

    def chunks():
        for c in range(d // FFN_TN):
            cols = slice(c * FFN_TN, (c + 1) * FFN_TN)
            yield cols, jnp.dot(hid, wd[:, cols], preferred_element_type=F32)

    @pl.when(f == 0)
    def _():
        for cols, part in chunks():
            acc_s[:, cols] = part

    @pl.when((f > 0) & (f < last))
    def _():
        for cols, part in chunks():
            acc_s[:, cols] += part

    @pl.when(f == last)
    def _():
        for cols, part in chunks():
            o_ref[0, :, cols] = ((acc_s[:, cols] + part) * g_ref[0]).astype(BF16)


def _ffn_call(xe, g, w_gate, w_up, w_down):
    e, m, d = xe.shape
    dff = w_gate.shape[2]
    tm = min(FFN_TM, m)
    assert dff // FFN_TF >= 2 and m % tm == 0 and d % FFN_TN == 0
    return pl.pallas_call(
        _ffn_kernel,
        grid=(e, m // tm, dff // FFN_TF),
        in_specs=[
            pl.BlockSpec((1, tm, d), lambda ei, mi, fi: (ei, mi, 0)),
            pl.BlockSpec((1, tm, 1), lambda ei, mi, fi: (ei, mi, 0)),
            pl.BlockSpec((1, d, FFN_TF), lambda ei, mi, fi: (ei, 0, fi)),
            pl.BlockSpec((1, d, FFN_TF), lambda ei, mi, fi: (ei, 0, fi)),
            pl.BlockSpec((1, FFN_TF, d), lambda ei, mi, fi: (ei, fi, 0)),
        ],
        out_specs=pl.BlockSpec((1, tm, d), lambda ei, mi, fi: (ei, mi, 0)),
        out_shape=jax.ShapeDtypeStruct((e, m, d), BF16),
        scratch_shapes=[pltpu.VMEM((tm, d), F32)],
        compiler_params=_cparams(("arbitrary", "arbitrary", "arbitrary")),
        name="expert_ffn",
    )(xe, g, w_gate, w_up, w_down)


def _combine_kernel(cnt_ref, ye_ref, rank_ref, x1_ref, g2_ref, fn_ref, o_ref, acc_s, *, cap, n_chunk):
    b = pl.program_id(0)
    c = pl.program_id(1)
    n_e = rank_ref.shape[1]
    tn_dims = (((0,), (0,)), ((), ()))
    wins = _slot_windows(cnt_ref, b, c, n_e, n_chunk, cap)
    hots = [_one_hot_t(rank_ref[0, e:e + 1, :], wins[e][0]).astype(BF16) for e in range(n_e)]
    ys = [ye_ref[e, pl.ds(wins[e][0], ROUTE_WIN), :] for e in range(n_e)]
    moe = lax.dot_general(jnp.concatenate(hots, axis=0), jnp.concatenate(ys, axis=0), tn_dims,
                          preferred_element_type=F32)

    def finish(m):
        x = x1_ref[0] + g2_ref[0] * m
        o_ref[0] = _rms(x) * fn_ref[0]

    any_spill = sum(extra for _, extra in wins) > 0

    @pl.when(jnp.logical_not(any_spill))
    def _():
        finish(moe)

    @pl.when(any_spill)
    def _():
        acc_s[...] = moe
        for e in range(n_e):
            lo_al, extra = wins[e]

            def spill(k, carry, e=e, lo_al=lo_al):
                start, first = _spill_window(k, lo_al, cap)
                hot = _one_hot_t(rank_ref[0, e:e + 1, :], start, first).astype(BF16)
                acc_s[...] += lax.dot_general(hot, ye_ref[e, pl.ds(start, ROUTE_WIN), :], tn_dims,
                                              preferred_element_type=F32)
                return carry

            lax.fori_loop(1, extra + 1, spill, 0)
        finish(acc_s[...])


def _combine_call(cnt_flat, ye, rank_t, x1, g2, fnorm, cap):
    nb, n, d = x1.shape
    n_e = rank_t.shape[1]
    n_chunk = n // ROUTE_CHUNK
    grid_spec = pltpu.PrefetchScalarGridSpec(
        num_scalar_prefetch=1,
        grid=(nb, n_chunk),
        in_specs=[
            pl.BlockSpec((n_e, cap, d), lambda b, c, cnt: (0, b, 0)),
            pl.BlockSpec((1, n_e, ROUTE_CHUNK), lambda b, c, cnt: (b, 0, c)),
            pl.BlockSpec((1, ROUTE_CHUNK, d), lambda b, c, cnt: (b, c, 0)),
            pl.BlockSpec((1, 1, d), lambda b, c, cnt: (b, 0, 0)),
            pl.BlockSpec((1, 1, d), lambda b, c, cnt: (0, 0, 0)),
        ],
        out_specs=pl.BlockSpec((1, ROUTE_CHUNK, d), lambda b, c, cnt: (b, c, 0)),
        scratch_shapes=[pltpu.VMEM((ROUTE_CHUNK, d), F32)],
    )
    return pl.pallas_call(
        functools.partial(_combine_kernel, cap=cap, n_chunk=n_chunk),
        grid_spec=grid_spec,
        out_shape=jax.ShapeDtypeStruct((nb, n, d), F32),
        compiler_params=_cparams(("arbitrary", "arbitrary")),
        name="moe_combine_final_norm",
    )(cnt_flat, ye, rank_t, x1, g2, fnorm.reshape(1, 1, d))


def kernel(x, c, ctx, c_ctx, w_mod, b_mod, w_in, b_in, conv_w, conv_b, lru_wa, lru_ba, lru_wi, lru_bi,
           lru_lambda, na_rpb, w_proj_rnn, w_proj_na, w_out, w_router, w_exp_gate, w_exp_up, w_exp_down,
           final_norm):
    nb, n, d = x.shape
    l = ctx.shape[1]
    assert w_mod.shape[0] == 1, "single-layer problem"
    assert nb == SUBLANES and n % LRU_TN == 0 and l % LRU_TN == 0 and d % LRU_SLAB == 0
    lyr = 0

    cc = jnp.concatenate([c, c_ctx[None], jnp.zeros((2 * SUBLANES - nb - 1, d), F32)], axis=0)
    mod = _mod_call(cc, w_mod[lyr], b_mod[lyr])
    sh1, sc1, g1, sh2, sc2, g2 = [m[:nb, None, :] for m in jnp.split(mod, 6, axis=-1)]
    csh1, csc1 = [jnp.broadcast_to(m[nb:nb + 1, None, :], (nb, 1, d)) for m in jnp.split(mod, 6, axis=-1)[:2]]

    w_in_b = w_in[lyr].astype(BF16)
    b_in_r = b_in[lyr].reshape(1, -1)
    tables = _rope_tables(n)
    gy, xr_t, q, k, v, sgr, sgn = _inproj_call(
        x, sh1, sc1, w_in_b, b_in_r, tables,
        ("gelu", "tok_major", "rope_q", "rope_k", "bf16", "sigmoid", "sigmoid"), tuple(range(7)), "in_proj_latent",
        pair_major=(2,), col_blocked=(3, 4))
    xrc_t, kc, vc = _inproj_call(
        ctx, csh1, csc1, w_in_b, b_in_r, tables, ("tok_major", "bf16", "bf16"), (1, 3, 4), "in_proj_context",
        pair_major=(1, 2))

    gw = [_lru_gate_weights(lru_wa[lyr, dr], lru_ba[lyr, dr], lru_wi[lyr, dr], lru_bi[lyr, dr], lru_lambda[lyr, dr])
          for dr in range(2)]
    zeros_h = jnp.zeros((nb, d), F32)
    conv = (conv_w[lyr], conv_b[lyr])
    _, xcc_t, hc_f = _lru_call(xrc_t, *gw[0], zeros_h, conv=conv, reverse=False, name="rglru_ctx_fwd")
    _, hc_b = _lru_call(xcc_t, *gw[1], zeros_h, hf=xcc_t, gy=jnp.zeros((nb, l, d), BF16), reverse=True,
                        name="rglru_ctx_bwd")
    hf_t, xc_t, _ = _lru_call(xr_t, *gw[0], hc_f, conv=conv, reverse=False, name="rglru_fwd")
    y_rnn, _ = _lru_call(xc_t, *gw[1], hc_b, hf=hf_t, gy=gy, reverse=True, name="rglru_bwd")

    y_na = _na_call(q, k, v, kc, vc, _na_bias_table(na_rpb[lyr]))

    x1, xn2, lg_tiles = _mix_call(y_rnn, y_na, sgr, sgn, x, g1, sh2, sc2,
                                  w_proj_rnn[lyr].astype(BF16), w_proj_na[lyr].astype(BF16),
                                  w_out[lyr].astype(BF16), w_router[lyr].T.astype(BF16))
    n_e = w_router.shape[2]
    lg_t = lg_tiles.reshape(n // TQ, n_e, nb, TQ).transpose(2, 1, 0, 3).reshape(nb, n_e, n)

    cap = EC_CAPACITY * n // N_EXPERTS
    n_chunk = n // ROUTE_CHUNK
    assert cap % SLOT_ALIGN == 0 and cap >= ROUTE_WIN and n % ROUTE_CHUNK == 0
    rank_t, aff_t, cnt = _route_call(lg_t, cap)
    cnt_flat = cnt[:, :, :n_chunk + 1].reshape(-1)
    xe, ge = _dispatch_call(cnt_flat, xn2, rank_t, aff_t, cap)
    ye = _ffn_call(xe, ge, w_exp_gate[lyr], w_exp_up[lyr], w_exp_down[lyr])
    return _combine_call(cnt_flat, ye, rank_t, x1, g2, final_norm, cap)
```

```python
import functools

import numpy as np
import jax
import jax.numpy as jnp
from jax import lax
from jax.experimental import pallas as pl
from jax.experimental.pallas import tpu as pltpu

F32 = jnp.float32
BF16 = jnp.bfloat16

GRID_W = 64
N_HEADS = 16
HEAD_DIM = 64
N_LRU_BLOCKS = 16
CONV_W = 4
LRU_C = 8.0
WIN_R = 8
WIN_C = 16
ROPE_BASE = 10000.0
N_EXPERTS = 16
EC_CAPACITY = 2
EPS = 1e-6
NEG_INF = -1e30

LANES = 128
SUBLANES = 8
VMEM_LIMIT = 56 * 1024 * 1024

TQ = 128
LRU_SLAB = 256
LRU_TN = 128
FFN_TM = 2048
FFN_TF = 512
FFN_TN = 256
NA_CHUNK = 2
ROUTE_CHUNK = 256
ROUTE_WIN = 64
SLOT_ALIGN = 16
MIN_NORMAL_BITS = 0x00800000
LOG2_E = 1.4426950408889634


def _cparams(sem):
    return pltpu.CompilerParams(dimension_semantics=sem, vmem_limit_bytes=VMEM_LIMIT)


def _sigmoid(x):
    return 0.5 * jnp.tanh(0.5 * x) + 0.5


def _rms(x):
    return x * lax.rsqrt(jnp.mean(x * x, axis=-1, keepdims=True) + EPS)


def _mod_kernel(c_ref, w_ref, b_ref, o_ref):
    c = c_ref[...]
    s = c * jax.nn.sigmoid(c)
    o_ref[...] = jnp.dot(s, w_ref[...], preferred_element_type=F32) + b_ref[...]


def _mod_call(cc, w_mod, b_mod):
    rows, d = cc.shape
    n_out = w_mod.shape[1]
    tn = 1024
    return pl.pallas_call(
        _mod_kernel,
        grid=(n_out // tn,),
        in_specs=[
            pl.BlockSpec((rows, d), lambda j: (0, 0)),
            pl.BlockSpec((d, tn), lambda j: (0, j)),
            pl.BlockSpec((1, tn), lambda j: (0, j)),
        ],
        out_specs=pl.BlockSpec((rows, tn), lambda j: (0, j)),
        out_shape=jax.ShapeDtypeStruct((rows, n_out), F32),
        compiler_params=_cparams(("arbitrary",)),
        name="adaln_mod",
    )(cc, w_mod, b_mod.reshape(1, n_out))


def _rope(z, cos, sa, sb):
    outs = []
    for g in range(z.shape[1] // LANES):
        zg = z[:, g * LANES:(g + 1) * LANES]
        outs.append(zg * cos + pltpu.roll(zg, 16, 1) * sa + pltpu.roll(zg, LANES - 16, 1) * sb)
    return jnp.concatenate(outs, axis=1)


def _inproj_kernel(*refs, kinds, groups):
    n_out = len(kinds)
    x_ref, sh_ref, sc_ref, w_ref, b_ref, cos_ref, sa_ref, sb_ref = refs[:8]
    out_refs = refs[8:8 + n_out]
    nb, tq, d = x_ref.shape
    rows = nb * tq
    xn = _rms(x_ref[...]) * (1.0 + sc_ref[...]) + sh_ref[...]
    xb = xn.reshape(rows, d).astype(BF16)

    def tiled(t_ref):
        return jnp.broadcast_to(t_ref[...][None], (nb, tq, LANES)).reshape(rows, LANES)

    def store_tokens(o_ref, val):
        if o_ref.ndim == 3:
            o_ref[...] = val.reshape(nb, tq, d)
            return
        for p in range(d // LANES):
            vp = val[:, p * LANES:(p + 1) * LANES]
            if o_ref.ndim == 4:
                o_ref[:, p] = vp.reshape(nb, tq, LANES)
            else:
                n_cb = o_ref.shape[2]
                v5 = vp.reshape(nb, tq // GRID_W, n_cb, WIN_C, LANES)
                for gr in range(tq // GRID_W):
                    o_ref[:, p, :, gr * WIN_C:(gr + 1) * WIN_C, :] = v5[:, gr]

    for g, kind, o_ref in zip(groups, kinds, out_refs):
        z = jnp.dot(xb, w_ref[:, g * d:(g + 1) * d], preferred_element_type=F32) + b_ref[:, g * d:(g + 1) * d]
        if kind == "gelu":
            o_ref[...] = jax.nn.gelu(z, approximate=True).astype(BF16).reshape(nb, tq, d)
        elif kind == "tok_major":
            o_ref[...] = jnp.swapaxes(z.reshape(nb, tq, d), 0, 1).reshape(rows, d).astype(BF16)
        elif kind == "rope_q":
            r = _rope(z * (HEAD_DIM ** -0.5), tiled(cos_ref), tiled(sa_ref), tiled(sb_ref))
            store_tokens(o_ref, r.astype(BF16))
        elif kind == "rope_k":
            r = _rope(z, tiled(cos_ref), tiled(sa_ref), tiled(sb_ref))
            store_tokens(o_ref, r.astype(BF16))
        elif kind == "bf16":
            store_tokens(o_ref, z.astype(BF16))
        elif kind == "sigmoid":
            o_ref[...] = _sigmoid(z).astype(BF16).reshape(nb, tq, d)
        else:
            raise ValueError(kind)


def _inproj_call(x, sh, sc, w, b, tables, kinds, groups, name, pair_major=(), col_blocked=()):
    nb, n, d = x.shape
    assert len(groups) == len(kinds) and w.shape[0] == d and max(groups) * d < w.shape[1]
    cos, sa, sb = tables
    bat = lambda i: (0, i, 0)
    const3 = lambda i: (0, 0, 0)
    const2 = lambda i: (0, 0)
    out_shapes, out_specs = [], []
    for kind in kinds:
        if kind == "tok_major":
            out_shapes.append(jax.ShapeDtypeStruct((n * nb, d), BF16))
            out_specs.append(pl.BlockSpec((TQ * nb, d), lambda i: (i, 0)))
        elif len(out_shapes) in col_blocked:
            assert TQ % GRID_W == 0
            n_cb = GRID_W // WIN_C
            out_shapes.append(jax.ShapeDtypeStruct((nb, d // LANES, n_cb, n // n_cb, LANES), BF16))
            out_specs.append(pl.BlockSpec((nb, d // LANES, n_cb, TQ // n_cb, LANES), lambda i: (0, 0, 0, i, 0)))
        elif len(out_shapes) in pair_major:
            out_shapes.append(jax.ShapeDtypeStruct((nb, d // LANES, n, LANES), BF16))
            out_specs.append(pl.BlockSpec((nb, d // LANES, TQ, LANES), lambda i: (0, 0, i, 0)))
        else:
            out_shapes.append(jax.ShapeDtypeStruct((nb, n, d), BF16))
            out_specs.append(pl.BlockSpec((nb, TQ, d), bat))
    rows_per_table = cos.shape[0] // TQ
    return pl.pallas_call(
        functools.partial(_inproj_kernel, kinds=kinds, groups=groups),
        grid=(n // TQ,),
        in_specs=[
            pl.BlockSpec((nb, TQ, d), bat),
            pl.BlockSpec(sh.shape, const3),
            pl.BlockSpec(sc.shape, const3),
            pl.BlockSpec(w.shape, const2, pipeline_mode=pl.Buffered(1)),
            pl.BlockSpec(b.shape, const2),
            pl.BlockSpec((TQ, LANES), lambda i: (i % rows_per_table, 0)),
            pl.BlockSpec((TQ, LANES), lambda i: (i % rows_per_table, 0)),
            pl.BlockSpec((TQ, LANES), lambda i: (i % rows_per_table, 0)),
        ],
        out_specs=out_specs,
        out_shape=out_shapes,
        compiler_params=_cparams(("arbitrary",)),
        name=name,
    )(x, sh, sc, w, b, cos, sa, sb)


def _rope_tables(n):
    t = np.arange(n)
    row = (t // GRID_W).astype(np.float32)
    col = (t % GRID_W).astype(np.float32)
    n_freq = HEAD_DIM // 4
    inv = (np.float32(ROPE_BASE) ** (-np.arange(n_freq, dtype=np.float32) / np.float32(n_freq))).astype(np.float32)
    ang_r = (row[:, None] * inv).astype(np.float32)
    ang_c = (col[:, None] * inv).astype(np.float32)
    zero = np.zeros_like(ang_r)
    cos_h = np.concatenate([np.cos(ang_r), np.cos(ang_r), np.cos(ang_c), np.cos(ang_c)], axis=1)
    sa_h = np.concatenate([zero, np.sin(ang_r), zero, np.sin(ang_c)], axis=1)
    sb_h = np.concatenate([-np.sin(ang_r), zero, -np.sin(ang_c), zero], axis=1)
    rep = LANES // HEAD_DIM
    return tuple(jnp.asarray(np.tile(a.astype(np.float32), (1, rep))) for a in (cos_h, sa_h, sb_h))


def _lru_kernel(*refs, tn, nt, reverse, fuse_out):
    if fuse_out:
        (xc_ref, wg_ref, bg_ref, lam_ref, h0_ref, hf_ref, gy_ref, out_ref, hfin_ref, a_s, b_s, hcar_s) = refs
    else:
        (xp_ref, xin_ref, xn_ref, cw_ref, cb_ref, wg_ref, bg_ref, lam_ref, h0_ref,
         out_ref, xc_ref, hfin_ref, a_s, b_s, hcar_s) = refs
    i = pl.program_id(0)
    ti = (nt - 1 - i) if reverse else i
    nb, d = hcar_s.shape

    @pl.when(i == 0)
    def _():
        hcar_s[...] = h0_ref[...]

    def tokens(ref):
        return ref[...].astype(F32).reshape(ref.shape[0] // nb, nb, d)

    if fuse_out:
        xc = tokens(xc_ref)
    else:
        x = tokens(xin_ref)
        xp = jnp.where(ti == 0, 0.0, tokens(xp_ref))
        xnx = jnp.where(ti == nt - 1, 0.0, tokens(xn_ref)[:1])
        ext = jnp.concatenate([xp, x, xnx], axis=0)
        xc = cb_ref[...]
        for k in range(CONV_W):
            xc = xc + ext[k:k + tn] * cw_ref[k]
        xc_ref[...] = xc.reshape(tn * nb, d).astype(BF16)
    rows = tn * nb
    xc2 = xc.reshape(rows, d)
    for s in range(d // LRU_SLAB):
        sl = slice(s * LRU_SLAB, (s + 1) * LRU_SLAB)
        xs = xc2[:, sl]
        half_gates = jnp.dot(xs.astype(BF16), wg_ref[s], preferred_element_type=F32) + bg_ref[s]
        t_r = jnp.tanh(half_gates[:, :LRU_SLAB])
        ig = 0.5 * jnp.tanh(half_gates[:, LRU_SLAB:]) + 0.5
        z = -lam_ref[s]
        softplus = jnp.maximum(z, 0.0) + jnp.log1p(jnp.exp(-jnp.abs(z)))
        k = (-0.5 * LRU_C * LOG2_E) * softplus
        a = jnp.exp2(t_r * k + k)
        b = jnp.sqrt(1.0 - a * a) * (ig * xs)
        a_s[:, :, sl] = a.reshape(tn, nb, LRU_SLAB)
        b_s[:, :, sl] = b.reshape(tn, nb, LRU_SLAB)

    unroll = 8

    def step(j, h):
        for k in range(unroll):
            jj = j * unroll + k
            t = (tn - 1 - jj) if reverse else jj
            h = a_s[t] * h + b_s[t]
            a_s[t] = h
        return h

    h = lax.fori_loop(0, tn // unroll, step, hcar_s[...])
    hcar_s[...] = h
    hfin_ref[...] = h
    if fuse_out:
        hs = jnp.swapaxes(tokens(hf_ref) + a_s[...], 0, 1)
        out_ref[...] = (hs * gy_ref[...].astype(F32)).astype(BF16)
    else:
        out_ref[...] = a_s[...].reshape(tn * nb, d).astype(BF16)


def _lru_call(x_t, wg, bg, lam, h0, conv=None, hf=None, gy=None, *, reverse, name):
    nb, d = h0.shape
    n = x_t.shape[0] // nb
    tn = LRU_TN
    nt = n // tn
    halo = 2
    fuse_out = conv is None
    tile = (lambda i: nt - 1 - i) if reverse else (lambda i: i)
    n_slab = d // LRU_SLAB
    tok_spec = pl.BlockSpec((tn * nb, d), lambda i: (tile(i), 0))
    gate_specs = [
        pl.BlockSpec((n_slab, LRU_SLAB, 2 * LRU_SLAB), lambda i: (0, 0, 0)),
        pl.BlockSpec((n_slab, 1, 2 * LRU_SLAB), lambda i: (0, 0, 0)),
        pl.BlockSpec((n_slab, 1, LRU_SLAB), lambda i: (0, 0, 0)),
        pl.BlockSpec((nb, d), lambda i: (0, 0)),
    ]
    fin_spec = pl.BlockSpec((nb, d), lambda i: (0, 0))
    fin_shape = jax.ShapeDtypeStruct((nb, d), F32)
    tok_shape = jax.ShapeDtypeStruct((n * nb, d), BF16)
    if fuse_out:
        in_specs = [tok_spec] + gate_specs + [tok_spec, pl.BlockSpec((nb, tn, d), lambda i: (0, tile(i), 0))]
        args = [x_t, wg, bg, lam, h0, hf, gy]
        out_specs = [pl.BlockSpec((nb, tn, d), lambda i: (0, tile(i), 0)), fin_spec]
        out_shape = [jax.ShapeDtypeStruct((nb, n, d), BF16), fin_shape]
    else:
        conv_w, conv_b = conv
        in_specs = [
            pl.BlockSpec((halo * nb, d), lambda i: (jnp.maximum(tile(i) * (tn // halo) - 1, 0), 0)),
            tok_spec,
            pl.BlockSpec((halo * nb, d), lambda i: (jnp.minimum((tile(i) + 1) * (tn // halo), n // halo - 1), 0)),
            pl.BlockSpec((CONV_W, 1, d), lambda i: (0, 0, 0)),
            pl.BlockSpec((1, d), lambda i: (0, 0)),
        ] + gate_specs
        args = [x_t, x_t, x_t, conv_w.reshape(CONV_W, 1, d), conv_b.reshape(1, d), wg, bg, lam, h0]
        out_specs = [tok_spec, tok_spec, fin_spec]
        out_shape = [tok_shape, tok_shape, fin_shape]
    return pl.pallas_call(
        functools.partial(_lru_kernel, tn=tn, nt=nt, reverse=reverse, fuse_out=fuse_out),
        grid=(nt,),
        in_specs=in_specs,
        out_specs=out_specs,
        out_shape=out_shape,
        scratch_shapes=[
            pltpu.VMEM((tn, nb, d), F32),
            pltpu.VMEM((tn, nb, d), F32),
            pltpu.VMEM((nb, d), F32),
        ],
        compiler_params=_cparams(("arbitrary",)),
        name=name,
    )(*args)


def _lru_gate_weights(wa, ba, wi, bi, lam):
    n_blk, blk, _ = wa.shape
    d = n_blk * blk
    n_slab = d // LRU_SLAB
    per = LRU_SLAB // blk
    eye = jnp.eye(per, dtype=F32)

    def dense_slabs(w):
        return jnp.einsum("saij,ac->saicj", w.reshape(n_slab, per, blk, blk), eye).reshape(n_slab, LRU_SLAB, LRU_SLAB)

    wg = (0.5 * jnp.concatenate([dense_slabs(wa), dense_slabs(wi)], axis=2).astype(BF16)).astype(BF16)
    bg = 0.5 * jnp.concatenate([ba.reshape(n_slab, 1, LRU_SLAB), bi.reshape(n_slab, 1, LRU_SLAB)], axis=2)
    return wg, bg, lam.reshape(n_slab, 1, LRU_SLAB)


def _na_blocks(g):
    lo = min(max(8 * g - WIN_C // 2, 0) // WIN_C, GRID_W // WIN_C - 2)
    return (lo, lo + 1)


def _na_kernel(q_ref, k_ref, v_ref, kc_ref, vc_ref, tab_ref, o_ref, s_s, p_s, pc_s, inv_s, qs_s, sc_s, *, rows):
    win = WIN_R * GRID_W
    ch = NA_CHUNK
    n_chunk = rows // ch
    qh = 2 * GRID_W
    n_cb = GRID_W // WIN_C
    blk = WIN_R * WIN_C
    grp = SUBLANES
    lane = lax.broadcasted_iota(jnp.int32, (GRID_W, LANES), 1)
    first_head = lane < HEAD_DIM
    nt_dims = (((1,), (1,)), ((), ()))

    def row_start(r):
        return pl.multiple_of(r * GRID_W, GRID_W)

    def window(ref, rs):
        run = pl.ds(pl.multiple_of(rs * WIN_C, WIN_C), blk)
        return jnp.concatenate([ref[0, 0, cb, run, :] for cb in range(n_cb)], axis=0)

    def groups():
        for h in range(2):
            for g in range(GRID_W // grp):
                yield h, g * grp, _na_blocks(g)

    p_s[...] = jnp.zeros_like(p_s)

    def stack_heads(r, carry):
        q = q_ref[0, 0, pl.ds(row_start(r), GRID_W), :]
        zero = jnp.zeros_like(q)
        qs_s[r] = jnp.concatenate([jnp.where(first_head, q, zero), jnp.where(first_head, zero, q)], axis=0)
        return carry

    lax.fori_loop(0, rows, stack_heads, 0)
    sc_s[...] = lax.dot_general(qs_s[...].reshape(rows * qh, LANES), kc_ref[0, 0], nt_dims,
                                preferred_element_type=F32)

    def stage1(c, slot):
        for u in range(ch):
            r = c * ch + u
            rs = jnp.clip(r - WIN_R // 2, 0, rows - WIN_R)
            s = lax.dot_general(qs_s[r], window(k_ref, rs), nt_dims, preferred_element_type=F32)
            d0 = rs - r + (WIN_R - 1)
            for h, q0, cbs in groups():
                for cb in cbs:
                    rw = slice(h * GRID_W + q0, h * GRID_W + q0 + grp)
                    ln = slice(cb * blk, (cb + 1) * blk)
                    s_s[slot, u, rw, ln] = s[rw, ln] + tab_ref[0, h, d0, q0:q0 + grp, ln]

    def stage2(c, slot):
        for u in range(ch):
            base = pl.multiple_of((c * ch + u) * qh, qh)
            for r0 in range(0, qh, 2 * grp):
                parts = {}
                pcs, invs = [], []
                for half in range(2):
                    ra = r0 + half * grp
                    cbs = _na_blocks((ra % GRID_W) // grp)
                    sl = [s_s[slot, u, ra:ra + grp, cb * blk:(cb + 1) * blk] for cb in cbs]
                    sc = sc_s[pl.ds(base + ra, grp), :]
                    m = jnp.maximum(jnp.max(jnp.maximum(sl[0], sl[1]), axis=1, keepdims=True),
                                    jnp.max(sc, axis=1, keepdims=True))
                    ps = [jnp.exp(x - m) for x in sl]
                    pc = jnp.exp(sc - m)
                    den = jnp.sum(ps[0] + ps[1], axis=1, keepdims=True) + jnp.sum(pc, axis=1, keepdims=True)
                    for cb, pb in zip(cbs, ps):
                        parts[(half, cb)] = pb
                    pcs.append(pc)
                    invs.append(jnp.broadcast_to(1.0 / den, (grp, LANES)))
                zero = jnp.zeros((grp, blk), F32)
                for cb in sorted({cb for _, cb in parts}):
                    both = jnp.concatenate([parts.get((0, cb), zero), parts.get((1, cb), zero)], axis=0)
                    p_s[slot, u, r0:r0 + 2 * grp, cb * blk:(cb + 1) * blk] = both.astype(BF16)
                pc_s[slot, u * qh + r0:u * qh + r0 + 2 * grp, :] = jnp.concatenate(pcs, axis=0).astype(BF16)
                inv_s[slot, u, r0:r0 + 2 * grp, :] = jnp.concatenate(invs, axis=0)

    def stage3(c, slot):
        oc = jnp.dot(pc_s[slot], vc_ref[0, 0], preferred_element_type=F32)
        for u in range(ch):
            r = c * ch + u
            rs = jnp.clip(r - WIN_R // 2, 0, rows - WIN_R)
            o = (jnp.dot(p_s[slot, u], window(v_ref, rs), preferred_element_type=F32)
                 + oc[u * qh:(u + 1) * qh]) * inv_s[slot, u]
            out = jnp.where(first_head, o[:GRID_W], o[GRID_W:])
            o_ref[0, 0, pl.ds(row_start(r), GRID_W), :] = out.astype(BF16)

    stage1(0, 0)
    stage2(0, 0)
    stage1(1, 1)

    def body(j, carry):
        slot = j % 2
        stage3(j - 2, slot)
        stage2(j - 1, 1 - slot)
        stage1(j, slot)
        return carry

    lax.fori_loop(2, n_chunk, body, 0)
    last = n_chunk % 2
    stage3(n_chunk - 2, last)
    stage2(n_chunk - 1, 1 - last)
    stage3(n_chunk - 1, 1 - last)


def _na_call(q, k, v, kc, vc, tab):
    nb, n_pair, n, _ = q.shape
    l = kc.shape[2]
    rows = n // GRID_W
    n_cb = GRID_W // WIN_C
    assert k.shape == v.shape == (nb, n_pair, n_cb, n // n_cb, LANES)
    win = WIN_R * GRID_W
    qh = 2 * GRID_W
    assert rows % NA_CHUNK == 0 and rows // NA_CHUNK >= 2
    tok_spec = pl.BlockSpec((1, 1, n, LANES), lambda b, p: (b, p, 0, 0))
    kv_spec = pl.BlockSpec((1, 1, n_cb, n // n_cb, LANES), lambda b, p: (b, p, 0, 0, 0))
    ctx_spec = pl.BlockSpec((1, 1, l, LANES), lambda b, p: (b, p, 0, 0))
    return pl.pallas_call(
        functools.partial(_na_kernel, rows=rows),
        grid=(nb, n_pair),
        in_specs=[tok_spec, kv_spec, kv_spec, ctx_spec, ctx_spec,
                  pl.BlockSpec((1,) + tab.shape[1:], lambda b, p: (p, 0, 0, 0, 0))],
        out_specs=tok_spec,
        out_shape=jax.ShapeDtypeStruct((nb, n_pair, n, LANES), BF16),
        scratch_shapes=[
            pltpu.VMEM((2, NA_CHUNK, qh, win), F32),
            pltpu.VMEM((2, NA_CHUNK, qh, win), BF16),
            pltpu.VMEM((2, NA_CHUNK * qh, l), BF16),
            pltpu.VMEM((2, NA_CHUNK, qh, LANES), F32),
            pltpu.VMEM((rows, qh, LANES), BF16),
            pltpu.VMEM((rows * qh, l), F32),
        ],
        compiler_params=_cparams(("arbitrary", "arbitrary")),
        name="neighbourhood_attention",
    )(q, k, v, kc, vc, tab)


def _na_bias_table(rpb):
    qc = np.arange(GRID_W)
    kc = np.arange(GRID_W)
    n_cb = GRID_W // WIN_C
    cstart = np.clip(qc - WIN_C // 2, 0, GRID_W - WIN_C)
    ok = (kc[None, :] >= cstart[:, None]) & (kc[None, :] < cstart[:, None] + WIN_C)
    dc = np.clip(kc[None, :] - qc[:, None], -(WIN_C - 1), WIN_C - 1) + WIN_C - 1
    expand = (dc[None] == np.arange(2 * WIN_C - 1)[:, None, None]).astype(np.float32)
    expand = expand.reshape(2 * WIN_C - 1, GRID_W, n_cb, WIN_C)
    mask = np.where(ok, 0.0, NEG_INF).astype(np.float32).reshape(GRID_W, n_cb, 1, WIN_C)
    h = rpb.shape[0]
    rows = jnp.stack([rpb[:, d0:d0 + WIN_R].astype(F32) for d0 in range(WIN_R)], axis=1)
    tab = jnp.einsum("hdjx,xqbc->hdqbjc", rows, jnp.asarray(expand), precision=lax.Precision.HIGHEST)
    tab = tab + jnp.asarray(mask)
    return tab.reshape(h // 2, 2, WIN_R, GRID_W, WIN_R * GRID_W)


def _mix_kernel(yr_ref, yn_ref, gr_ref, gn_ref, x_ref, g1_ref, sh2_ref, sc2_ref,
                wpr_ref, wpn_ref, wo_ref, wr_ref, x1_ref, xn2_ref, lg_ref):
    nb, tq, d = x_ref.shape
    rows = nb * tq
    pr = jnp.dot(yr_ref[...].reshape(rows, d), wpr_ref[...], preferred_element_type=F32)
    yn = jnp.concatenate([yn_ref[:, p].reshape(rows, LANES) for p in range(d // LANES)], axis=1)
    pn = jnp.dot(yn, wpn_ref[...], preferred_element_type=F32)
    mix = gr_ref[...].reshape(rows, d).astype(F32) * pr + gn_ref[...].reshape(rows, d).astype(F32) * pn
    o = jnp.dot(mix.astype(BF16), wo_ref[...], preferred_element_type=F32)
    x1 = x_ref[...] + g1_ref[...] * o.reshape(nb, tq, d)
    x1_ref[...] = x1
    xb = (_rms(x1) * (1.0 + sc2_ref[...]) + sh2_ref[...]).astype(BF16)
    xn2_ref[...] = xb
    lg_ref[0] = lax.dot_general(wr_ref[...], xb.reshape(rows, d), (((1,), (1,)), ((), ())),
                                preferred_element_type=F32)


def _mix_call(yr, yn, gr, gn, x, g1, sh2, sc2, wpr, wpn, wo, wr):
    nb, n, d = x.shape
    bat = lambda i: (0, i, 0)
    const3 = lambda i: (0, 0, 0)
    const2 = lambda i: (0, 0)
    act = pl.BlockSpec((nb, TQ, d), bat)
    vec = pl.BlockSpec((nb, 1, d), const3)
    wsp = lambda w: pl.BlockSpec(w.shape, const2, pipeline_mode=pl.Buffered(1))
    return pl.pallas_call(
        _mix_kernel,
        grid=(n // TQ,),
        in_specs=[act, pl.BlockSpec((nb, d // LANES, TQ, LANES), lambda i: (0, 0, i, 0)), act, act, act,
                  vec, vec, vec, wsp(wpr), wsp(wpn), wsp(wo), wsp(wr)],
        out_specs=[act, act, pl.BlockSpec((1, wr.shape[0], nb * TQ), lambda i: (i, 0, 0))],
        out_shape=[
            jax.ShapeDtypeStruct((nb, n, d), F32),
            jax.ShapeDtypeStruct((nb, n, d), BF16),
            jax.ShapeDtypeStruct((n // TQ, wr.shape[0], nb * TQ), F32),
        ],
        compiler_params=_cparams(("arbitrary",)),
        name="merge_out_norm_router",
    )(yr, yn, gr, gn, x, g1, sh2, sc2, wpr, wpn, wo, wr)


def _route_kernel(lg_ref, rank_ref, aff_ref, cnt_ref, *, cap):
    lg = lg_ref[0]
    n_e, n = lg.shape
    ex = jnp.exp(lg - jnp.max(lg, axis=0, keepdims=True))
    aff = ex / jnp.sum(ex, axis=0, keepdims=True)
    aff_ref[0] = aff

    def as_float(bits):
        return lax.bitcast_convert_type(bits, F32)

    thr = jnp.zeros((n_e, 1), jnp.int32)
    for bit in range(30, -1, -1):
        cand = thr | (1 << bit)
        cnt = jnp.sum((aff >= as_float(cand)).astype(F32), axis=1, keepdims=True)
        thr = jnp.where(cnt >= cap, cand, thr)
    thr = jnp.where(thr < MIN_NORMAL_BITS, 0, thr)
    gt = aff >= as_float(jnp.where(thr == 0, MIN_NORMAL_BITS, thr + 1))
    eq = (aff >= as_float(thr)) & jnp.logical_not(gt)
    need = cap - jnp.sum(gt.astype(F32), axis=1, keepdims=True)

    blk = ROUTE_CHUNK
    row = lax.broadcasted_iota(jnp.int32, (blk, blk), 0)
    col = lax.broadcasted_iota(jnp.int32, (blk, blk), 1)
    upper = (row <= col).astype(BF16)

    def cumsum_tokens(mask):
        off = jnp.zeros((n_e, 1), F32)
        outs, offs = [], []
        for c in range(n // blk):
            x = mask[:, c * blk:(c + 1) * blk].astype(BF16)
            cs = jnp.dot(x, upper, preferred_element_type=F32) + off
            offs.append(off)
            outs.append(cs)
            off = cs[:, blk - 1:blk]
        offs.append(off)
        return jnp.concatenate(outs, axis=1), offs

    cum_eq, _ = cumsum_tokens(eq)
    sel = gt | (eq & ((cum_eq - eq.astype(F32)) < need))
    cum_sel, offs = cumsum_tokens(sel)
    rank_ref[0] = jnp.where(sel, cum_sel - 1.0, -1.0).astype(jnp.int32)
    pad = jnp.zeros((n_e, LANES - len(offs)), F32)
    cnt_ref[0] = jnp.concatenate(offs + [pad], axis=1).astype(jnp.int32)


def _route_call(lg_t, cap):
    nb, n_e, n = lg_t.shape
    blk3 = lambda b: (b, 0, 0)
    return pl.pallas_call(
        functools.partial(_route_kernel, cap=cap),
        grid=(nb,),
        in_specs=[pl.BlockSpec((1, n_e, n), blk3)],
        out_specs=[pl.BlockSpec((1, n_e, n), blk3), pl.BlockSpec((1, n_e, n), blk3),
                   pl.BlockSpec((1, n_e, LANES), blk3)],
        out_shape=[jax.ShapeDtypeStruct((nb, n_e, n), jnp.int32), jax.ShapeDtypeStruct((nb, n_e, n), F32),
                   jax.ShapeDtypeStruct((nb, n_e, LANES), jnp.int32)],
        compiler_params=_cparams(("arbitrary",)),
        name="route_select",
    )(lg_t)


def _slot_windows(cnt_ref, b, c, n_e, n_chunk, cap):
    wins = []
    for e in range(n_e):
        base = (b * n_e + e) * (n_chunk + 1) + c
        lo = cnt_ref[base]
        hi = cnt_ref[base + 1]
        lo_al = jnp.minimum((lo // SLOT_ALIGN) * SLOT_ALIGN, cap - ROUTE_WIN)
        extra = jnp.maximum((hi - lo_al + ROUTE_WIN - 1) // ROUTE_WIN - 1, 0)
        wins.append((pl.multiple_of(lo_al, SLOT_ALIGN), extra))
    return wins


def _one_hot_t(rank_row, start, first=None):
    slot = start + lax.broadcasted_iota(jnp.int32, (ROUTE_WIN, rank_row.shape[1]), 0)
    hit = rank_row == slot
    if first is not None:
        hit = hit & (slot >= first)
    return hit


def _spill_window(k, lo_al, cap):
    first = lo_al + k * ROUTE_WIN
    start = pl.multiple_of(jnp.minimum(first, cap - ROUTE_WIN), SLOT_ALIGN)
    return start, first


def _dispatch_kernel(cnt_ref, x_ref, rank_ref, aff_ref, xe_ref, g_ref, *, cap, n_chunk):
    b = pl.program_id(0)
    c = pl.program_id(1)
    n_e = rank_ref.shape[1]

    @pl.when(c == 0)
    def _():
        xe_ref[...] = jnp.zeros_like(xe_ref)
        g_ref[...] = jnp.zeros_like(g_ref)

    x = x_ref[0]
    wins = _slot_windows(cnt_ref, b, c, n_e, n_chunk, cap)
    hots = [_one_hot_t(rank_ref[0, e:e + 1, :], wins[e][0]) for e in range(n_e)]
    stacked = jnp.concatenate([h.astype(BF16) for h in hots], axis=0)
    rows = jnp.dot(stacked, x, preferred_element_type=F32)
    for e in range(n_e):
        sl = pl.ds(wins[e][0], ROUTE_WIN)
        xe_ref[e, sl, :] = xe_ref[e, sl, :] + rows[e * ROUTE_WIN:(e + 1) * ROUTE_WIN].astype(BF16)
        g_ref[e, sl, :] = g_ref[e, sl, :] + jnp.sum(jnp.where(hots[e], aff_ref[0, e:e + 1, :], 0.0),
                                                    axis=1, keepdims=True)

    @pl.when(sum(extra for _, extra in wins) > 0)
    def _():
        for e in range(n_e):
            lo_al, extra = wins[e]

            def spill(k, carry, e=e, lo_al=lo_al):
                start, first = _spill_window(k, lo_al, cap)
                hot = _one_hot_t(rank_ref[0, e:e + 1, :], start, first)
                sl2 = pl.ds(start, ROUTE_WIN)
                xe_ref[e, sl2, :] = xe_ref[e, sl2, :] + jnp.dot(hot.astype(BF16), x_ref[0],
                                                               preferred_element_type=F32).astype(BF16)
                g_ref[e, sl2, :] = g_ref[e, sl2, :] + jnp.sum(jnp.where(hot, aff_ref[0, e:e + 1, :], 0.0),
                                                              axis=1, keepdims=True)
                return carry

            lax.fori_loop(1, extra + 1, spill, 0)


def _dispatch_call(cnt_flat, xn2, rank_t, aff_t, cap):
    nb, n, d = xn2.shape
    n_e = rank_t.shape[1]
    n_chunk = n // ROUTE_CHUNK
    grid_spec = pltpu.PrefetchScalarGridSpec(
        num_scalar_prefetch=1,
        grid=(nb, n_chunk),
        in_specs=[
            pl.BlockSpec((1, ROUTE_CHUNK, d), lambda b, c, cnt: (b, c, 0)),
            pl.BlockSpec((1, n_e, ROUTE_CHUNK), lambda b, c, cnt: (b, 0, c)),
            pl.BlockSpec((1, n_e, ROUTE_CHUNK), lambda b, c, cnt: (b, 0, c)),
        ],
        out_specs=[
            pl.BlockSpec((n_e, cap, d), lambda b, c, cnt: (0, b, 0)),
            pl.BlockSpec((n_e, cap, 1), lambda b, c, cnt: (0, b, 0)),
        ],
    )
    return pl.pallas_call(
        functools.partial(_dispatch_kernel, cap=cap, n_chunk=n_chunk),
        grid_spec=grid_spec,
        out_shape=[jax.ShapeDtypeStruct((n_e, nb * cap, d), BF16), jax.ShapeDtypeStruct((n_e, nb * cap, 1), F32)],
        compiler_params=_cparams(("arbitrary", "arbitrary")),
        name="moe_dispatch",
    )(cnt_flat, xn2, rank_t, aff_t)


def _ffn_kernel(xe_ref, g_ref, wg_ref, wu_ref, wd_ref, o_ref, acc_s):
    f = pl.program_id(2)
    last = pl.num_programs(2) - 1
    d = o_ref.shape[2]
    xe = xe_ref[0]
    h1 = jnp.dot(xe, wg_ref[0].astype(BF16), preferred_element_type=F32)
    h2 = jnp.dot(xe, wu_ref[0].astype(BF16), preferred_element_type=F32)
    hid = (h1 * _sigmoid(h1) * h2).astype(BF16)
    wd = wd_ref[0].astype(BF16)

    def chunks():
        for c in range(d // FFN_TN):
            cols = slice(c * FFN_TN, (c + 1) * FFN_TN)
            yield cols, jnp.dot(hid, wd[:, cols], preferred_element_type=F32)

    @pl.when(f == 0)
    def _():
        for cols, part in chunks():
            acc_s[:, cols] = part

    @pl.when((f > 0) & (f < last))
    def _():
        for cols, part in chunks():
            acc_s[:, cols] += part

    @pl.when(f == last)
    def _():
        for cols, part in chunks():
            o_ref[0, :, cols] = ((acc_s[:, cols] + part) * g_ref[0]).astype(BF16)


def _ffn_call(xe, g, w_gate, w_up, w_down):
    e, m, d = xe.shape
    dff = w_gate.shape[2]
    tm = min(FFN_TM, m)
    assert dff // FFN_TF >= 2 and m % tm == 0 and d % FFN_TN == 0
    return pl.pallas_call(
        _ffn_kernel,
        grid=(e, m // tm, dff // FFN_TF),
        in_specs=[
            pl.BlockSpec((1, tm, d), lambda ei, mi, fi: (ei, mi, 0)),
            pl.BlockSpec((1, tm, 1), lambda ei, mi, fi: (ei, mi, 0)),
            pl.BlockSpec((1, d, FFN_TF), lambda ei, mi, fi: (ei, 0, fi)),
            pl.BlockSpec((1, d, FFN_TF), lambda ei, mi, fi: (ei, 0, fi)),
            pl.BlockSpec((1, FFN_TF, d), lambda ei, mi, fi: (ei, fi, 0)),
        ],
        out_specs=pl.BlockSpec((1, tm, d), lambda ei, mi, fi: (ei, mi, 0)),
        out_shape=jax.ShapeDtypeStruct((e, m, d), BF16),
        scratch_shapes=[pltpu.VMEM((tm, d), F32)],
        compiler_params=_cparams(("arbitrary", "arbitrary", "arbitrary")),
        name="expert_ffn",
    )(xe, g, w_gate, w_up, w_down)


def _combine_kernel(cnt_ref, ye_ref, rank_ref, x1_ref, g2_ref, fn_ref, o_ref, acc_s, *, cap, n_chunk):
    b = pl.program_id(0)
    c = pl.program_id(1)
    n_e = rank_ref.shape[1]
    tn_dims = (((0,), (0,)), ((), ()))
    wins = _slot_windows(cnt_ref, b, c, n_e, n_chunk, cap)
    hots = [_one_hot_t(rank_ref[0, e:e + 1, :], wins[e][0]).astype(BF16) for e in range(n_e)]
    ys = [ye_ref[e, pl.ds(wins[e][0], ROUTE_WIN), :] for e in range(n_e)]
    moe = lax.dot_general(jnp.concatenate(hots, axis=0), jnp.concatenate(ys, axis=0), tn_dims,
                          preferred_element_type=F32)

    def finish(m):
        x = x1_ref[0] + g2_ref[0] * m
        o_ref[0] = _rms(x) * fn_ref[0]

    any_spill = sum(extra for _, extra in wins) > 0

    @pl.when(jnp.logical_not(any_spill))
    def _():
        finish(moe)

    @pl.when(any_spill)
    def _():
        acc_s[...] = moe
        for e in range(n_e):
            lo_al, extra = wins[e]

            def spill(k, carry, e=e, lo_al=lo_al):
                start, first = _spill_window(k, lo_al, cap)
                hot = _one_hot_t(rank_ref[0, e:e + 1, :], start, first).astype(BF16)
                acc_s[...] += lax.dot_general(hot, ye_ref[e, pl.ds(start, ROUTE_WIN), :], tn_dims,
                                              preferred_element_type=F32)
                return carry

            lax.fori_loop(1, extra + 1, spill, 0)
        finish(acc_s[...])


def _combine_call(cnt_flat, ye, rank_t, x1, g2, fnorm, cap):
    nb, n, d = x1.shape
    n_e = rank_t.shape[1]
    n_chunk = n // ROUTE_CHUNK
    grid_spec = pltpu.PrefetchScalarGridSpec(
        num_scalar_prefetch=1,
        grid=(nb, n_chunk),
        in_specs=[
            pl.BlockSpec((n_e, cap, d), lambda b, c, cnt: (0, b, 0)),
            pl.BlockSpec((1, n_e, ROUTE_CHUNK), lambda b, c, cnt: (b, 0, c)),
            pl.BlockSpec((1, ROUTE_CHUNK, d), lambda b, c, cnt: (b, c, 0)),
            pl.BlockSpec((1, 1, d), lambda b, c, cnt: (b, 0, 0)),
            pl.BlockSpec((1, 1, d), lambda b, c, cnt: (0, 0, 0)),
        ],
        out_specs=pl.BlockSpec((1, ROUTE_CHUNK, d), lambda b, c, cnt: (b, c, 0)),
        scratch_shapes=[pltpu.VMEM((ROUTE_CHUNK, d), F32)],
    )
    return pl.pallas_call(
        functools.partial(_combine_kernel, cap=cap, n_chunk=n_chunk),
        grid_spec=grid_spec,
        out_shape=jax.ShapeDtypeStruct((nb, n, d), F32),
        compiler_params=_cparams(("arbitrary", "arbitrary")),
        name="moe_combine_final_norm",
    )(cnt_flat, ye, rank_t, x1, g2, fnorm.reshape(1, 1, d))


def kernel(x, c, ctx, c_ctx, w_mod, b_mod, w_in, b_in, conv_w, conv_b, lru_wa, lru_ba, lru_wi, lru_bi,
           lru_lambda, na_rpb, w_proj_rnn, w_proj_na, w_out, w_router, w_exp_gate, w_exp_up, w_exp_down,
           final_norm):
    nb, n, d = x.shape
    l = ctx.shape[1]
    assert w_mod.shape[0] == 1, "single-layer problem"
    assert nb == SUBLANES and n % LRU_TN == 0 and l % LRU_TN == 0 and d % LRU_SLAB == 0
    lyr = 0

    cc = jnp.concatenate([c, c_ctx[None], jnp.zeros((2 * SUBLANES - nb - 1, d), F32)], axis=0)
    mod = _mod_call(cc, w_mod[lyr], b_mod[lyr])
    sh1, sc1, g1, sh2, sc2, g2 = [m[:nb, None, :] for m in jnp.split(mod, 6, axis=-1)]
    csh1, csc1 = [jnp.broadcast_to(m[nb:nb + 1, None, :], (nb, 1, d)) for m in jnp.split(mod, 6, axis=-1)[:2]]

    w_in_b = w_in[lyr].astype(BF16)
    b_in_r = b_in[lyr].reshape(1, -1)
    tables = _rope_tables(n)
    gy, xr_t, q, k, v, sgr, sgn = _inproj_call(
        x, sh1, sc1, w_in_b, b_in_r, tables,
        ("gelu", "tok_major", "rope_q", "rope_k", "bf16", "sigmoid", "sigmoid"), tuple(range(7)), "in_proj_latent",
        pair_major=(2,), col_blocked=(3, 4))
    xrc_t, kc, vc = _inproj_call(
        ctx, csh1, csc1, w_in_b, b_in_r, tables, ("tok_major", "bf16", "bf16"), (1, 3, 4), "in_proj_context",
        pair_major=(1, 2))

    gw = [_lru_gate_weights(lru_wa[lyr, dr], lru_ba[lyr, dr], lru_wi[lyr, dr], lru_bi[lyr, dr], lru_lambda[lyr, dr])
          for dr in range(2)]
    zeros_h = jnp.zeros((nb, d), F32)
    conv = (conv_w[lyr], conv_b[lyr])
    _, xcc_t, hc_f = _lru_call(xrc_t, *gw[0], zeros_h, conv=conv, reverse=False, name="rglru_ctx_fwd")
    _, hc_b = _lru_call(xcc_t, *gw[1], zeros_h, hf=xcc_t, gy=jnp.zeros((nb, l, d), BF16), reverse=True,
                        name="rglru_ctx_bwd")
    hf_t, xc_t, _ = _lru_call(xr_t, *gw[0], hc_f, conv=conv, reverse=False, name="rglru_fwd")
    y_rnn, _ = _lru_call(xc_t, *gw[1], hc_b, hf=hf_t, gy=gy, reverse=True, name="rglru_bwd")

    y_na = _na_call(q, k, v, kc, vc, _na_bias_table(na_rpb[lyr]))

    x1, xn2, lg_tiles = _mix_call(y_rnn, y_na, sgr, sgn, x, g1, sh2, sc2,
                                  w_proj_rnn[lyr].astype(BF16), w_proj_na[lyr].astype(BF16),
                                  w_out[lyr].astype(BF16), w_router[lyr].T.astype(BF16))
    n_e = w_router.shape[2]
    lg_t = lg_tiles.reshape(n // TQ, n_e, nb, TQ).transpose(2, 1, 0, 3).reshape(nb, n_e, n)

    cap = EC_CAPACITY * n // N_EXPERTS
    n_chunk = n // ROUTE_CHUNK
    assert cap % SLOT_ALIGN == 0 and cap >= ROUTE_WIN and n % ROUTE_CHUNK == 0
    rank_t, aff_t, cnt = _route_call(lg_t, cap)
    cnt_flat = cnt[:, :, :n_chunk + 1].reshape(-1)
    xe, ge = _dispatch_call(cnt_flat, xn2, rank_t, aff_t, cap)
    ye = _ffn_call(xe, ge, w_exp_gate[lyr], w_exp_up[lyr], w_exp_down[lyr])
    return _combine_call(cnt_flat, ye, rank_t, x1, g2, final_norm, cap)
```

```python
import functools

import numpy as np
import jax
import jax.numpy as jnp
from jax import lax
from jax.experimental import pallas as pl
from jax.experimental.pallas import tpu as pltpu

F32 = jnp.float32
BF16 = jnp.bfloat16

GRID_W = 64
N_HEADS = 16
HEAD_DIM = 64
N_LRU_BLOCKS = 16
CONV_W = 4
LRU_C = 8.0
WIN_R = 8
WIN_C = 16
ROPE_BASE = 10000.0
N_EXPERTS = 16
EC_CAPACITY = 2
EPS = 1e-6
NEG_INF = -1e30

LANES = 128
SUBLANES = 8
VMEM_LIMIT = 56 * 1024 * 1024

TQ = 128
LRU_SLAB = 256
LRU_TN = 128
FFN_TM = 2048
FFN_TF = 512
FFN_TN = 256
NA_CHUNK = 2
ROUTE_CHUNK = 256
ROUTE_SUB = 2
ROUTE_WIN = 64
SLOT_ALIGN = 16
MIN_NORMAL_BITS = 0x00800000
LOG2_E = 1.4426950408889634


def _cparams(sem):
    return pltpu.CompilerParams(dimension_semantics=sem, vmem_limit_bytes=VMEM_LIMIT)


def _sigmoid(x):
    return 0.5 * jnp.tanh(0.5 * x) + 0.5


def _rms(x):
    return x * lax.rsqrt(jnp.mean(x * x, axis=-1, keepdims=True) + EPS)


def _mod_kernel(c_ref, w_ref, b_ref, o_ref):
    c = c_ref[...]
    s = c * jax.nn.sigmoid(c)
    o_ref[...] = jnp.dot(s, w_ref[...], preferred_element_type=F32) + b_ref[...]


def _mod_call(cc, w_mod, b_mod):
    rows, d = cc.shape
    n_out = w_mod.shape[1]
    tn = 1024
    return pl.pallas_call(
        _mod_kernel,
        grid=(n_out // tn,),
        in_specs=[
            pl.BlockSpec((rows, d), lambda j: (0, 0)),
            pl.BlockSpec((d, tn), lambda j: (0, j)),
            pl.BlockSpec((1, tn), lambda j: (0, j)),
        ],
        out_specs=pl.BlockSpec((rows, tn), lambda j: (0, j)),
        out_shape=jax.ShapeDtypeStruct((rows, n_out), F32),
        compiler_params=_cparams(("arbitrary",)),
        name="adaln_mod",
    )(cc, w_mod, b_mod.reshape(1, n_out))


def _rope(z, cos, sa, sb):
    outs = []
    for g in range(z.shape[1] // LANES):
        zg = z[:, g * LANES:(g + 1) * LANES]
        outs.append(zg * cos + pltpu.roll(zg, 16, 1) * sa + pltpu.roll(zg, LANES - 16, 1) * sb)
    return jnp.concatenate(outs, axis=1)


def _inproj_kernel(*refs, kinds, groups):
    n_out = len(kinds)
    x_ref, sh_ref, sc_ref, w_ref, b_ref, cos_ref, sa_ref, sb_ref = refs[:8]
    out_refs = refs[8:8 + n_out]
    nb, tq, d = x_ref.shape
    rows = nb * tq
    xn = _rms(x_ref[...]) * (1.0 + sc_ref[...]) + sh_ref[...]
    xb = xn.reshape(rows, d).astype(BF16)

    def tiled(t_ref):
        return jnp.broadcast_to(t_ref[...][None], (nb, tq, LANES)).reshape(rows, LANES)

    def store_tokens(o_ref, val):
        if o_ref.ndim == 3:
            o_ref[...] = val.reshape(nb, tq, d)
            return
        for p in range(d // LANES):
            vp = val[:, p * LANES:(p + 1) * LANES]
            if o_ref.ndim == 4:
                o_ref[:, p] = vp.reshape(nb, tq, LANES)
            else:
                n_cb = o_ref.shape[2]
                v5 = vp.reshape(nb, tq // GRID_W, n_cb, WIN_C, LANES)
                for gr in range(tq // GRID_W):
                    o_ref[:, p, :, gr * WIN_C:(gr + 1) * WIN_C, :] = v5[:, gr]

    for g, kind, o_ref in zip(groups, kinds, out_refs):
        z = jnp.dot(xb, w_ref[:, g * d:(g + 1) * d], preferred_element_type=F32) + b_ref[:, g * d:(g + 1) * d]
        if kind == "gelu":
            o_ref[...] = jax.nn.gelu(z, approximate=True).astype(BF16).reshape(nb, tq, d)
        elif kind == "tok_major":
            o_ref[...] = jnp.swapaxes(z.reshape(nb, tq, d), 0, 1).reshape(rows, d).astype(BF16)
        elif kind == "rope_q":
            r = _rope(z * (HEAD_DIM ** -0.5), tiled(cos_ref), tiled(sa_ref), tiled(sb_ref))
            store_tokens(o_ref, r.astype(BF16))
        elif kind == "rope_k":
            r = _rope(z, tiled(cos_ref), tiled(sa_ref), tiled(sb_ref))
            store_tokens(o_ref, r.astype(BF16))
        elif kind == "bf16":
            store_tokens(o_ref, z.astype(BF16))
        elif kind == "sigmoid":
            o_ref[...] = _sigmoid(z).astype(BF16).reshape(nb, tq, d)
        else:
            raise ValueError(kind)


def _inproj_call(x, sh, sc, w, b, tables, kinds, groups, name, pair_major=(), col_blocked=()):
    nb, n, d = x.shape
    assert len(groups) == len(kinds) and w.shape[0] == d and max(groups) * d < w.shape[1]
    cos, sa, sb = tables
    bat = lambda i: (0, i, 0)
    const3 = lambda i: (0, 0, 0)
    const2 = lambda i: (0, 0)
    out_shapes, out_specs = [], []
    for kind in kinds:
        if kind == "tok_major":
            out_shapes.append(jax.ShapeDtypeStruct((n * nb, d), BF16))
            out_specs.append(pl.BlockSpec((TQ * nb, d), lambda i: (i, 0)))
        elif len(out_shapes) in col_blocked:
            assert TQ % GRID_W == 0
            n_cb = GRID_W // WIN_C
            out_shapes.append(jax.ShapeDtypeStruct((nb, d // LANES, n_cb, n // n_cb, LANES), BF16))
            out_specs.append(pl.BlockSpec((nb, d // LANES, n_cb, TQ // n_cb, LANES), lambda i: (0, 0, 0, i, 0)))
        elif len(out_shapes) in pair_major:
            out_shapes.append(jax.ShapeDtypeStruct((nb, d // LANES, n, LANES), BF16))
            out_specs.append(pl.BlockSpec((nb, d // LANES, TQ, LANES), lambda i: (0, 0, i, 0)))
        else:
            out_shapes.append(jax.ShapeDtypeStruct((nb, n, d), BF16))
            out_specs.append(pl.BlockSpec((nb, TQ, d), bat))
    rows_per_table = cos.shape[0] // TQ
    return pl.pallas_call(
        functools.partial(_inproj_kernel, kinds=kinds, groups=groups),
        grid=(n // TQ,),
        in_specs=[
            pl.BlockSpec((nb, TQ, d), bat),
            pl.BlockSpec(sh.shape, const3),
            pl.BlockSpec(sc.shape, const3),
            pl.BlockSpec(w.shape, const2, pipeline_mode=pl.Buffered(1)),
            pl.BlockSpec(b.shape, const2),
            pl.BlockSpec((TQ, LANES), lambda i: (i % rows_per_table, 0)),
            pl.BlockSpec((TQ, LANES), lambda i: (i % rows_per_table, 0)),
            pl.BlockSpec((TQ, LANES), lambda i: (i % rows_per_table, 0)),
        ],
        out_specs=out_specs,
        out_shape=out_shapes,
        compiler_params=_cparams(("arbitrary",)),
        name=name,
    )(x, sh, sc, w, b, cos, sa, sb)


def _rope_tables(n):
    t = np.arange(n)
    row = (t // GRID_W).astype(np.float32)
    col = (t % GRID_W).astype(np.float32)
    n_freq = HEAD_DIM // 4
    inv = (np.float32(ROPE_BASE) ** (-np.arange(n_freq, dtype=np.float32) / np.float32(n_freq))).astype(np.float32)
    ang_r = (row[:, None] * inv).astype(np.float32)
    ang_c = (col[:, None] * inv).astype(np.float32)
    zero = np.zeros_like(ang_r)
    cos_h = np.concatenate([np.cos(ang_r), np.cos(ang_r), np.cos(ang_c), np.cos(ang_c)], axis=1)
    sa_h = np.concatenate([zero, np.sin(ang_r), zero, np.sin(ang_c)], axis=1)
    sb_h = np.concatenate([-np.sin(ang_r), zero, -np.sin(ang_c), zero], axis=1)
    rep = LANES // HEAD_DIM
    return tuple(jnp.asarray(np.tile(a.astype(np.float32), (1, rep))) for a in (cos_h, sa_h, sb_h))


def _lru_kernel(*refs, tn, nt, reverse, fuse_out):
    if fuse_out:
        (xc_ref, wg_ref, bg_ref, lam_ref, h0_ref, hf_ref, gy_ref, out_ref, hfin_ref, a_s, b_s, hcar_s) = refs
    else:
        (xp_ref, xin_ref, xn_ref, cw_ref, cb_ref, wg_ref, bg_ref, lam_ref, h0_ref,
         out_ref, xc_ref, hfin_ref, a_s, b_s, hcar_s) = refs
    i = pl.program_id(0)
    ti = (nt - 1 - i) if reverse else i
    nb, d = hcar_s.shape

    @pl.when(i == 0)
    def _():
        hcar_s[...] = h0_ref[...]

    def tokens(ref):
        return ref[...].astype(F32).reshape(ref.shape[0] // nb, nb, d)

    if fuse_out:
        xc = tokens(xc_ref)
    else:
        x = tokens(xin_ref)
        xp = jnp.where(ti == 0, 0.0, tokens(xp_ref))
        xnx = jnp.where(ti == nt - 1, 0.0, tokens(xn_ref)[:1])
        ext = jnp.concatenate([xp, x, xnx], axis=0)
        xc = cb_ref[...]
        for k in range(CONV_W):
            xc = xc + ext[k:k + tn] * cw_ref[k]
        xc_ref[...] = xc.reshape(tn * nb, d).astype(BF16)
    rows = tn * nb
    xc2 = xc.reshape(rows, d)
    for s in range(d // LRU_SLAB):
        sl = slice(s * LRU_SLAB, (s + 1) * LRU_SLAB)
        xs = xc2[:, sl]
        half_gates = jnp.dot(xs.astype(BF16), wg_ref[s], preferred_element_type=F32) + bg_ref[s]
        t_r = jnp.tanh(half_gates[:, :LRU_SLAB])
        ig = 0.5 * jnp.tanh(half_gates[:, LRU_SLAB:]) + 0.5
        z = -lam_ref[s]
        softplus = jnp.maximum(z, 0.0) + jnp.log1p(jnp.exp(-jnp.abs(z)))
        k = (-0.5 * LRU_C * LOG2_E) * softplus
        a = jnp.exp2(t_r * k + k)
        b = jnp.sqrt(1.0 - a * a) * (ig * xs)
        a_s[:, :, sl] = a.reshape(tn, nb, LRU_SLAB)
        b_s[:, :, sl] = b.reshape(tn, nb, LRU_SLAB)

    unroll = 8

    def step(j, h):
        for k in range(unroll):
            jj = j * unroll + k
            t = (tn - 1 - jj) if reverse else jj
            h = a_s[t] * h + b_s[t]
            a_s[t] = h
        return h

    h = lax.fori_loop(0, tn // unroll, step, hcar_s[...])
    hcar_s[...] = h
    hfin_ref[...] = h
    if fuse_out:
        hs = jnp.swapaxes(tokens(hf_ref) + a_s[...], 0, 1)
        out_ref[...] = (hs * gy_ref[...].astype(F32)).astype(BF16)
    else:
        out_ref[...] = a_s[...].reshape(tn * nb, d).astype(BF16)


def _lru_call(x_t, wg, bg, lam, h0, conv=None, hf=None, gy=None, *, reverse, name):
    nb, d = h0.shape
    n = x_t.shape[0] // nb
    tn = LRU_TN
    nt = n // tn
    halo = 2
    fuse_out = conv is None
    tile = (lambda i: nt - 1 - i) if reverse else (lambda i: i)
    n_slab = d // LRU_SLAB
    tok_spec = pl.BlockSpec((tn * nb, d), lambda i: (tile(i), 0))
    gate_specs = [
        pl.BlockSpec((n_slab, LRU_SLAB, 2 * LRU_SLAB), lambda i: (0, 0, 0)),
        pl.BlockSpec((n_slab, 1, 2 * LRU_SLAB), lambda i: (0, 0, 0)),
        pl.BlockSpec((n_slab, 1, LRU_SLAB), lambda i: (0, 0, 0)),
        pl.BlockSpec((nb, d), lambda i: (0, 0)),
    ]
    fin_spec = pl.BlockSpec((nb, d), lambda i: (0, 0))
    fin_shape = jax.ShapeDtypeStruct((nb, d), F32)
    tok_shape = jax.ShapeDtypeStruct((n * nb, d), BF16)
    if fuse_out:
        in_specs = [tok_spec] + gate_specs + [tok_spec, pl.BlockSpec((nb, tn, d), lambda i: (0, tile(i), 0))]
        args = [x_t, wg, bg, lam, h0, hf, gy]
        out_specs = [pl.BlockSpec((nb, tn, d), lambda i: (0, tile(i), 0)), fin_spec]
        out_shape = [jax.ShapeDtypeStruct((nb, n, d), BF16), fin_shape]
    else:
        conv_w, conv_b = conv
        in_specs = [
            pl.BlockSpec((halo * nb, d), lambda i: (jnp.maximum(tile(i) * (tn // halo) - 1, 0), 0)),
            tok_spec,
            pl.BlockSpec((halo * nb, d), lambda i: (jnp.minimum((tile(i) + 1) * (tn // halo), n // halo - 1), 0)),
            pl.BlockSpec((CONV_W, 1, d), lambda i: (0, 0, 0)),
            pl.BlockSpec((1, d), lambda i: (0, 0)),
        ] + gate_specs
        args = [x_t, x_t, x_t, conv_w.reshape(CONV_W, 1, d), conv_b.reshape(1, d), wg, bg, lam, h0]
        out_specs = [tok_spec, tok_spec, fin_spec]
        out_shape = [tok_shape, tok_shape, fin_shape]
    return pl.pallas_call(
        functools.partial(_lru_kernel, tn=tn, nt=nt, reverse=reverse, fuse_out=fuse_out),
        grid=(nt,),
        in_specs=in_specs,
        out_specs=out_specs,
        out_shape=out_shape,
        scratch_shapes=[
            pltpu.VMEM((tn, nb, d), F32),
            pltpu.VMEM((tn, nb, d), F32),
            pltpu.VMEM((nb, d), F32),
        ],
        compiler_params=_cparams(("arbitrary",)),
        name=name,
    )(*args)


def _lru_gate_weights(wa, ba, wi, bi, lam):
    n_blk, blk, _ = wa.shape
    d = n_blk * blk
    n_slab = d // LRU_SLAB
    per = LRU_SLAB // blk
    eye = jnp.eye(per, dtype=F32)

    def dense_slabs(w):
        return jnp.einsum("saij,ac->saicj", w.reshape(n_slab, per, blk, blk), eye).reshape(n_slab, LRU_SLAB, LRU_SLAB)

    wg = (0.5 * jnp.concatenate([dense_slabs(wa), dense_slabs(wi)], axis=2).astype(BF16)).astype(BF16)
    bg = 0.5 * jnp.concatenate([ba.reshape(n_slab, 1, LRU_SLAB), bi.reshape(n_slab, 1, LRU_SLAB)], axis=2)
    return wg, bg, lam.reshape(n_slab, 1, LRU_SLAB)


def _na_blocks(g):
    lo = min(max(8 * g - WIN_C // 2, 0) // WIN_C, GRID_W // WIN_C - 2)
    return (lo, lo + 1)


def _na_kernel(q_ref, k_ref, v_ref, kc_ref, vc_ref, tab_ref, o_ref, s_s, p_s, pc_s, inv_s, qs_s, sc_s, *, rows):
    win = WIN_R * GRID_W
    ch = NA_CHUNK
    n_chunk = rows // ch
    qh = 2 * GRID_W
    n_cb = GRID_W // WIN_C
    blk = WIN_R * WIN_C
    grp = SUBLANES
    lane = lax.broadcasted_iota(jnp.int32, (GRID_W, LANES), 1)
    first_head = lane < HEAD_DIM
    nt_dims = (((1,), (1,)), ((), ()))

    def row_start(r):
        return pl.multiple_of(r * GRID_W, GRID_W)

    def window(ref, rs):
        run = pl.ds(pl.multiple_of(rs * WIN_C, WIN_C), blk)
        return jnp.concatenate([ref[0, 0, cb, run, :] for cb in range(n_cb)], axis=0)

    def groups():
        for h in range(2):
            for g in range(GRID_W // grp):
                yield h, g * grp, _na_blocks(g)

    p_s[...] = jnp.zeros_like(p_s)

    def stack_heads(r, carry):
        q = q_ref[0, 0, pl.ds(row_start(r), GRID_W), :]
        zero = jnp.zeros_like(q)
        qs_s[r] = jnp.concatenate([jnp.where(first_head, q, zero), jnp.where(first_head, zero, q)], axis=0)
        return carry

    lax.fori_loop(0, rows, stack_heads, 0)
    sc_s[...] = lax.dot_general(qs_s[...].reshape(rows * qh, LANES), kc_ref[0, 0], nt_dims,
                                preferred_element_type=F32)

    def stage1(c, slot):
        for u in range(ch):
            r = c * ch + u
            rs = jnp.clip(r - WIN_R // 2, 0, rows - WIN_R)
            s = lax.dot_general(qs_s[r], window(k_ref, rs), nt_dims, preferred_element_type=F32)
            d0 = rs - r + (WIN_R - 1)
            for h, q0, cbs in groups():
                for cb in cbs:
                    rw = slice(h * GRID_W + q0, h * GRID_W + q0 + grp)
                    ln = slice(cb * blk, (cb + 1) * blk)
                    s_s[slot, u, rw, ln] = s[rw, ln] + tab_ref[0, h, d0, q0:q0 + grp, ln]

    def stage2(c, slot):
        for u in range(ch):
            base = pl.multiple_of((c * ch + u) * qh, qh)
            for r0 in range(0, qh, 2 * grp):
                parts = {}
                pcs, invs = [], []
                for half in range(2):
                    ra = r0 + half * grp
                    cbs = _na_blocks((ra % GRID_W) // grp)
                    sl = [s_s[slot, u, ra:ra + grp, cb * blk:(cb + 1) * blk] for cb in cbs]
                    sc = sc_s[pl.ds(base + ra, grp), :]
                    m = jnp.maximum(jnp.max(jnp.maximum(sl[0], sl[1]), axis=1, keepdims=True),
                                    jnp.max(sc, axis=1, keepdims=True))
                    ps = [jnp.exp(x - m) for x in sl]
                    pc = jnp.exp(sc - m)
                    den = jnp.sum(ps[0] + ps[1], axis=1, keepdims=True) + jnp.sum(pc, axis=1, keepdims=True)
                    for cb, pb in zip(cbs, ps):
                        parts[(half, cb)] = pb
                    pcs.append(pc)
                    invs.append(jnp.broadcast_to(1.0 / den, (grp, LANES)))
                zero = jnp.zeros((grp, blk), F32)
                for cb in sorted({cb for _, cb in parts}):
                    both = jnp.concatenate([parts.get((0, cb), zero), parts.get((1, cb), zero)], axis=0)
                    p_s[slot, u, r0:r0 + 2 * grp, cb * blk:(cb + 1) * blk] = both.astype(BF16)
                pc_s[slot, u * qh + r0:u * qh + r0 + 2 * grp, :] = jnp.concatenate(pcs, axis=0).astype(BF16)
                inv_s[slot, u, r0:r0 + 2 * grp, :] = jnp.concatenate(invs, axis=0)

    def stage3(c, slot):
        oc = jnp.dot(pc_s[slot], vc_ref[0, 0], preferred_element_type=F32)
        for u in range(ch):
            r = c * ch + u
            rs = jnp.clip(r - WIN_R // 2, 0, rows - WIN_R)
            o = (jnp.dot(p_s[slot, u], window(v_ref, rs), preferred_element_type=F32)
                 + oc[u * qh:(u + 1) * qh]) * inv_s[slot, u]
            out = jnp.where(first_head, o[:GRID_W], o[GRID_W:])
            o_ref[0, 0, pl.ds(row_start(r), GRID_W), :] = out.astype(BF16)

    stage1(0, 0)
    stage2(0, 0)
    stage1(1, 1)

    def body(j, carry):
        slot = j % 2
        stage3(j - 2, slot)
        stage2(j - 1, 1 - slot)
        stage1(j, slot)
        return carry

    lax.fori_loop(2, n_chunk, body, 0)
    last = n_chunk % 2
    stage3(n_chunk - 2, last)
    stage2(n_chunk - 1, 1 - last)
    stage3(n_chunk - 1, 1 - last)


def _na_call(q, k, v, kc, vc, tab):
    nb, n_pair, n, _ = q.shape
    l = kc.shape[2]
    rows = n // GRID_W
    n_cb = GRID_W // WIN_C
    assert k.shape == v.shape == (nb, n_pair, n_cb, n // n_cb, LANES)
    win = WIN_R * GRID_W
    qh = 2 * GRID_W
    assert rows % NA_CHUNK == 0 and rows // NA_CHUNK >= 2
    tok_spec = pl.BlockSpec((1, 1, n, LANES), lambda b, p: (b, p, 0, 0))
    kv_spec = pl.BlockSpec((1, 1, n_cb, n // n_cb, LANES), lambda b, p: (b, p, 0, 0, 0))
    ctx_spec = pl.BlockSpec((1, 1, l, LANES), lambda b, p: (b, p, 0, 0))
    return pl.pallas_call(
        functools.partial(_na_kernel, rows=rows),
        grid=(nb, n_pair),
        in_specs=[tok_spec, kv_spec, kv_spec, ctx_spec, ctx_spec,
                  pl.BlockSpec((1,) + tab.shape[1:], lambda b, p: (p, 0, 0, 0, 0))],
        out_specs=tok_spec,
        out_shape=jax.ShapeDtypeStruct((nb, n_pair, n, LANES), BF16),
        scratch_shapes=[
            pltpu.VMEM((2, NA_CHUNK, qh, win), F32),
            pltpu.VMEM((2, NA_CHUNK, qh, win), BF16),
            pltpu.VMEM((2, NA_CHUNK * qh, l), BF16),
            pltpu.VMEM((2, NA_CHUNK, qh, LANES), F32),
            pltpu.VMEM((rows, qh, LANES), BF16),
            pltpu.VMEM((rows * qh, l), F32),
        ],
        compiler_params=_cparams(("arbitrary", "arbitrary")),
        name="neighbourhood_attention",
    )(q, k, v, kc, vc, tab)


def _na_bias_table(rpb):
    qc = np.arange(GRID_W)
    kc = np.arange(GRID_W)
    n_cb = GRID_W // WIN_C
    cstart = np.clip(qc - WIN_C // 2, 0, GRID_W - WIN_C)
    ok = (kc[None, :] >= cstart[:, None]) & (kc[None, :] < cstart[:, None] + WIN_C)
    dc = np.clip(kc[None, :] - qc[:, None], -(WIN_C - 1), WIN_C - 1) + WIN_C - 1
    expand = (dc[None] == np.arange(2 * WIN_C - 1)[:, None, None]).astype(np.float32)
    expand = expand.reshape(2 * WIN_C - 1, GRID_W, n_cb, WIN_C)
    mask = np.where(ok, 0.0, NEG_INF).astype(np.float32).reshape(GRID_W, n_cb, 1, WIN_C)
    h = rpb.shape[0]
    rows = jnp.stack([rpb[:, d0:d0 + WIN_R].astype(F32) for d0 in range(WIN_R)], axis=1)
    tab = jnp.einsum("hdjx,xqbc->hdqbjc", rows, jnp.asarray(expand), precision=lax.Precision.HIGHEST)
    tab = tab + jnp.asarray(mask)
    return tab.reshape(h // 2, 2, WIN_R, GRID_W, WIN_R * GRID_W)


def _mix_kernel(yr_ref, yn_ref, gr_ref, gn_ref, x_ref, g1_ref, sh2_ref, sc2_ref,
                wpr_ref, wpn_ref, wo_ref, wr_ref, x1_ref, xn2_ref, lg_ref):
    nb, tq, d = x_ref.shape
    rows = nb * tq
    pr = jnp.dot(yr_ref[...].reshape(rows, d), wpr_ref[...], preferred_element_type=F32)
    yn = jnp.concatenate([yn_ref[:, p].reshape(rows, LANES) for p in range(d // LANES)], axis=1)
    pn = jnp.dot(yn, wpn_ref[...], preferred_element_type=F32)
    mix = gr_ref[...].reshape(rows, d).astype(F32) * pr + gn_ref[...].reshape(rows, d).astype(F32) * pn
    o = jnp.dot(mix.astype(BF16), wo_ref[...], preferred_element_type=F32)
    x1 = x_ref[...] + g1_ref[...] * o.reshape(nb, tq, d)
    x1_ref[...] = x1
    xb = (_rms(x1) * (1.0 + sc2_ref[...]) + sh2_ref[...]).astype(BF16)
    xn2_ref[...] = xb
    lg_ref[0] = lax.dot_general(wr_ref[...], xb.reshape(rows, d), (((1,), (1,)), ((), ())),
                                preferred_element_type=F32)


def _mix_call(yr, yn, gr, gn, x, g1, sh2, sc2, wpr, wpn, wo, wr):
    nb, n, d = x.shape
    bat = lambda i: (0, i, 0)
    const3 = lambda i: (0, 0, 0)
    const2 = lambda i: (0, 0)
    act = pl.BlockSpec((nb, TQ, d), bat)
    vec = pl.BlockSpec((nb, 1, d), const3)
    wsp = lambda w: pl.BlockSpec(w.shape, const2, pipeline_mode=pl.Buffered(1))
    return pl.pallas_call(
        _mix_kernel,
        grid=(n // TQ,),
        in_specs=[act, pl.BlockSpec((nb, d // LANES, TQ, LANES), lambda i: (0, 0, i, 0)), act, act, act,
                  vec, vec, vec, wsp(wpr), wsp(wpn), wsp(wo), wsp(wr)],
        out_specs=[act, act, pl.BlockSpec((1, wr.shape[0], nb * TQ), lambda i: (i, 0, 0))],
        out_shape=[
            jax.ShapeDtypeStruct((nb, n, d), F32),
            jax.ShapeDtypeStruct((nb, n, d), BF16),
            jax.ShapeDtypeStruct((n // TQ, wr.shape[0], nb * TQ), F32),
        ],
        compiler_params=_cparams(("arbitrary",)),
        name="merge_out_norm_router",
    )(yr, yn, gr, gn, x, g1, sh2, sc2, wpr, wpn, wo, wr)


def _route_kernel(lg_ref, rank_ref, aff_ref, cnt_ref, *, cap):
    lg = lg_ref[0]
    n_e, n = lg.shape
    ex = jnp.exp(lg - jnp.max(lg, axis=0, keepdims=True))
    aff = ex / jnp.sum(ex, axis=0, keepdims=True)
    aff_ref[0] = aff

    def as_float(bits):
        return lax.bitcast_convert_type(bits, F32)

    thr = jnp.zeros((n_e, 1), jnp.int32)
    for bit in range(30, -1, -1):
        cand = thr | (1 << bit)
        cnt = jnp.sum((aff >= as_float(cand)).astype(F32), axis=1, keepdims=True)
        thr = jnp.where(cnt >= cap, cand, thr)
    thr = jnp.where(thr < MIN_NORMAL_BITS, 0, thr)
    gt = aff >= as_float(jnp.where(thr == 0, MIN_NORMAL_BITS, thr + 1))
    eq = (aff >= as_float(thr)) & jnp.logical_not(gt)
    need = cap - jnp.sum(gt.astype(F32), axis=1, keepdims=True)

    blk = ROUTE_CHUNK
    row = lax.broadcasted_iota(jnp.int32, (blk, blk), 0)
    col = lax.broadcasted_iota(jnp.int32, (blk, blk), 1)
    upper = (row <= col).astype(BF16)

    def cumsum_tokens(mask):
        off = jnp.zeros((n_e, 1), F32)
        outs, offs = [], []
        for c in range(n // blk):
            x = mask[:, c * blk:(c + 1) * blk].astype(BF16)
            cs = jnp.dot(x, upper, preferred_element_type=F32) + off
            offs.append(off)
            outs.append(cs)
            off = cs[:, blk - 1:blk]
        offs.append(off)
        return jnp.concatenate(outs, axis=1), offs

    cum_eq, _ = cumsum_tokens(eq)
    sel = gt | (eq & ((cum_eq - eq.astype(F32)) < need))
    cum_sel, offs = cumsum_tokens(sel)
    rank_ref[0] = jnp.where(sel, cum_sel - 1.0, -1.0).astype(jnp.int32)
    pad = jnp.zeros((n_e, LANES - len(offs)), F32)
    cnt_ref[0] = jnp.concatenate(offs + [pad], axis=1).astype(jnp.int32)


def _route_call(lg_t, cap):
    nb, n_e, n = lg_t.shape
    blk3 = lambda b: (b, 0, 0)
    return pl.pallas_call(
        functools.partial(_route_kernel, cap=cap),
        grid=(nb,),
        in_specs=[pl.BlockSpec((1, n_e, n), blk3)],
        out_specs=[pl.BlockSpec((1, n_e, n), blk3), pl.BlockSpec((1, n_e, n), blk3),
                   pl.BlockSpec((1, n_e, LANES), blk3)],
        out_shape=[jax.ShapeDtypeStruct((nb, n_e, n), jnp.int32), jax.ShapeDtypeStruct((nb, n_e, n), F32),
                   jax.ShapeDtypeStruct((nb, n_e, LANES), jnp.int32)],
        compiler_params=_cparams(("arbitrary",)),
        name="route_select",
    )(lg_t)


def _slot_windows(cnt_ref, b, c, n_e, n_chunk, cap):
    wins = []
    for e in range(n_e):
        base = (b * n_e + e) * (n_chunk + 1) + c
        lo = cnt_ref[base]
        hi = cnt_ref[base + 1]
        lo_al = jnp.minimum((lo // SLOT_ALIGN) * SLOT_ALIGN, cap - ROUTE_WIN)
        extra = jnp.maximum((hi - lo_al + ROUTE_WIN - 1) // ROUTE_WIN - 1, 0)
        wins.append((pl.multiple_of(lo_al, SLOT_ALIGN), extra))
    return wins


def _one_hot_t(rank_row, start, first=None):
    slot = start + lax.broadcasted_iota(jnp.int32, (ROUTE_WIN, rank_row.shape[1]), 0)
    hit = rank_row == slot
    if first is not None:
        hit = hit & (slot >= first)
    return hit


def _spill_window(k, lo_al, cap):
    first = lo_al + k * ROUTE_WIN
    start = pl.multiple_of(jnp.minimum(first, cap - ROUTE_WIN), SLOT_ALIGN)
    return start, first


def _dispatch_kernel(cnt_ref, x_ref, rank_ref, aff_ref, xe_ref, g_ref, *, cap, n_chunk):
    b = pl.program_id(0)
    step = pl.program_id(1)
    n_e = rank_ref.shape[1]

    @pl.when(step == 0)
    def _():
        xe_ref[...] = jnp.zeros_like(xe_ref)
        g_ref[...] = jnp.zeros_like(g_ref)

    for sub in range(ROUTE_SUB):
        toks = slice(sub * ROUTE_CHUNK, (sub + 1) * ROUTE_CHUNK)
        c = step * ROUTE_SUB + sub
        x = x_ref[0, toks, :]
        wins = _slot_windows(cnt_ref, b, c, n_e, n_chunk, cap)
        hots = [_one_hot_t(rank_ref[0, e:e + 1, toks], wins[e][0]) for e in range(n_e)]
        stacked = jnp.concatenate([h.astype(BF16) for h in hots], axis=0)
        rows = jnp.dot(stacked, x, preferred_element_type=F32)
        for e in range(n_e):
            sl = pl.ds(wins[e][0], ROUTE_WIN)
            xe_ref[e, sl, :] = xe_ref[e, sl, :] + rows[e * ROUTE_WIN:(e + 1) * ROUTE_WIN].astype(BF16)
            g_ref[e, sl, :] = g_ref[e, sl, :] + jnp.sum(jnp.where(hots[e], aff_ref[0, e:e + 1, toks], 0.0),
                                                        axis=1, keepdims=True)

        @pl.when(sum(extra for _, extra in wins) > 0)
        def _(wins=wins, toks=toks):
            for e in range(n_e):
                lo_al, extra = wins[e]

                def spill(k, carry, e=e, lo_al=lo_al):
                    start, first = _spill_window(k, lo_al, cap)
                    hot = _one_hot_t(rank_ref[0, e:e + 1, toks], start, first)
                    sl2 = pl.ds(start, ROUTE_WIN)
                    xe_ref[e, sl2, :] = xe_ref[e, sl2, :] + jnp.dot(hot.astype(BF16), x_ref[0, toks, :],
                                                                   preferred_element_type=F32).astype(BF16)
                    g_ref[e, sl2, :] = g_ref[e, sl2, :] + jnp.sum(jnp.where(hot, aff_ref[0, e:e + 1, toks], 0.0),
                                                                  axis=1, keepdims=True)
                    return carry

                lax.fori_loop(1, extra + 1, spill, 0)


def _dispatch_call(cnt_flat, xn2, rank_t, aff_t, cap):
    nb, n, d = xn2.shape
    n_e = rank_t.shape[1]
    n_chunk = n // ROUTE_CHUNK
    grid_spec = pltpu.PrefetchScalarGridSpec(
        num_scalar_prefetch=1,
        grid=(nb, n_chunk // ROUTE_SUB),
        in_specs=[
            pl.BlockSpec((1, ROUTE_SUB * ROUTE_CHUNK, d), lambda b, c, cnt: (b, c, 0)),
            pl.BlockSpec((1, n_e, ROUTE_SUB * ROUTE_CHUNK), lambda b, c, cnt: (b, 0, c)),
            pl.BlockSpec((1, n_e, ROUTE_SUB * ROUTE_CHUNK), lambda b, c, cnt: (b, 0, c)),
        ],
        out_specs=[
            pl.BlockSpec((n_e, cap, d), lambda b, c, cnt: (0, b, 0)),
            pl.BlockSpec((n_e, cap, 1), lambda b, c, cnt: (0, b, 0)),
        ],
    )
    return pl.pallas_call(
        functools.partial(_dispatch_kernel, cap=cap, n_chunk=n_chunk),
        grid_spec=grid_spec,
        out_shape=[jax.ShapeDtypeStruct((n_e, nb * cap, d), BF16), jax.ShapeDtypeStruct((n_e, nb * cap, 1), F32)],
        compiler_params=_cparams(("arbitrary", "arbitrary")),
        name="moe_dispatch",
    )(cnt_flat, xn2, rank_t, aff_t)


def _ffn_kernel(xe_ref, g_ref, wg_ref, wu_ref, wd_ref, o_ref, acc_s):
    f = pl.program_id(2)
    last = pl.num_programs(2) - 1
    d = o_ref.shape[2]
    xe = xe_ref[0]
    h1 = jnp.dot(xe, wg_ref[0].astype(BF16), preferred_element_type=F32)
    h2 = jnp.dot(xe, wu_ref[0].astype(BF16), preferred_element_type=F32)
    hid = (h1 * _sigmoid(h1) * h2).astype(BF16)
    wd = wd_ref[0].astype(BF16)

    def chunks():
        for c in range(d // FFN_TN):
            cols = slice(c * FFN_TN, (c + 1) * FFN_TN)
            yield cols, jnp.dot(hid, wd[:, cols], preferred_element_type=F32)

    @pl.when(f == 0)
    def _():
        for cols, part in chunks():
            acc_s[:, cols] = part

    @pl.when((f > 0) & (f < last))
    def _():
        for cols, part in chunks():
            acc_s[:, cols] += part

    @pl.when(f == last)
    def _():
        for cols, part in chunks():
            o_ref[0, :, cols] = ((acc_s[:, cols] + part) * g_ref[0]).astype(BF16)


def _ffn_call(xe, g, w_gate, w_up, w_down):
    e, m, d = xe.shape
    dff = w_gate.shape[2]
    tm = min(FFN_TM, m)
    assert dff // FFN_TF >= 2 and m % tm == 0 and d % FFN_TN == 0
    return pl.pallas_call(
        _ffn_kernel,
        grid=(e, m // tm, dff // FFN_TF),
        in_specs=[
            pl.BlockSpec((1, tm, d), lambda ei, mi, fi: (ei, mi, 0)),
            pl.BlockSpec((1, tm, 1), lambda ei, mi, fi: (ei, mi, 0)),
            pl.BlockSpec((1, d, FFN_TF), lambda ei, mi, fi: (ei, 0, fi)),
            pl.BlockSpec((1, d, FFN_TF), lambda ei, mi, fi: (ei, 0, fi)),
            pl.BlockSpec((1, FFN_TF, d), lambda ei, mi, fi: (ei, fi, 0)),
        ],
        out_specs=pl.BlockSpec((1, tm, d), lambda ei, mi, fi: (ei, mi, 0)),
        out_shape=jax.ShapeDtypeStruct((e, m, d), BF16),
        scratch_shapes=[pltpu.VMEM((tm, d), F32)],
        compiler_params=_cparams(("arbitrary", "arbitrary", "arbitrary")),
        name="expert_ffn",
    )(xe, g, w_gate, w_up, w_down)


def _combine_kernel(cnt_ref, ye_ref, rank_ref, x1_ref, g2_ref, fn_ref, o_ref, acc_s, *, cap, n_chunk):
    b = pl.program_id(0)
    step = pl.program_id(1)
    n_e = rank_ref.shape[1]
    tn_dims = (((0,), (0,)), ((), ()))

    for sub in range(ROUTE_SUB):
        toks = slice(sub * ROUTE_CHUNK, (sub + 1) * ROUTE_CHUNK)
        c = step * ROUTE_SUB + sub
        wins = _slot_windows(cnt_ref, b, c, n_e, n_chunk, cap)
        hots = [_one_hot_t(rank_ref[0, e:e + 1, toks], wins[e][0]).astype(BF16) for e in range(n_e)]
        ys = [ye_ref[e, pl.ds(wins[e][0], ROUTE_WIN), :] for e in range(n_e)]
        moe = lax.dot_general(jnp.concatenate(hots, axis=0), jnp.concatenate(ys, axis=0), tn_dims,
                              preferred_element_type=F32)

        def finish(m, toks=toks):
            x = x1_ref[0, toks, :] + g2_ref[0] * m
            o_ref[0, toks, :] = _rms(x) * fn_ref[0]

        any_spill = sum(extra for _, extra in wins) > 0

        @pl.when(jnp.logical_not(any_spill))
        def _(moe=moe, finish=finish):
            finish(moe)

        @pl.when(any_spill)
        def _(moe=moe, finish=finish, wins=wins, toks=toks):
            acc_s[...] = moe
            for e in range(n_e):
                lo_al, extra = wins[e]

                def spill(k, carry, e=e, lo_al=lo_al):
                    start, first = _spill_window(k, lo_al, cap)
                    hot = _one_hot_t(rank_ref[0, e:e + 1, toks], start, first).astype(BF16)
                    acc_s[...] += lax.dot_general(hot, ye_ref[e, pl.ds(start, ROUTE_WIN), :], tn_dims,
                                                  preferred_element_type=F32)
                    return carry

                lax.fori_loop(1, extra + 1, spill, 0)
            finish(acc_s[...])


def _combine_call(cnt_flat, ye, rank_t, x1, g2, fnorm, cap):
    nb, n, d = x1.shape
    n_e = rank_t.shape[1]
    n_chunk = n // ROUTE_CHUNK
    grid_spec = pltpu.PrefetchScalarGridSpec(
        num_scalar_prefetch=1,
        grid=(nb, n_chunk // ROUTE_SUB),
        in_specs=[
            pl.BlockSpec((n_e, cap, d), lambda b, c, cnt: (0, b, 0)),
            pl.BlockSpec((1, n_e, ROUTE_SUB * ROUTE_CHUNK), lambda b, c, cnt: (b, 0, c)),
            pl.BlockSpec((1, ROUTE_SUB * ROUTE_CHUNK, d), lambda b, c, cnt: (b, c, 0)),
            pl.BlockSpec((1, 1, d), lambda b, c, cnt: (b, 0, 0)),
            pl.BlockSpec((1, 1, d), lambda b, c, cnt: (0, 0, 0)),
        ],
        out_specs=pl.BlockSpec((1, ROUTE_SUB * ROUTE_CHUNK, d), lambda b, c, cnt: (b, c, 0)),
        scratch_shapes=[pltpu.VMEM((ROUTE_CHUNK, d), F32)],
    )
    return pl.pallas_call(
        functools.partial(_combine_kernel, cap=cap, n_chunk=n_chunk),
        grid_spec=grid_spec,
        out_shape=jax.ShapeDtypeStruct((nb, n, d), F32),
        compiler_params=_cparams(("arbitrary", "arbitrary")),
        name="moe_combine_final_norm",
    )(cnt_flat, ye, rank_t, x1, g2, fnorm.reshape(1, 1, d))


def kernel(x, c, ctx, c_ctx, w_mod, b_mod, w_in, b_in, conv_w, conv_b, lru_wa, lru_ba, lru_wi, lru_bi,
           lru_lambda, na_rpb, w_proj_rnn, w_proj_na, w_out, w_router, w_exp_gate, w_exp_up, w_exp_down,
           final_norm):
    nb, n, d = x.shape
    l = ctx.shape[1]
    assert w_mod.shape[0] == 1, "single-layer problem"
    assert nb == SUBLANES and n % LRU_TN == 0 and l % LRU_TN == 0 and d % LRU_SLAB == 0
    lyr = 0

    cc = jnp.concatenate([c, c_ctx[None], jnp.zeros((2 * SUBLANES - nb - 1, d), F32)], axis=0)
    mod = _mod_call(cc, w_mod[lyr], b_mod[lyr])
    sh1, sc1, g1, sh2, sc2, g2 = [m[:nb, None, :] for m in jnp.split(mod, 6, axis=-1)]
    csh1, csc1 = [jnp.broadcast_to(m[nb:nb + 1, None, :], (nb, 1, d)) for m in jnp.split(mod, 6, axis=-1)[:2]]

    w_in_b = w_in[lyr].astype(BF16)
    b_in_r = b_in[lyr].reshape(1, -1)
    tables = _rope_tables(n)
    gy, xr_t, q, k, v, sgr, sgn = _inproj_call(
        x, sh1, sc1, w_in_b, b_in_r, tables,
        ("gelu", "tok_major", "rope_q", "rope_k", "bf16", "sigmoid", "sigmoid"), tuple(range(7)), "in_proj_latent",
        pair_major=(2,), col_blocked=(3, 4))
    xrc_t, kc, vc = _inproj_call(
        ctx, csh1, csc1, w_in_b, b_in_r, tables, ("tok_major", "bf16", "bf16"), (1, 3, 4), "in_proj_context",
        pair_major=(1, 2))

    gw = [_lru_gate_weights(lru_wa[lyr, dr], lru_ba[lyr, dr], lru_wi[lyr, dr], lru_bi[lyr, dr], lru_lambda[lyr, dr])
          for dr in range(2)]
    zeros_h = jnp.zeros((nb, d), F32)
    conv = (conv_w[lyr], conv_b[lyr])
    _, xcc_t, hc_f = _lru_call(xrc_t, *gw[0], zeros_h, conv=conv, reverse=False, name="rglru_ctx_fwd")
    _, hc_b = _lru_call(xcc_t, *gw[1], zeros_h, hf=xcc_t, gy=jnp.zeros((nb, l, d), BF16), reverse=True,
                        name="rglru_ctx_bwd")
    hf_t, xc_t, _ = _lru_call(xr_t, *gw[0], hc_f, conv=conv, reverse=False, name="rglru_fwd")
    y_rnn, _ = _lru_call(xc_t, *gw[1], hc_b, hf=hf_t, gy=gy, reverse=True, name="rglru_bwd")

    y_na = _na_call(q, k, v, kc, vc, _na_bias_table(na_rpb[lyr]))

    x1, xn2, lg_tiles = _mix_call(y_rnn, y_na, sgr, sgn, x, g1, sh2, sc2,
                                  w_proj_rnn[lyr].astype(BF16), w_proj_na[lyr].astype(BF16),
                                  w_out[lyr].astype(BF16), w_router[lyr].T.astype(BF16))
    n_e = w_router.shape[2]
    lg_t = lg_tiles.reshape(n // TQ, n_e, nb, TQ).transpose(2, 1, 0, 3).reshape(nb, n_e, n)

    cap = EC_CAPACITY * n // N_EXPERTS
    n_chunk = n // ROUTE_CHUNK
    assert cap % SLOT_ALIGN == 0 and cap >= ROUTE_WIN and n % (ROUTE_SUB * ROUTE_CHUNK) == 0
    rank_t, aff_t, cnt = _route_call(lg_t, cap)
    cnt_flat = cnt[:, :, :n_chunk + 1].reshape(-1)
    xe, ge = _dispatch_call(cnt_flat, xn2, rank_t, aff_t, cap)
    ye = _ffn_call(xe, ge, w_exp_gate[lyr], w_exp_up[lyr], w_exp_down[lyr])
    return _combine_call(cnt_flat, ye, rank_t, x1, g2, final_norm, cap)
```

```python
import functools

import numpy as np
import jax
import jax.numpy as jnp
from jax import lax
from jax.experimental import pallas as pl
from jax.experimental.pallas import tpu as pltpu

F32 = jnp.float32
BF16 = jnp.bfloat16

GRID_W = 64
N_HEADS = 16
HEAD_DIM = 64
N_LRU_BLOCKS = 16
CONV_W = 4
LRU_C = 8.0
WIN_R = 8
WIN_C = 16
ROPE_BASE = 10000.0
N_EXPERTS = 16
EC_CAPACITY = 2
EPS = 1e-6
NEG_INF = -1e30

LANES = 128
SUBLANES = 8
VMEM_LIMIT = 56 * 1024 * 1024

TQ = 128
LRU_SLAB = 256
LRU_TN = 128
FFN_TM = 2048
FFN_TF = 512
FFN_TN = 256
NA_CHUNK = 2
ROUTE_CHUNK = 256
ROUTE_SUB = 4
ROUTE_WIN = 64
SLOT_ALIGN = 16
MIN_NORMAL_BITS = 0x00800000
LOG2_E = 1.4426950408889634


def _cparams(sem):
    return pltpu.CompilerParams(dimension_semantics=sem, vmem_limit_bytes=VMEM_LIMIT)


def _sigmoid(x):
    return 0.5 * jnp.tanh(0.5 * x) + 0.5


def _rms(x):
    return x * lax.rsqrt(jnp.mean(x * x, axis=-1, keepdims=True) + EPS)


def _mod_kernel(c_ref, w_ref, b_ref, o_ref):
    c = c_ref[...]
    s = c * jax.nn.sigmoid(c)
    o_ref[...] = jnp.dot(s, w_ref[...], preferred_element_type=F32) + b_ref[...]


def _mod_call(cc, w_mod, b_mod):
    rows, d = cc.shape
    n_out = w_mod.shape[1]
    tn = 1024
    return pl.pallas_call(
        _mod_kernel,
        grid=(n_out // tn,),
        in_specs=[
            pl.BlockSpec((rows, d), lambda j: (0, 0)),
            pl.BlockSpec((d, tn), lambda j: (0, j)),
            pl.BlockSpec((1, tn), lambda j: (0, j)),
        ],
        out_specs=pl.BlockSpec((rows, tn), lambda j: (0, j)),
        out_shape=jax.ShapeDtypeStruct((rows, n_out), F32),
        compiler_params=_cparams(("arbitrary",)),
        name="adaln_mod",
    )(cc, w_mod, b_mod.reshape(1, n_out))


def _rope(z, cos, sa, sb):
    outs = []
    for g in range(z.shape[1] // LANES):
        zg = z[:, g * LANES:(g + 1) * LANES]
        outs.append(zg * cos + pltpu.roll(zg, 16, 1) * sa + pltpu.roll(zg, LANES - 16, 1) * sb)
    return jnp.concatenate(outs, axis=1)


def _inproj_kernel(*refs, kinds, groups):
    n_out = len(kinds)
    x_ref, sh_ref, sc_ref, w_ref, b_ref, cos_ref, sa_ref, sb_ref = refs[:8]
    out_refs = refs[8:8 + n_out]
    nb, tq, d = x_ref.shape
    rows = nb * tq
    xn = _rms(x_ref[...]) * (1.0 + sc_ref[...]) + sh_ref[...]
    xb = xn.reshape(rows, d).astype(BF16)

    def tiled(t_ref):
        return jnp.broadcast_to(t_ref[...][None], (nb, tq, LANES)).reshape(rows, LANES)

    def store_tokens(o_ref, val):
        if o_ref.ndim == 3:
            o_ref[...] = val.reshape(nb, tq, d)
            return
        for p in range(d // LANES):
            vp = val[:, p * LANES:(p + 1) * LANES]
            if o_ref.ndim == 4:
                o_ref[:, p] = vp.reshape(nb, tq, LANES)
            else:
                n_cb = o_ref.shape[2]
                v5 = vp.reshape(nb, tq // GRID_W, n_cb, WIN_C, LANES)
                for gr in range(tq // GRID_W):
                    o_ref[:, p, :, gr * WIN_C:(gr + 1) * WIN_C, :] = v5[:, gr]

    for g, kind, o_ref in zip(groups, kinds, out_refs):
        z = jnp.dot(xb, w_ref[:, g * d:(g + 1) * d], preferred_element_type=F32) + b_ref[:, g * d:(g + 1) * d]
        if kind == "gelu":
            o_ref[...] = jax.nn.gelu(z, approximate=True).astype(BF16).reshape(nb, tq, d)
        elif kind == "tok_major":
            o_ref[...] = jnp.swapaxes(z.reshape(nb, tq, d), 0, 1).reshape(rows, d).astype(BF16)
        elif kind == "rope_q":
            r = _rope(z * (HEAD_DIM ** -0.5), tiled(cos_ref), tiled(sa_ref), tiled(sb_ref))
            store_tokens(o_ref, r.astype(BF16))
        elif kind == "rope_k":
            r = _rope(z, tiled(cos_ref), tiled(sa_ref), tiled(sb_ref))
            store_tokens(o_ref, r.astype(BF16))
        elif kind == "bf16":
            store_tokens(o_ref, z.astype(BF16))
        elif kind == "sigmoid":
            o_ref[...] = _sigmoid(z).astype(BF16).reshape(nb, tq, d)
        else:
            raise ValueError(kind)


def _inproj_call(x, sh, sc, w, b, tables, kinds, groups, name, pair_major=(), col_blocked=()):
    nb, n, d = x.shape
    assert len(groups) == len(kinds) and w.shape[0] == d and max(groups) * d < w.shape[1]
    cos, sa, sb = tables
    bat = lambda i: (0, i, 0)
    const3 = lambda i: (0, 0, 0)
    const2 = lambda i: (0, 0)
    out_shapes, out_specs = [], []
    for kind in kinds:
        if kind == "tok_major":
            out_shapes.append(jax.ShapeDtypeStruct((n * nb, d), BF16))
            out_specs.append(pl.BlockSpec((TQ * nb, d), lambda i: (i, 0)))
        elif len(out_shapes) in col_blocked:
            assert TQ % GRID_W == 0
            n_cb = GRID_W // WIN_C
            out_shapes.append(jax.ShapeDtypeStruct((nb, d // LANES, n_cb, n // n_cb, LANES), BF16))
            out_specs.append(pl.BlockSpec((nb, d // LANES, n_cb, TQ // n_cb, LANES), lambda i: (0, 0, 0, i, 0)))
        elif len(out_shapes) in pair_major:
            out_shapes.append(jax.ShapeDtypeStruct((nb, d // LANES, n, LANES), BF16))
            out_specs.append(pl.BlockSpec((nb, d // LANES, TQ, LANES), lambda i: (0, 0, i, 0)))
        else:
            out_shapes.append(jax.ShapeDtypeStruct((nb, n, d), BF16))
            out_specs.append(pl.BlockSpec((nb, TQ, d), bat))
    rows_per_table = cos.shape[0] // TQ
    return pl.pallas_call(
        functools.partial(_inproj_kernel, kinds=kinds, groups=groups),
        grid=(n // TQ,),
        in_specs=[
            pl.BlockSpec((nb, TQ, d), bat),
            pl.BlockSpec(sh.shape, const3),
            pl.BlockSpec(sc.shape, const3),
            pl.BlockSpec(w.shape, const2, pipeline_mode=pl.Buffered(1)),
            pl.BlockSpec(b.shape, const2),
            pl.BlockSpec((TQ, LANES), lambda i: (i % rows_per_table, 0)),
            pl.BlockSpec((TQ, LANES), lambda i: (i % rows_per_table, 0)),
            pl.BlockSpec((TQ, LANES), lambda i: (i % rows_per_table, 0)),
        ],
        out_specs=out_specs,
        out_shape=out_shapes,
        compiler_params=_cparams(("arbitrary",)),
        name=name,
    )(x, sh, sc, w, b, cos, sa, sb)


def _rope_tables(n):
    t = np.arange(n)
    row = (t // GRID_W).astype(np.float32)
    col = (t % GRID_W).astype(np.float32)
    n_freq = HEAD_DIM // 4
    inv = (np.float32(ROPE_BASE) ** (-np.arange(n_freq, dtype=np.float32) / np.float32(n_freq))).astype(np.float32)
    ang_r = (row[:, None] * inv).astype(np.float32)
    ang_c = (col[:, None] * inv).astype(np.float32)
    zero = np.zeros_like(ang_r)
    cos_h = np.concatenate([np.cos(ang_r), np.cos(ang_r), np.cos(ang_c), np.cos(ang_c)], axis=1)
    sa_h = np.concatenate([zero, np.sin(ang_r), zero, np.sin(ang_c)], axis=1)
    sb_h = np.concatenate([-np.sin(ang_r), zero, -np.sin(ang_c), zero], axis=1)
    rep = LANES // HEAD_DIM
    return tuple(jnp.asarray(np.tile(a.astype(np.float32), (1, rep))) for a in (cos_h, sa_h, sb_h))


def _lru_kernel(*refs, tn, nt, reverse, fuse_out):
    if fuse_out:
        (xc_ref, wg_ref, bg_ref, lam_ref, h0_ref, hf_ref, gy_ref, out_ref, hfin_ref, a_s, b_s, hcar_s) = refs
    else:
        (xp_ref, xin_ref, xn_ref, cw_ref, cb_ref, wg_ref, bg_ref, lam_ref, h0_ref,
         out_ref, xc_ref, hfin_ref, a_s, b_s, hcar_s) = refs
    i = pl.program_id(0)
    ti = (nt - 1 - i) if reverse else i
    nb, d = hcar_s.shape

    @pl.when(i == 0)
    def _():
        hcar_s[...] = h0_ref[...]

    def tokens(ref):
        return ref[...].astype(F32).reshape(ref.shape[0] // nb, nb, d)

    if fuse_out:
        xc = tokens(xc_ref)
    else:
        x = tokens(xin_ref)
        xp = jnp.where(ti == 0, 0.0, tokens(xp_ref))
        xnx = jnp.where(ti == nt - 1, 0.0, tokens(xn_ref)[:1])
        ext = jnp.concatenate([xp, x, xnx], axis=0)
        xc = cb_ref[...]
        for k in range(CONV_W):
            xc = xc + ext[k:k + tn] * cw_ref[k]
        xc_ref[...] = xc.reshape(tn * nb, d).astype(BF16)
    rows = tn * nb
    xc2 = xc.reshape(rows, d)
    for s in range(d // LRU_SLAB):
        sl = slice(s * LRU_SLAB, (s + 1) * LRU_SLAB)
        xs = xc2[:, sl]
        half_gates = jnp.dot(xs.astype(BF16), wg_ref[s], preferred_element_type=F32) + bg_ref[s]
        t_r = jnp.tanh(half_gates[:, :LRU_SLAB])
        ig = 0.5 * jnp.tanh(half_gates[:, LRU_SLAB:]) + 0.5
        z = -lam_ref[s]
        softplus = jnp.maximum(z, 0.0) + jnp.log1p(jnp.exp(-jnp.abs(z)))
        k = (-0.5 * LRU_C * LOG2_E) * softplus
        a = jnp.exp2(t_r * k + k)
        b = jnp.sqrt(1.0 - a * a) * (ig * xs)
        a_s[:, :, sl] = a.reshape(tn, nb, LRU_SLAB)
        b_s[:, :, sl] = b.reshape(tn, nb, LRU_SLAB)

    unroll = 8

    def step(j, h):
        for k in range(unroll):
            jj = j * unroll + k
            t = (tn - 1 - jj) if reverse else jj
            h = a_s[t] * h + b_s[t]
            a_s[t] = h
        return h

    h = lax.fori_loop(0, tn // unroll, step, hcar_s[...])
    hcar_s[...] = h
    hfin_ref[...] = h
    if fuse_out:
        hs = jnp.swapaxes(tokens(hf_ref) + a_s[...], 0, 1)
        out_ref[...] = (hs * gy_ref[...].astype(F32)).astype(BF16)
    else:
        out_ref[...] = a_s[...].reshape(tn * nb, d).astype(BF16)


def _lru_call(x_t, wg, bg, lam, h0, conv=None, hf=None, gy=None, *, reverse, name):
    nb, d = h0.shape
    n = x_t.shape[0] // nb
    tn = LRU_TN
    nt = n // tn
    halo = 2
    fuse_out = conv is None
    tile = (lambda i: nt - 1 - i) if reverse else (lambda i: i)
    n_slab = d // LRU_SLAB
    tok_spec = pl.BlockSpec((tn * nb, d), lambda i: (tile(i), 0))
    gate_specs = [
        pl.BlockSpec((n_slab, LRU_SLAB, 2 * LRU_SLAB), lambda i: (0, 0, 0)),
        pl.BlockSpec((n_slab, 1, 2 * LRU_SLAB), lambda i: (0, 0, 0)),
        pl.BlockSpec((n_slab, 1, LRU_SLAB), lambda i: (0, 0, 0)),
        pl.BlockSpec((nb, d), lambda i: (0, 0)),
    ]
    fin_spec = pl.BlockSpec((nb, d), lambda i: (0, 0))
    fin_shape = jax.ShapeDtypeStruct((nb, d), F32)
    tok_shape = jax.ShapeDtypeStruct((n * nb, d), BF16)
    if fuse_out:
        in_specs = [tok_spec] + gate_specs + [tok_spec, pl.BlockSpec((nb, tn, d), lambda i: (0, tile(i), 0))]
        args = [x_t, wg, bg, lam, h0, hf, gy]
        out_specs = [pl.BlockSpec((nb, tn, d), lambda i: (0, tile(i), 0)), fin_spec]
        out_shape = [jax.ShapeDtypeStruct((nb, n, d), BF16), fin_shape]
    else:
        conv_w, conv_b = conv
        in_specs = [
            pl.BlockSpec((halo * nb, d), lambda i: (jnp.maximum(tile(i) * (tn // halo) - 1, 0), 0)),
            tok_spec,
            pl.BlockSpec((halo * nb, d), lambda i: (jnp.minimum((tile(i) + 1) * (tn // halo), n // halo - 1), 0)),
            pl.BlockSpec((CONV_W, 1, d), lambda i: (0, 0, 0)),
            pl.BlockSpec((1, d), lambda i: (0, 0)),
        ] + gate_specs
        args = [x_t, x_t, x_t, conv_w.reshape(CONV_W, 1, d), conv_b.reshape(1, d), wg, bg, lam, h0]
        out_specs = [tok_spec, tok_spec, fin_spec]
        out_shape = [tok_shape, tok_shape, fin_shape]
    return pl.pallas_call(
        functools.partial(_lru_kernel, tn=tn, nt=nt, reverse=reverse, fuse_out=fuse_out),
        grid=(nt,),
        in_specs=in_specs,
        out_specs=out_specs,
        out_shape=out_shape,
        scratch_shapes=[
            pltpu.VMEM((tn, nb, d), F32),
            pltpu.VMEM((tn, nb, d), F32),
            pltpu.VMEM((nb, d), F32),
        ],
        compiler_params=_cparams(("arbitrary",)),
        name=name,
    )(*args)


def _lru_gate_weights(wa, ba, wi, bi, lam):
    n_blk, blk, _ = wa.shape
    d = n_blk * blk
    n_slab = d // LRU_SLAB
    per = LRU_SLAB // blk
    eye = jnp.eye(per, dtype=F32)

    def dense_slabs(w):
        return jnp.einsum("saij,ac->saicj", w.reshape(n_slab, per, blk, blk), eye).reshape(n_slab, LRU_SLAB, LRU_SLAB)

    wg = (0.5 * jnp.concatenate([dense_slabs(wa), dense_slabs(wi)], axis=2).astype(BF16)).astype(BF16)
    bg = 0.5 * jnp.concatenate([ba.reshape(n_slab, 1, LRU_SLAB), bi.reshape(n_slab, 1, LRU_SLAB)], axis=2)
    return wg, bg, lam.reshape(n_slab, 1, LRU_SLAB)


def _na_blocks(g):
    lo = min(max(8 * g - WIN_C // 2, 0) // WIN_C, GRID_W // WIN_C - 2)
    return (lo, lo + 1)


def _na_kernel(q_ref, k_ref, v_ref, kc_ref, vc_ref, tab_ref, o_ref, s_s, p_s, pc_s, inv_s, qs_s, sc_s, *, rows):
    win = WIN_R * GRID_W
    ch = NA_CHUNK
    n_chunk = rows // ch
    qh = 2 * GRID_W
    n_cb = GRID_W // WIN_C
    blk = WIN_R * WIN_C
    grp = SUBLANES
    lane = lax.broadcasted_iota(jnp.int32, (GRID_W, LANES), 1)
    first_head = lane < HEAD_DIM
    nt_dims = (((1,), (1,)), ((), ()))

    def row_start(r):
        return pl.multiple_of(r * GRID_W, GRID_W)

    def window(ref, rs):
        run = pl.ds(pl.multiple_of(rs * WIN_C, WIN_C), blk)
        return jnp.concatenate([ref[0, 0, cb, run, :] for cb in range(n_cb)], axis=0)

    def groups():
        for h in range(2):
            for g in range(GRID_W // grp):
                yield h, g * grp, _na_blocks(g)

    p_s[...] = jnp.zeros_like(p_s)

    def stack_heads(r, carry):
        q = q_ref[0, 0, pl.ds(row_start(r), GRID_W), :]
        zero = jnp.zeros_like(q)
        qs_s[r] = jnp.concatenate([jnp.where(first_head, q, zero), jnp.where(first_head, zero, q)], axis=0)
        return carry

    lax.fori_loop(0, rows, stack_heads, 0)
    sc_s[...] = lax.dot_general(qs_s[...].reshape(rows * qh, LANES), kc_ref[0, 0], nt_dims,
                                preferred_element_type=F32)

    def stage1(c, slot):
        for u in range(ch):
            r = c * ch + u
            rs = jnp.clip(r - WIN_R // 2, 0, rows - WIN_R)
            s = lax.dot_general(qs_s[r], window(k_ref, rs), nt_dims, preferred_element_type=F32)
            d0 = rs - r + (WIN_R - 1)
            for h, q0, cbs in groups():
                for cb in cbs:
                    rw = slice(h * GRID_W + q0, h * GRID_W + q0 + grp)
                    ln = slice(cb * blk, (cb + 1) * blk)
                    s_s[slot, u, rw, ln] = s[rw, ln] + tab_ref[0, h, d0, q0:q0 + grp, ln]

    def stage2(c, slot):
        for u in range(ch):
            base = pl.multiple_of((c * ch + u) * qh, qh)
            for r0 in range(0, qh, 2 * grp):
                parts = {}
                pcs, invs = [], []
                for half in range(2):
                    ra = r0 + half * grp
                    cbs = _na_blocks((ra % GRID_W) // grp)
                    sl = [s_s[slot, u, ra:ra + grp, cb * blk:(cb + 1) * blk] for cb in cbs]
                    sc = sc_s[pl.ds(base + ra, grp), :]
                    m = jnp.maximum(jnp.max(jnp.maximum(sl[0], sl[1]), axis=1, keepdims=True),
                                    jnp.max(sc, axis=1, keepdims=True))
                    ps = [jnp.exp(x - m) for x in sl]
                    pc = jnp.exp(sc - m)
                    den = jnp.sum(ps[0] + ps[1], axis=1, keepdims=True) + jnp.sum(pc, axis=1, keepdims=True)
                    for cb, pb in zip(cbs, ps):
                        parts[(half, cb)] = pb
                    pcs.append(pc)
                    invs.append(jnp.broadcast_to(1.0 / den, (grp, LANES)))
                zero = jnp.zeros((grp, blk), F32)
                for cb in sorted({cb for _, cb in parts}):
                    both = jnp.concatenate([parts.get((0, cb), zero), parts.get((1, cb), zero)], axis=0)
                    p_s[slot, u, r0:r0 + 2 * grp, cb * blk:(cb + 1) * blk] = both.astype(BF16)
                pc_s[slot, u * qh + r0:u * qh + r0 + 2 * grp, :] = jnp.concatenate(pcs, axis=0).astype(BF16)
                inv_s[slot, u, r0:r0 + 2 * grp, :] = jnp.concatenate(invs, axis=0)

    def stage3(c, slot):
        oc = jnp.dot(pc_s[slot], vc_ref[0, 0], preferred_element_type=F32)
        for u in range(ch):
            r = c * ch + u
            rs = jnp.clip(r - WIN_R // 2, 0, rows - WIN_R)
            o = (jnp.dot(p_s[slot, u], window(v_ref, rs), preferred_element_type=F32)
                 + oc[u * qh:(u + 1) * qh]) * inv_s[slot, u]
            out = jnp.where(first_head, o[:GRID_W], o[GRID_W:])
            o_ref[0, 0, pl.ds(row_start(r), GRID_W), :] = out.astype(BF16)

    stage1(0, 0)
    stage2(0, 0)
    stage1(1, 1)

    def body(j, carry):
        slot = j % 2
        stage3(j - 2, slot)
        stage2(j - 1, 1 - slot)
        stage1(j, slot)
        return carry

    lax.fori_loop(2, n_chunk, body, 0)
    last = n_chunk % 2
    stage3(n_chunk - 2, last)
    stage2(n_chunk - 1, 1 - last)
    stage3(n_chunk - 1, 1 - last)


def _na_call(q, k, v, kc, vc, tab):
    nb, n_pair, n, _ = q.shape
    l = kc.shape[2]
    rows = n // GRID_W
    n_cb = GRID_W // WIN_C
    assert k.shape == v.shape == (nb, n_pair, n_cb, n // n_cb, LANES)
    win = WIN_R * GRID_W
    qh = 2 * GRID_W
    assert rows % NA_CHUNK == 0 and rows // NA_CHUNK >= 2
    tok_spec = pl.BlockSpec((1, 1, n, LANES), lambda b, p: (b, p, 0, 0))
    kv_spec = pl.BlockSpec((1, 1, n_cb, n // n_cb, LANES), lambda b, p: (b, p, 0, 0, 0))
    ctx_spec = pl.BlockSpec((1, 1, l, LANES), lambda b, p: (b, p, 0, 0))
    return pl.pallas_call(
        functools.partial(_na_kernel, rows=rows),
        grid=(nb, n_pair),
        in_specs=[tok_spec, kv_spec, kv_spec, ctx_spec, ctx_spec,
                  pl.BlockSpec((1,) + tab.shape[1:], lambda b, p: (p, 0, 0, 0, 0))],
        out_specs=tok_spec,
        out_shape=jax.ShapeDtypeStruct((nb, n_pair, n, LANES), BF16),
        scratch_shapes=[
            pltpu.VMEM((2, NA_CHUNK, qh, win), F32),
            pltpu.VMEM((2, NA_CHUNK, qh, win), BF16),
            pltpu.VMEM((2, NA_CHUNK * qh, l), BF16),
            pltpu.VMEM((2, NA_CHUNK, qh, LANES), F32),
            pltpu.VMEM((rows, qh, LANES), BF16),
            pltpu.VMEM((rows * qh, l), F32),
        ],
        compiler_params=_cparams(("arbitrary", "arbitrary")),
        name="neighbourhood_attention",
    )(q, k, v, kc, vc, tab)


def _na_bias_table(rpb):
    qc = np.arange(GRID_W)
    kc = np.arange(GRID_W)
    n_cb = GRID_W // WIN_C
    cstart = np.clip(qc - WIN_C // 2, 0, GRID_W - WIN_C)
    ok = (kc[None, :] >= cstart[:, None]) & (kc[None, :] < cstart[:, None] + WIN_C)
    dc = np.clip(kc[None, :] - qc[:, None], -(WIN_C - 1), WIN_C - 1) + WIN_C - 1
    expand = (dc[None] == np.arange(2 * WIN_C - 1)[:, None, None]).astype(np.float32)
    expand = expand.reshape(2 * WIN_C - 1, GRID_W, n_cb, WIN_C)
    mask = np.where(ok, 0.0, NEG_INF).astype(np.float32).reshape(GRID_W, n_cb, 1, WIN_C)
    h = rpb.shape[0]
    rows = jnp.stack([rpb[:, d0:d0 + WIN_R].astype(F32) for d0 in range(WIN_R)], axis=1)
    tab = jnp.einsum("hdjx,xqbc->hdqbjc", rows, jnp.asarray(expand), precision=lax.Precision.HIGHEST)
    tab = tab + jnp.asarray(mask)
    return tab.reshape(h // 2, 2, WIN_R, GRID_W, WIN_R * GRID_W)


def _mix_kernel(yr_ref, yn_ref, gr_ref, gn_ref, x_ref, g1_ref, sh2_ref, sc2_ref,
                wpr_ref, wpn_ref, wo_ref, wr_ref, x1_ref, xn2_ref, lg_ref):
    nb, tq, d = x_ref.shape
    rows = nb * tq
    pr = jnp.dot(yr_ref[...].reshape(rows, d), wpr_ref[...], preferred_element_type=F32)
    yn = jnp.concatenate([yn_ref[:, p].reshape(rows, LANES) for p in range(d // LANES)], axis=1)
    pn = jnp.dot(yn, wpn_ref[...], preferred_element_type=F32)
    mix = gr_ref[...].reshape(rows, d).astype(F32) * pr + gn_ref[...].reshape(rows, d).astype(F32) * pn
    o = jnp.dot(mix.astype(BF16), wo_ref[...], preferred_element_type=F32)
    x1 = x_ref[...] + g1_ref[...] * o.reshape(nb, tq, d)
    x1_ref[...] = x1
    xb = (_rms(x1) * (1.0 + sc2_ref[...]) + sh2_ref[...]).astype(BF16)
    xn2_ref[...] = xb
    lg_ref[0] = lax.dot_general(wr_ref[...], xb.reshape(rows, d), (((1,), (1,)), ((), ())),
                                preferred_element_type=F32)


def _mix_call(yr, yn, gr, gn, x, g1, sh2, sc2, wpr, wpn, wo, wr):
    nb, n, d = x.shape
    bat = lambda i: (0, i, 0)
    const3 = lambda i: (0, 0, 0)
    const2 = lambda i: (0, 0)
    act = pl.BlockSpec((nb, TQ, d), bat)
    vec = pl.BlockSpec((nb, 1, d), const3)
    wsp = lambda w: pl.BlockSpec(w.shape, const2, pipeline_mode=pl.Buffered(1))
    return pl.pallas_call(
        _mix_kernel,
        grid=(n // TQ,),
        in_specs=[act, pl.BlockSpec((nb, d // LANES, TQ, LANES), lambda i: (0, 0, i, 0)), act, act, act,
                  vec, vec, vec, wsp(wpr), wsp(wpn), wsp(wo), wsp(wr)],
        out_specs=[act, act, pl.BlockSpec((1, wr.shape[0], nb * TQ), lambda i: (i, 0, 0))],
        out_shape=[
            jax.ShapeDtypeStruct((nb, n, d), F32),
            jax.ShapeDtypeStruct((nb, n, d), BF16),
            jax.ShapeDtypeStruct((n // TQ, wr.shape[0], nb * TQ), F32),
        ],
        compiler_params=_cparams(("arbitrary",)),
        name="merge_out_norm_router",
    )(yr, yn, gr, gn, x, g1, sh2, sc2, wpr, wpn, wo, wr)


def _route_kernel(lg_ref, rank_ref, aff_ref, cnt_ref, *, cap):
    lg = lg_ref[0]
    n_e, n = lg.shape
    ex = jnp.exp(lg - jnp.max(lg, axis=0, keepdims=True))
    aff = ex / jnp.sum(ex, axis=0, keepdims=True)
    aff_ref[0] = aff

    def as_float(bits):
        return lax.bitcast_convert_type(bits, F32)

    thr = jnp.zeros((n_e, 1), jnp.int32)
    for bit in range(30, -1, -1):
        cand = thr | (1 << bit)
        cnt = jnp.sum((aff >= as_float(cand)).astype(F32), axis=1, keepdims=True)
        thr = jnp.where(cnt >= cap, cand, thr)
    thr = jnp.where(thr < MIN_NORMAL_BITS, 0, thr)
    gt = aff >= as_float(jnp.where(thr == 0, MIN_NORMAL_BITS, thr + 1))
    eq = (aff >= as_float(thr)) & jnp.logical_not(gt)
    need = cap - jnp.sum(gt.astype(F32), axis=1, keepdims=True)

    blk = ROUTE_CHUNK
    row = lax.broadcasted_iota(jnp.int32, (blk, blk), 0)
    col = lax.broadcasted_iota(jnp.int32, (blk, blk), 1)
    upper = (row <= col).astype(BF16)

    def cumsum_tokens(mask):
        off = jnp.zeros((n_e, 1), F32)
        outs, offs = [], []
        for c in range(n // blk):
            x = mask[:, c * blk:(c + 1) * blk].astype(BF16)
            cs = jnp.dot(x, upper, preferred_element_type=F32) + off
            offs.append(off)
            outs.append(cs)
            off = cs[:, blk - 1:blk]
        offs.append(off)
        return jnp.concatenate(outs, axis=1), offs

    cum_eq, _ = cumsum_tokens(eq)
    sel = gt | (eq & ((cum_eq - eq.astype(F32)) < need))
    cum_sel, offs = cumsum_tokens(sel)
    rank_ref[0] = jnp.where(sel, cum_sel - 1.0, -1.0).astype(jnp.int32)
    pad = jnp.zeros((n_e, LANES - len(offs)), F32)
    cnt_ref[0] = jnp.concatenate(offs + [pad], axis=1).astype(jnp.int32)


def _route_call(lg_t, cap):
    nb, n_e, n = lg_t.shape
    blk3 = lambda b: (b, 0, 0)
    return pl.pallas_call(
        functools.partial(_route_kernel, cap=cap),
        grid=(nb,),
        in_specs=[pl.BlockSpec((1, n_e, n), blk3)],
        out_specs=[pl.BlockSpec((1, n_e, n), blk3), pl.BlockSpec((1, n_e, n), blk3),
                   pl.BlockSpec((1, n_e, LANES), blk3)],
        out_shape=[jax.ShapeDtypeStruct((nb, n_e, n), jnp.int32), jax.ShapeDtypeStruct((nb, n_e, n), F32),
                   jax.ShapeDtypeStruct((nb, n_e, LANES), jnp.int32)],
        compiler_params=_cparams(("arbitrary",)),
        name="route_select",
    )(lg_t)


def _slot_windows(cnt_ref, b, c, n_e, n_chunk, cap):
    wins = []
    for e in range(n_e):
        base = (b * n_e + e) * (n_chunk + 1) + c
        lo = cnt_ref[base]
        hi = cnt_ref[base + 1]
        lo_al = jnp.minimum((lo // SLOT_ALIGN) * SLOT_ALIGN, cap - ROUTE_WIN)
        extra = jnp.maximum((hi - lo_al + ROUTE_WIN - 1) // ROUTE_WIN - 1, 0)
        wins.append((pl.multiple_of(lo_al, SLOT_ALIGN), extra))
    return wins


def _one_hot_t(rank_row, start, first=None):
    slot = start + lax.broadcasted_iota(jnp.int32, (ROUTE_WIN, rank_row.shape[1]), 0)
    hit = rank_row == slot
    if first is not None:
        hit = hit & (slot >= first)
    return hit


def _spill_window(k, lo_al, cap):
    first = lo_al + k * ROUTE_WIN
    start = pl.multiple_of(jnp.minimum(first, cap - ROUTE_WIN), SLOT_ALIGN)
    return start, first


def _dispatch_kernel(cnt_ref, x_ref, rank_ref, aff_ref, xe_ref, g_ref, *, cap, n_chunk):
    b = pl.program_id(0)
    step = pl.program_id(1)
    n_e = rank_ref.shape[1]

    @pl.when(step == 0)
    def _():
        xe_ref[...] = jnp.zeros_like(xe_ref)
        g_ref[...] = jnp.zeros_like(g_ref)

    for sub in range(ROUTE_SUB):
        toks = slice(sub * ROUTE_CHUNK, (sub + 1) * ROUTE_CHUNK)
        c = step * ROUTE_SUB + sub
        x = x_ref[0, toks, :]
        wins = _slot_windows(cnt_ref, b, c, n_e, n_chunk, cap)
        hots = [_one_hot_t(rank_ref[0, e:e + 1, toks], wins[e][0]) for e in range(n_e)]
        stacked = jnp.concatenate([h.astype(BF16) for h in hots], axis=0)
        rows = jnp.dot(stacked, x, preferred_element_type=F32)
        for e in range(n_e):
            sl = pl.ds(wins[e][0], ROUTE_WIN)
            xe_ref[e, sl, :] = xe_ref[e, sl, :] + rows[e * ROUTE_WIN:(e + 1) * ROUTE_WIN].astype(BF16)
            g_ref[e, sl, :] = g_ref[e, sl, :] + jnp.sum(jnp.where(hots[e], aff_ref[0, e:e + 1, toks], 0.0),
                                                        axis=1, keepdims=True)

        @pl.when(sum(extra for _, extra in wins) > 0)
        def _(wins=wins, toks=toks):
            for e in range(n_e):
                lo_al, extra = wins[e]

                def spill(k, carry, e=e, lo_al=lo_al):
                    start, first = _spill_window(k, lo_al, cap)
                    hot = _one_hot_t(rank_ref[0, e:e + 1, toks], start, first)
                    sl2 = pl.ds(start, ROUTE_WIN)
                    xe_ref[e, sl2, :] = xe_ref[e, sl2, :] + jnp.dot(hot.astype(BF16), x_ref[0, toks, :],
                                                                   preferred_element_type=F32).astype(BF16)
                    g_ref[e, sl2, :] = g_ref[e, sl2, :] + jnp.sum(jnp.where(hot, aff_ref[0, e:e + 1, toks], 0.0),
                                                                  axis=1, keepdims=True)
                    return carry

                lax.fori_loop(1, extra + 1, spill, 0)


def _dispatch_call(cnt_flat, xn2, rank_t, aff_t, cap):
    nb, n, d = xn2.shape
    n_e = rank_t.shape[1]
    n_chunk = n // ROUTE_CHUNK
    grid_spec = pltpu.PrefetchScalarGridSpec(
        num_scalar_prefetch=1,
        grid=(nb, n_chunk // ROUTE_SUB),
        in_specs=[
            pl.BlockSpec((1, ROUTE_SUB * ROUTE_CHUNK, d), lambda b, c, cnt: (b, c, 0)),
            pl.BlockSpec((1, n_e, ROUTE_SUB * ROUTE_CHUNK), lambda b, c, cnt: (b, 0, c)),
            pl.BlockSpec((1, n_e, ROUTE_SUB * ROUTE_CHUNK), lambda b, c, cnt: (b, 0, c)),
        ],
        out_specs=[
            pl.BlockSpec((n_e, cap, d), lambda b, c, cnt: (0, b, 0)),
            pl.BlockSpec((n_e, cap, 1), lambda b, c, cnt: (0, b, 0)),
        ],
    )
    return pl.pallas_call(
        functools.partial(_dispatch_kernel, cap=cap, n_chunk=n_chunk),
        grid_spec=grid_spec,
        out_shape=[jax.ShapeDtypeStruct((n_e, nb * cap, d), BF16), jax.ShapeDtypeStruct((n_e, nb * cap, 1), F32)],
        compiler_params=_cparams(("arbitrary", "arbitrary")),
        name="moe_dispatch",
    )(cnt_flat, xn2, rank_t, aff_t)


def _ffn_kernel(xe_ref, g_ref, wg_ref, wu_ref, wd_ref, o_ref, acc_s):
    f = pl.program_id(2)
    last = pl.num_programs(2) - 1
    d = o_ref.shape[2]
    xe = xe_ref[0]
    h1 = jnp.dot(xe, wg_ref[0].astype(BF16), preferred_element_type=F32)
    h2 = jnp.dot(xe, wu_ref[0].astype(BF16), preferred_element_type=F32)
    hid = (h1 * _sigmoid(h1) * h2).astype(BF16)
    wd = wd_ref[0].astype(BF16)

    def chunks():
        for c in range(d // FFN_TN):
            cols = slice(c * FFN_TN, (c + 1) * FFN_TN)
            yield cols, jnp.dot(hid, wd[:, cols], preferred_element_type=F32)

    @pl.when(f == 0)
    def _():
        for cols, part in chunks():
            acc_s[:, cols] = part

    @pl.when((f > 0) & (f < last))
    def _():
        for cols, part in chunks():
            acc_s[:, cols] += part

    @pl.when(f == last)
    def _():
        for cols, part in chunks():
            o_ref[0, :, cols] = ((acc_s[:, cols] + part) * g_ref[0]).astype(BF16)


def _ffn_call(xe, g, w_gate, w_up, w_down):
    e, m, d = xe.shape
    dff = w_gate.shape[2]
    tm = min(FFN_TM, m)
    assert dff // FFN_TF >= 2 and m % tm == 0 and d % FFN_TN == 0
    return pl.pallas_call(
        _ffn_kernel,
        grid=(e, m // tm, dff // FFN_TF),
        in_specs=[
            pl.BlockSpec((1, tm, d), lambda ei, mi, fi: (ei, mi, 0)),
            pl.BlockSpec((1, tm, 1), lambda ei, mi, fi: (ei, mi, 0)),
            pl.BlockSpec((1, d, FFN_TF), lambda ei, mi, fi: (ei, 0, fi)),
            pl.BlockSpec((1, d, FFN_TF), lambda ei, mi, fi: (ei, 0, fi)),
            pl.BlockSpec((1, FFN_TF, d), lambda ei, mi, fi: (ei, fi, 0)),
        ],
        out_specs=pl.BlockSpec((1, tm, d), lambda ei, mi, fi: (ei, mi, 0)),
        out_shape=jax.ShapeDtypeStruct((e, m, d), BF16),
        scratch_shapes=[pltpu.VMEM((tm, d), F32)],
        compiler_params=_cparams(("arbitrary", "arbitrary", "arbitrary")),
        name="expert_ffn",
    )(xe, g, w_gate, w_up, w_down)


def _combine_kernel(cnt_ref, ye_ref, rank_ref, x1_ref, g2_ref, fn_ref, o_ref, acc_s, *, cap, n_chunk):
    b = pl.program_id(0)
    step = pl.program_id(1)
    n_e = rank_ref.shape[1]
    tn_dims = (((0,), (0,)), ((), ()))

    for sub in range(ROUTE_SUB):
        toks = slice(sub * ROUTE_CHUNK, (sub + 1) * ROUTE_CHUNK)
        c = step * ROUTE_SUB + sub
        wins = _slot_windows(cnt_ref, b, c, n_e, n_chunk, cap)
        hots = [_one_hot_t(rank_ref[0, e:e + 1, toks], wins[e][0]).astype(BF16) for e in range(n_e)]
        ys = [ye_ref[e, pl.ds(wins[e][0], ROUTE_WIN), :] for e in range(n_e)]
        moe = lax.dot_general(jnp.concatenate(hots, axis=0), jnp.concatenate(ys, axis=0), tn_dims,
                              preferred_element_type=F32)

        def finish(m, toks=toks):
            x = x1_ref[0, toks, :] + g2_ref[0] * m
            o_ref[0, toks, :] = _rms(x) * fn_ref[0]

        any_spill = sum(extra for _, extra in wins) > 0

        @pl.when(jnp.logical_not(any_spill))
        def _(moe=moe, finish=finish):
            finish(moe)

        @pl.when(any_spill)
        def _(moe=moe, finish=finish, wins=wins, toks=toks):
            acc_s[...] = moe
            for e in range(n_e):
                lo_al, extra = wins[e]

                def spill(k, carry, e=e, lo_al=lo_al):
                    start, first = _spill_window(k, lo_al, cap)
                    hot = _one_hot_t(rank_ref[0, e:e + 1, toks], start, first).astype(BF16)
                    acc_s[...] += lax.dot_general(hot, ye_ref[e, pl.ds(start, ROUTE_WIN), :], tn_dims,
                                                  preferred_element_type=F32)
                    return carry

                lax.fori_loop(1, extra + 1, spill, 0)
            finish(acc_s[...])


def _combine_call(cnt_flat, ye, rank_t, x1, g2, fnorm, cap):
    nb, n, d = x1.shape
    n_e = rank_t.shape[1]
    n_chunk = n // ROUTE_CHUNK
    grid_spec = pltpu.PrefetchScalarGridSpec(
        num_scalar_prefetch=1,
        grid=(nb, n_chunk // ROUTE_SUB),
        in_specs=[
            pl.BlockSpec((n_e, cap, d), lambda b, c, cnt: (0, b, 0)),
            pl.BlockSpec((1, n_e, ROUTE_SUB * ROUTE_CHUNK), lambda b, c, cnt: (b, 0, c)),
            pl.BlockSpec((1, ROUTE_SUB * ROUTE_CHUNK, d), lambda b, c, cnt: (b, c, 0)),
            pl.BlockSpec((1, 1, d), lambda b, c, cnt: (b, 0, 0)),
            pl.BlockSpec((1, 1, d), lambda b, c, cnt: (0, 0, 0)),
        ],
        out_specs=pl.BlockSpec((1, ROUTE_SUB * ROUTE_CHUNK, d), lambda b, c, cnt: (b, c, 0)),
        scratch_shapes=[pltpu.VMEM((ROUTE_CHUNK, d), F32)],
    )
    return pl.pallas_call(
        functools.partial(_combine_kernel, cap=cap, n_chunk=n_chunk),
        grid_spec=grid_spec,
        out_shape=jax.ShapeDtypeStruct((nb, n, d), F32),
        compiler_params=_cparams(("arbitrary", "arbitrary")),
        name="moe_combine_final_norm",
    )(cnt_flat, ye, rank_t, x1, g2, fnorm.reshape(1, 1, d))


def kernel(x, c, ctx, c_ctx, w_mod, b_mod, w_in, b_in, conv_w, conv_b, lru_wa, lru_ba, lru_wi, lru_bi,
           lru_lambda, na_rpb, w_proj_rnn, w_proj_na, w_out, w_router, w_exp_gate, w_exp_up, w_exp_down,
           final_norm):
    nb, n, d = x.shape
    l = ctx.shape[1]
    assert w_mod.shape[0] == 1, "single-layer problem"
    assert nb == SUBLANES and n % LRU_TN == 0 and l % LRU_TN == 0 and d % LRU_SLAB == 0
    lyr = 0

    cc = jnp.concatenate([c, c_ctx[None], jnp.zeros((2 * SUBLANES - nb - 1, d), F32)], axis=0)
    mod = _mod_call(cc, w_mod[lyr], b_mod[lyr])
    sh1, sc1, g1, sh2, sc2, g2 = [m[:nb, None, :] for m in jnp.split(mod, 6, axis=-1)]
    csh1, csc1 = [jnp.broadcast_to(m[nb:nb + 1, None, :], (nb, 1, d)) for m in jnp.split(mod, 6, axis=-1)[:2]]

    w_in_b = w_in[lyr].astype(BF16)
    b_in_r = b_in[lyr].reshape(1, -1)
    tables = _rope_tables(n)
    gy, xr_t, q, k, v, sgr, sgn = _inproj_call(
        x, sh1, sc1, w_in_b, b_in_r, tables,
        ("gelu", "tok_major", "rope_q", "rope_k", "bf16", "sigmoid", "sigmoid"), tuple(range(7)), "in_proj_latent",
        pair_major=(2,), col_blocked=(3, 4))
    xrc_t, kc, vc = _inproj_call(
        ctx, csh1, csc1, w_in_b, b_in_r, tables, ("tok_major", "bf16", "bf16"), (1, 3, 4), "in_proj_context",
        pair_major=(1, 2))

    gw = [_lru_gate_weights(lru_wa[lyr, dr], lru_ba[lyr, dr], lru_wi[lyr, dr], lru_bi[lyr, dr], lru_lambda[lyr, dr])
          for dr in range(2)]
    zeros_h = jnp.zeros((nb, d), F32)
    conv = (conv_w[lyr], conv_b[lyr])
    _, xcc_t, hc_f = _lru_call(xrc_t, *gw[0], zeros_h, conv=conv, reverse=False, name="rglru_ctx_fwd")
    _, hc_b = _lru_call(xcc_t, *gw[1], zeros_h, hf=xcc_t, gy=jnp.zeros((nb, l, d), BF16), reverse=True,
                        name="rglru_ctx_bwd")
    hf_t, xc_t, _ = _lru_call(xr_t, *gw[0], hc_f, conv=conv, reverse=False, name="rglru_fwd")
    y_rnn, _ = _lru_call(xc_t, *gw[1], hc_b, hf=hf_t, gy=gy, reverse=True, name="rglru_bwd")

    y_na = _na_call(q, k, v, kc, vc, _na_bias_table(na_rpb[lyr]))

    x1, xn2, lg_tiles = _mix_call(y_rnn, y_na, sgr, sgn, x, g1, sh2, sc2,
                                  w_proj_rnn[lyr].astype(BF16), w_proj_na[lyr].astype(BF16),
                                  w_out[lyr].astype(BF16), w_router[lyr].T.astype(BF16))
    n_e = w_router.shape[2]
    lg_t = lg_tiles.reshape(n // TQ, n_e, nb, TQ).transpose(2, 1, 0, 3).reshape(nb, n_e, n)

    cap = EC_CAPACITY * n // N_EXPERTS
    n_chunk = n // ROUTE_CHUNK
    assert cap % SLOT_ALIGN == 0 and cap >= ROUTE_WIN and n % (ROUTE_SUB * ROUTE_CHUNK) == 0
    rank_t, aff_t, cnt = _route_call(lg_t, cap)
    cnt_flat = cnt[:, :, :n_chunk + 1].reshape(-1)
    xe, ge = _dispatch_call(cnt_flat, xn2, rank_t, aff_t, cap)
    ye = _ffn_call(xe, ge, w_exp_gate[lyr], w_exp_up[lyr], w_exp_down[lyr])
    return _combine_call(cnt_flat, ye, rank_t, x1, g2, final_norm, cap)
```

```python
import functools

import numpy as np
import jax
import jax.numpy as jnp
from jax import lax
from jax.experimental import pallas as pl
from jax.experimental.pallas import tpu as pltpu

F32 = jnp.float32
BF16 = jnp.bfloat16

GRID_W = 64
N_HEADS = 16
HEAD_DIM = 64
N_LRU_BLOCKS = 16
CONV_W = 4
LRU_C = 8.0
WIN_R = 8
WIN_C = 16
ROPE_BASE = 10000.0
N_EXPERTS = 16
EC_CAPACITY = 2
EPS = 1e-6
NEG_INF = -1e30

LANES = 128
SUBLANES = 8
VMEM_LIMIT = 56 * 1024 * 1024

TQ = 128
LRU_SLAB = 256
LRU_TN = 128
FFN_TM = 2048
FFN_TF = 512
FFN_TN = 256
NA_CHUNK = 2
ROUTE_CHUNK = 256
ROUTE_SUB = 4
ROUTE_WIN = 64
SLOT_ALIGN = 16
MIN_NORMAL_BITS = 0x00800000
LOG2_E = 1.4426950408889634


def _cparams(sem):
    return pltpu.CompilerParams(dimension_semantics=sem, vmem_limit_bytes=VMEM_LIMIT)


def _sigmoid(x):
    return 0.5 * jnp.tanh(0.5 * x) + 0.5


def _rms(x):
    return x * lax.rsqrt(jnp.mean(x * x, axis=-1, keepdims=True) + EPS)


def _mod_kernel(c_ref, w_ref, b_ref, o_ref):
    c = c_ref[...]
    s = c * jax.nn.sigmoid(c)
    o_ref[...] = jnp.dot(s, w_ref[...], preferred_element_type=F32) + b_ref[...]


def _mod_call(cc, w_mod, b_mod):
    rows, d = cc.shape
    n_out = w_mod.shape[1]
    tn = 1024
    return pl.pallas_call(
        _mod_kernel,
        grid=(n_out // tn,),
        in_specs=[
            pl.BlockSpec((rows, d), lambda j: (0, 0)),
            pl.BlockSpec((d, tn), lambda j: (0, j)),
            pl.BlockSpec((1, tn), lambda j: (0, j)),
        ],
        out_specs=pl.BlockSpec((rows, tn), lambda j: (0, j)),
        out_shape=jax.ShapeDtypeStruct((rows, n_out), F32),
        compiler_params=_cparams(("arbitrary",)),
        name="adaln_mod",
    )(cc, w_mod, b_mod.reshape(1, n_out))


def _rope(z, cos, sa, sb):
    outs = []
    for g in range(z.shape[1] // LANES):
        zg = z[:, g * LANES:(g + 1) * LANES]
        outs.append(zg * cos + pltpu.roll(zg, 16, 1) * sa + pltpu.roll(zg, LANES - 16, 1) * sb)
    return jnp.concatenate(outs, axis=1)


def _inproj_kernel(*refs, kinds, groups):
    n_out = len(kinds)
    x_ref, sh_ref, sc_ref, w_ref, b_ref, cos_ref, sa_ref, sb_ref = refs[:8]
    out_refs = refs[8:8 + n_out]
    nb, tq, d = x_ref.shape
    rows = nb * tq
    xn = _rms(x_ref[...]) * (1.0 + sc_ref[...]) + sh_ref[...]
    xb = xn.reshape(rows, d).astype(BF16)

    def tiled(t_ref):
        return jnp.broadcast_to(t_ref[...][None], (nb, tq, LANES)).reshape(rows, LANES)

    def store_tokens(o_ref, val):
        if o_ref.ndim == 3:
            o_ref[...] = val.reshape(nb, tq, d)
            return
        for p in range(d // LANES):
            vp = val[:, p * LANES:(p + 1) * LANES]
            if o_ref.ndim == 4:
                o_ref[:, p] = vp.reshape(nb, tq, LANES)
            else:
                n_cb = o_ref.shape[2]
                v5 = vp.reshape(nb, tq // GRID_W, n_cb, WIN_C, LANES)
                for gr in range(tq // GRID_W):
                    o_ref[:, p, :, gr * WIN_C:(gr + 1) * WIN_C, :] = v5[:, gr]

    for g, kind, o_ref in zip(groups, kinds, out_refs):
        z = jnp.dot(xb, w_ref[:, g * d:(g + 1) * d], preferred_element_type=F32) + b_ref[:, g * d:(g + 1) * d]
        if kind == "gelu":
            o_ref[...] = jax.nn.gelu(z, approximate=True).astype(BF16).reshape(nb, tq, d)
        elif kind == "tok_major":
            o_ref[...] = jnp.swapaxes(z.reshape(nb, tq, d), 0, 1).reshape(rows, d).astype(BF16)
        elif kind == "rope_q":
            r = _rope(z * (HEAD_DIM ** -0.5), tiled(cos_ref), tiled(sa_ref), tiled(sb_ref))
            store_tokens(o_ref, r.astype(BF16))
        elif kind == "rope_k":
            r = _rope(z, tiled(cos_ref), tiled(sa_ref), tiled(sb_ref))
            store_tokens(o_ref, r.astype(BF16))
        elif kind == "bf16":
            store_tokens(o_ref, z.astype(BF16))
        elif kind == "sigmoid":
            o_ref[...] = _sigmoid(z).astype(BF16).reshape(nb, tq, d)
        else:
            raise ValueError(kind)


def _inproj_call(x, sh, sc, w, b, tables, kinds, groups, name, pair_major=(), col_blocked=()):
    nb, n, d = x.shape
    assert len(groups) == len(kinds) and w.shape[0] == d and max(groups) * d < w.shape[1]
    cos, sa, sb = tables
    bat = lambda i: (0, i, 0)
    const3 = lambda i: (0, 0, 0)
    const2 = lambda i: (0, 0)
    out_shapes, out_specs = [], []
    for kind in kinds:
        if kind == "tok_major":
            out_shapes.append(jax.ShapeDtypeStruct((n * nb, d), BF16))
            out_specs.append(pl.BlockSpec((TQ * nb, d), lambda i: (i, 0)))
        elif len(out_shapes) in col_blocked:
            assert TQ % GRID_W == 0
            n_cb = GRID_W // WIN_C
            out_shapes.append(jax.ShapeDtypeStruct((nb, d // LANES, n_cb, n // n_cb, LANES), BF16))
            out_specs.append(pl.BlockSpec((nb, d // LANES, n_cb, TQ // n_cb, LANES), lambda i: (0, 0, 0, i, 0)))
        elif len(out_shapes) in pair_major:
            out_shapes.append(jax.ShapeDtypeStruct((nb, d // LANES, n, LANES), BF16))
            out_specs.append(pl.BlockSpec((nb, d // LANES, TQ, LANES), lambda i: (0, 0, i, 0)))
        else:
            out_shapes.append(jax.ShapeDtypeStruct((nb, n, d), BF16))
            out_specs.append(pl.BlockSpec((nb, TQ, d), bat))
    rows_per_table = cos.shape[0] // TQ
    return pl.pallas_call(
        functools.partial(_inproj_kernel, kinds=kinds, groups=groups),
        grid=(n // TQ,),
        in_specs=[
            pl.BlockSpec((nb, TQ, d), bat),
            pl.BlockSpec(sh.shape, const3),
            pl.BlockSpec(sc.shape, const3),
            pl.BlockSpec(w.shape, const2, pipeline_mode=pl.Buffered(1)),
            pl.BlockSpec(b.shape, const2),
            pl.BlockSpec((TQ, LANES), lambda i: (i % rows_per_table, 0)),
            pl.BlockSpec((TQ, LANES), lambda i: (i % rows_per_table, 0)),
            pl.BlockSpec((TQ, LANES), lambda i: (i % rows_per_table, 0)),
        ],
        out_specs=out_specs,
        out_shape=out_shapes,
        compiler_params=_cparams(("arbitrary",)),
        name=name,
    )(x, sh, sc, w, b, cos, sa, sb)


def _rope_tables(n):
    t = np.arange(n)
    row = (t // GRID_W).astype(np.float32)
    col = (t % GRID_W).astype(np.float32)
    n_freq = HEAD_DIM // 4
    inv = (np.float32(ROPE_BASE) ** (-np.arange(n_freq, dtype=np.float32) / np.float32(n_freq))).astype(np.float32)
    ang_r = (row[:, None] * inv).astype(np.float32)
    ang_c = (col[:, None] * inv).astype(np.float32)
    zero = np.zeros_like(ang_r)
    cos_h = np.concatenate([np.cos(ang_r), np.cos(ang_r), np.cos(ang_c), np.cos(ang_c)], axis=1)
    sa_h = np.concatenate([zero, np.sin(ang_r), zero, np.sin(ang_c)], axis=1)
    sb_h = np.concatenate([-np.sin(ang_r), zero, -np.sin(ang_c), zero], axis=1)
    rep = LANES // HEAD_DIM
    return tuple(jnp.asarray(np.tile(a.astype(np.float32), (1, rep))) for a in (cos_h, sa_h, sb_h))


def _lru_kernel(*refs, tn, nt, reverse, fuse_out):
    if fuse_out:
        (xc_ref, wg_ref, bg_ref, lam_ref, h0_ref, hf_ref, gy_ref, out_ref, hfin_ref, a_s, b_s, h_s, hcar_s) = refs
    else:
        (xp_ref, xin_ref, xn_ref, cw_ref, cb_ref, wg_ref, bg_ref, lam_ref, h0_ref,
         out_ref, xc_ref, hfin_ref, a_s, b_s, h_s, hcar_s) = refs
    i = pl.program_id(0)
    ti = (nt - 1 - i) if reverse else i
    nb, d = hcar_s.shape

    @pl.when(i == 0)
    def _():
        hcar_s[...] = h0_ref[...]

    def tokens(ref):
        return ref[...].astype(F32).reshape(ref.shape[0] // nb, nb, d)

    if fuse_out:
        xc = tokens(xc_ref)
    else:
        x = tokens(xin_ref)
        xp = jnp.where(ti == 0, 0.0, tokens(xp_ref))
        xnx = jnp.where(ti == nt - 1, 0.0, tokens(xn_ref)[:1])
        ext = jnp.concatenate([xp, x, xnx], axis=0)
        xc = cb_ref[...]
        for k in range(CONV_W):
            xc = xc + ext[k:k + tn] * cw_ref[k]
        xc_ref[...] = xc.reshape(tn * nb, d).astype(BF16)
    rows = tn * nb
    xc2 = xc.reshape(rows, d)
    for s in range(d // LRU_SLAB):
        sl = slice(s * LRU_SLAB, (s + 1) * LRU_SLAB)
        xs = xc2[:, sl]
        half_gates = jnp.dot(xs.astype(BF16), wg_ref[s], preferred_element_type=F32) + bg_ref[s]
        t_r = jnp.tanh(half_gates[:, :LRU_SLAB])
        ig = 0.5 * jnp.tanh(half_gates[:, LRU_SLAB:]) + 0.5
        z = -lam_ref[s]
        softplus = jnp.maximum(z, 0.0) + jnp.log1p(jnp.exp(-jnp.abs(z)))
        k = (-0.5 * LRU_C * LOG2_E) * softplus
        a = jnp.exp2(t_r * k + k)
        b = jnp.sqrt(1.0 - a * a) * (ig * xs)
        a_s[:, :, sl] = a.reshape(tn, nb, LRU_SLAB)
        b_s[:, :, sl] = b.reshape(tn, nb, LRU_SLAB)

    unroll = 8

    def step(j, h):
        for k in range(unroll):
            jj = j * unroll + k
            t = (tn - 1 - jj) if reverse else jj
            h = a_s[t] * h + b_s[t]
            h_s[t] = h
        return h

    h = lax.fori_loop(0, tn // unroll, step, hcar_s[...])
    hcar_s[...] = h
    hfin_ref[...] = h
    if fuse_out:
        hs = jnp.swapaxes(tokens(hf_ref) + h_s[...], 0, 1)
        out_ref[...] = (hs * gy_ref[...].astype(F32)).astype(BF16)
    else:
        out_ref[...] = h_s[...].reshape(tn * nb, d).astype(BF16)


def _lru_call(x_t, wg, bg, lam, h0, conv=None, hf=None, gy=None, *, reverse, name):
    nb, d = h0.shape
    n = x_t.shape[0] // nb
    tn = LRU_TN
    nt = n // tn
    halo = 2
    fuse_out = conv is None
    tile = (lambda i: nt - 1 - i) if reverse else (lambda i: i)
    n_slab = d // LRU_SLAB
    tok_spec = pl.BlockSpec((tn * nb, d), lambda i: (tile(i), 0))
    gate_specs = [
        pl.BlockSpec((n_slab, LRU_SLAB, 2 * LRU_SLAB), lambda i: (0, 0, 0)),
        pl.BlockSpec((n_slab, 1, 2 * LRU_SLAB), lambda i: (0, 0, 0)),
        pl.BlockSpec((n_slab, 1, LRU_SLAB), lambda i: (0, 0, 0)),
        pl.BlockSpec((nb, d), lambda i: (0, 0)),
    ]
    fin_spec = pl.BlockSpec((nb, d), lambda i: (0, 0))
    fin_shape = jax.ShapeDtypeStruct((nb, d), F32)
    tok_shape = jax.ShapeDtypeStruct((n * nb, d), BF16)
    if fuse_out:
        in_specs = [tok_spec] + gate_specs + [tok_spec, pl.BlockSpec((nb, tn, d), lambda i: (0, tile(i), 0))]
        args = [x_t, wg, bg, lam, h0, hf, gy]
        out_specs = [pl.BlockSpec((nb, tn, d), lambda i: (0, tile(i), 0)), fin_spec]
        out_shape = [jax.ShapeDtypeStruct((nb, n, d), BF16), fin_shape]
    else:
        conv_w, conv_b = conv
        in_specs = [
            pl.BlockSpec((halo * nb, d), lambda i: (jnp.maximum(tile(i) * (tn // halo) - 1, 0), 0)),
            tok_spec,
            pl.BlockSpec((halo * nb, d), lambda i: (jnp.minimum((tile(i) + 1) * (tn // halo), n // halo - 1), 0)),
            pl.BlockSpec((CONV_W, 1, d), lambda i: (0, 0, 0)),
            pl.BlockSpec((1, d), lambda i: (0, 0)),
        ] + gate_specs
        args = [x_t, x_t, x_t, conv_w.reshape(CONV_W, 1, d), conv_b.reshape(1, d), wg, bg, lam, h0]
        out_specs = [tok_spec, tok_spec, fin_spec]
        out_shape = [tok_shape, tok_shape, fin_shape]
    return pl.pallas_call(
        functools.partial(_lru_kernel, tn=tn, nt=nt, reverse=reverse, fuse_out=fuse_out),
        grid=(nt,),
        in_specs=in_specs,
        out_specs=out_specs,
        out_shape=out_shape,
        scratch_shapes=[
            pltpu.VMEM((tn, nb, d), F32),
            pltpu.VMEM((tn, nb, d), F32),
            pltpu.VMEM((tn, nb, d), F32),
            pltpu.VMEM((nb, d), F32),
        ],
        compiler_params=_cparams(("arbitrary",)),
        name=name,
    )(*args)


def _lru_gate_weights(wa, ba, wi, bi, lam):
    n_blk, blk, _ = wa.shape
    d = n_blk * blk
    n_slab = d // LRU_SLAB
    per = LRU_SLAB // blk
    eye = jnp.eye(per, dtype=F32)

    def dense_slabs(w):
        return jnp.einsum("saij,ac->saicj", w.reshape(n_slab, per, blk, blk), eye).reshape(n_slab, LRU_SLAB, LRU_SLAB)

    wg = (0.5 * jnp.concatenate([dense_slabs(wa), dense_slabs(wi)], axis=2).astype(BF16)).astype(BF16)
    bg = 0.5 * jnp.concatenate([ba.reshape(n_slab, 1, LRU_SLAB), bi.reshape(n_slab, 1, LRU_SLAB)], axis=2)
    return wg, bg, lam.reshape(n_slab, 1, LRU_SLAB)


def _na_blocks(g):
    lo = min(max(8 * g - WIN_C // 2, 0) // WIN_C, GRID_W // WIN_C - 2)
    return (lo, lo + 1)


def _na_kernel(q_ref, k_ref, v_ref, kc_ref, vc_ref, tab_ref, o_ref, s_s, p_s, pc_s, inv_s, qs_s, sc_s, *, rows):
    win = WIN_R * GRID_W
    ch = NA_CHUNK
    n_chunk = rows // ch
    qh = 2 * GRID_W
    n_cb = GRID_W // WIN_C
    blk = WIN_R * WIN_C
    grp = SUBLANES
    lane = lax.broadcasted_iota(jnp.int32, (GRID_W, LANES), 1)
    first_head = lane < HEAD_DIM
    nt_dims = (((1,), (1,)), ((), ()))

    def row_start(r):
        return pl.multiple_of(r * GRID_W, GRID_W)

    def window(ref, rs):
        run = pl.ds(pl.multiple_of(rs * WIN_C, WIN_C), blk)
        return jnp.concatenate([ref[0, 0, cb, run, :] for cb in range(n_cb)], axis=0)

    def groups():
        for h in range(2):
            for g in range(GRID_W // grp):
                yield h, g * grp, _na_blocks(g)

    p_s[...] = jnp.zeros_like(p_s)

    def stack_heads(r, carry):
        q = q_ref[0, 0, pl.ds(row_start(r), GRID_W), :]
        zero = jnp.zeros_like(q)
        qs_s[r] = jnp.concatenate([jnp.where(first_head, q, zero), jnp.where(first_head, zero, q)], axis=0)
        return carry

    lax.fori_loop(0, rows, stack_heads, 0)
    sc_s[...] = lax.dot_general(qs_s[...].reshape(rows * qh, LANES), kc_ref[0, 0], nt_dims,
                                preferred_element_type=F32)

    def stage1(c, slot):
        for u in range(ch):
            r = c * ch + u
            rs = jnp.clip(r - WIN_R // 2, 0, rows - WIN_R)
            s = lax.dot_general(qs_s[r], window(k_ref, rs), nt_dims, preferred_element_type=F32)
            d0 = rs - r + (WIN_R - 1)
            for h, q0, cbs in groups():
                for cb in cbs:
                    rw = slice(h * GRID_W + q0, h * GRID_W + q0 + grp)
                    ln = slice(cb * blk, (cb + 1) * blk)
                    s_s[slot, u, rw, ln] = s[rw, ln] + tab_ref[0, h, d0, q0:q0 + grp, ln]

    def stage2(c, slot):
        for u in range(ch):
            base = pl.multiple_of((c * ch + u) * qh, qh)
            for r0 in range(0, qh, 2 * grp):
                parts = {}
                pcs, invs = [], []
                for half in range(2):
                    ra = r0 + half * grp
                    cbs = _na_blocks((ra % GRID_W) // grp)
                    sl = [s_s[slot, u, ra:ra + grp, cb * blk:(cb + 1) * blk] for cb in cbs]
                    sc = sc_s[pl.ds(base + ra, grp), :]
                    m = jnp.maximum(jnp.max(jnp.maximum(sl[0], sl[1]), axis=1, keepdims=True),
                                    jnp.max(sc, axis=1, keepdims=True))
                    ps = [jnp.exp(x - m) for x in sl]
                    pc = jnp.exp(sc - m)
                    den = jnp.sum(ps[0] + ps[1], axis=1, keepdims=True) + jnp.sum(pc, axis=1, keepdims=True)
                    for cb, pb in zip(cbs, ps):
                        parts[(half, cb)] = pb
                    pcs.append(pc)
                    invs.append(jnp.broadcast_to(1.0 / den, (grp, LANES)))
                zero = jnp.zeros((grp, blk), F32)
                for cb in sorted({cb for _, cb in parts}):
                    both = jnp.concatenate([parts.get((0, cb), zero), parts.get((1, cb), zero)], axis=0)
                    p_s[slot, u, r0:r0 + 2 * grp, cb * blk:(cb + 1) * blk] = both.astype(BF16)
                pc_s[slot, u * qh + r0:u * qh + r0 + 2 * grp, :] = jnp.concatenate(pcs, axis=0).astype(BF16)
                inv_s[slot, u, r0:r0 + 2 * grp, :] = jnp.concatenate(invs, axis=0)

    def stage3(c, slot):
        oc = jnp.dot(pc_s[slot], vc_ref[0, 0], preferred_element_type=F32)
        for u in range(ch):
            r = c * ch + u
            rs = jnp.clip(r - WIN_R // 2, 0, rows - WIN_R)
            o = (jnp.dot(p_s[slot, u], window(v_ref, rs), preferred_element_type=F32)
                 + oc[u * qh:(u + 1) * qh]) * inv_s[slot, u]
            out = jnp.where(first_head, o[:GRID_W], o[GRID_W:])
            o_ref[0, 0, pl.ds(row_start(r), GRID_W), :] = out.astype(BF16)

    stage1(0, 0)
    stage2(0, 0)
    stage1(1, 1)

    def body(j, carry):
        slot = j % 2
        stage3(j - 2, slot)
        stage2(j - 1, 1 - slot)
        stage1(j, slot)
        return carry

    lax.fori_loop(2, n_chunk, body, 0)
    last = n_chunk % 2
    stage3(n_chunk - 2, last)
    stage2(n_chunk - 1, 1 - last)
    stage3(n_chunk - 1, 1 - last)


def _na_call(q, k, v, kc, vc, tab):
    nb, n_pair, n, _ = q.shape
    l = kc.shape[2]
    rows = n // GRID_W
    n_cb = GRID_W // WIN_C
    assert k.shape == v.shape == (nb, n_pair, n_cb, n // n_cb, LANES)
    win = WIN_R * GRID_W
    qh = 2 * GRID_W
    assert rows % NA_CHUNK == 0 and rows // NA_CHUNK >= 2
    tok_spec = pl.BlockSpec((1, 1, n, LANES), lambda b, p: (b, p, 0, 0))
    kv_spec = pl.BlockSpec((1, 1, n_cb, n // n_cb, LANES), lambda b, p: (b, p, 0, 0, 0))
    ctx_spec = pl.BlockSpec((1, 1, l, LANES), lambda b, p: (b, p, 0, 0))
    return pl.pallas_call(
        functools.partial(_na_kernel, rows=rows),
        grid=(nb, n_pair),
        in_specs=[tok_spec, kv_spec, kv_spec, ctx_spec, ctx_spec,
                  pl.BlockSpec((1,) + tab.shape[1:], lambda b, p: (p, 0, 0, 0, 0))],
        out_specs=tok_spec,
        out_shape=jax.ShapeDtypeStruct((nb, n_pair, n, LANES), BF16),
        scratch_shapes=[
            pltpu.VMEM((2, NA_CHUNK, qh, win), F32),
            pltpu.VMEM((2, NA_CHUNK, qh, win), BF16),
            pltpu.VMEM((2, NA_CHUNK * qh, l), BF16),
            pltpu.VMEM((2, NA_CHUNK, qh, LANES), F32),
            pltpu.VMEM((rows, qh, LANES), BF16),
            pltpu.VMEM((rows * qh, l), F32),
        ],
        compiler_params=_cparams(("arbitrary", "arbitrary")),
        name="neighbourhood_attention",
    )(q, k, v, kc, vc, tab)


def _na_bias_table(rpb):
    qc = np.arange(GRID_W)
    kc = np.arange(GRID_W)
    n_cb = GRID_W // WIN_C
    cstart = np.clip(qc - WIN_C // 2, 0, GRID_W - WIN_C)
    ok = (kc[None, :] >= cstart[:, None]) & (kc[None, :] < cstart[:, None] + WIN_C)
    dc = np.clip(kc[None, :] - qc[:, None], -(WIN_C - 1), WIN_C - 1) + WIN_C - 1
    expand = (dc[None] == np.arange(2 * WIN_C - 1)[:, None, None]).astype(np.float32)
    expand = expand.reshape(2 * WIN_C - 1, GRID_W, n_cb, WIN_C)
    mask = np.where(ok, 0.0, NEG_INF).astype(np.float32).reshape(GRID_W, n_cb, 1, WIN_C)
    h = rpb.shape[0]
    rows = jnp.stack([rpb[:, d0:d0 + WIN_R].astype(F32) for d0 in range(WIN_R)], axis=1)
    tab = jnp.einsum("hdjx,xqbc->hdqbjc", rows, jnp.asarray(expand), precision=lax.Precision.HIGHEST)
    tab = tab + jnp.asarray(mask)
    return tab.reshape(h // 2, 2, WIN_R, GRID_W, WIN_R * GRID_W)


def _mix_kernel(yr_ref, yn_ref, gr_ref, gn_ref, x_ref, g1_ref, sh2_ref, sc2_ref,
                wpr_ref, wpn_ref, wo_ref, wr_ref, x1_ref, xn2_ref, lg_ref):
    nb, tq, d = x_ref.shape
    rows = nb * tq
    pr = jnp.dot(yr_ref[...].reshape(rows, d), wpr_ref[...], preferred_element_type=F32)
    yn = jnp.concatenate([yn_ref[:, p].reshape(rows, LANES) for p in range(d // LANES)], axis=1)
    pn = jnp.dot(yn, wpn_ref[...], preferred_element_type=F32)
    mix = gr_ref[...].reshape(rows, d).astype(F32) * pr + gn_ref[...].reshape(rows, d).astype(F32) * pn
    o = jnp.dot(mix.astype(BF16), wo_ref[...], preferred_element_type=F32)
    x1 = x_ref[...] + g1_ref[...] * o.reshape(nb, tq, d)
    x1_ref[...] = x1
    xb = (_rms(x1) * (1.0 + sc2_ref[...]) + sh2_ref[...]).astype(BF16)
    xn2_ref[...] = xb
    lg_ref[0] = lax.dot_general(wr_ref[...], xb.reshape(rows, d), (((1,), (1,)), ((), ())),
                                preferred_element_type=F32)


def _mix_call(yr, yn, gr, gn, x, g1, sh2, sc2, wpr, wpn, wo, wr):
    nb, n, d = x.shape
    bat = lambda i: (0, i, 0)
    const3 = lambda i: (0, 0, 0)
    const2 = lambda i: (0, 0)
    act = pl.BlockSpec((nb, TQ, d), bat)
    vec = pl.BlockSpec((nb, 1, d), const3)
    wsp = lambda w: pl.BlockSpec(w.shape, const2, pipeline_mode=pl.Buffered(1))
    return pl.pallas_call(
        _mix_kernel,
        grid=(n // TQ,),
        in_specs=[act, pl.BlockSpec((nb, d // LANES, TQ, LANES), lambda i: (0, 0, i, 0)), act, act, act,
                  vec, vec, vec, wsp(wpr), wsp(wpn), wsp(wo), wsp(wr)],
        out_specs=[act, act, pl.BlockSpec((1, wr.shape[0], nb * TQ), lambda i: (i, 0, 0))],
        out_shape=[
            jax.ShapeDtypeStruct((nb, n, d), F32),
            jax.ShapeDtypeStruct((nb, n, d), BF16),
            jax.ShapeDtypeStruct((n // TQ, wr.shape[0], nb * TQ), F32),
        ],
        compiler_params=_cparams(("arbitrary",)),
        name="merge_out_norm_router",
    )(yr, yn, gr, gn, x, g1, sh2, sc2, wpr, wpn, wo, wr)


def _route_kernel(lg_ref, rank_ref, aff_ref, cnt_ref, *, cap):
    lg = lg_ref[0]
    n_e, n = lg.shape
    ex = jnp.exp(lg - jnp.max(lg, axis=0, keepdims=True))
    aff = ex / jnp.sum(ex, axis=0, keepdims=True)
    aff_ref[0] = aff

    def as_float(bits):
        return lax.bitcast_convert_type(bits, F32)

    thr = jnp.zeros((n_e, 1), jnp.int32)
    for bit in range(30, -1, -1):
        cand = thr | (1 << bit)
        cnt = jnp.sum((aff >= as_float(cand)).astype(F32), axis=1, keepdims=True)
        thr = jnp.where(cnt >= cap, cand, thr)
    thr = jnp.where(thr < MIN_NORMAL_BITS, 0, thr)
    gt = aff >= as_float(jnp.where(thr == 0, MIN_NORMAL_BITS, thr + 1))
    eq = (aff >= as_float(thr)) & jnp.logical_not(gt)
    need = cap - jnp.sum(gt.astype(F32), axis=1, keepdims=True)

    blk = ROUTE_CHUNK
    row = lax.broadcasted_iota(jnp.int32, (blk, blk), 0)
    col = lax.broadcasted_iota(jnp.int32, (blk, blk), 1)
    upper = (row <= col).astype(BF16)

    def cumsum_tokens(mask):
        off = jnp.zeros((n_e, 1), F32)
        outs, offs = [], []
        for c in range(n // blk):
            x = mask[:, c * blk:(c + 1) * blk].astype(BF16)
            cs = jnp.dot(x, upper, preferred_element_type=F32) + off
            offs.append(off)
            outs.append(cs)
            off = cs[:, blk - 1:blk]
        offs.append(off)
        return jnp.concatenate(outs, axis=1), offs

    cum_eq, _ = cumsum_tokens(eq)
    sel = gt | (eq & ((cum_eq - eq.astype(F32)) < need))
    cum_sel, offs = cumsum_tokens(sel)
    rank_ref[0] = jnp.where(sel, cum_sel - 1.0, -1.0).astype(jnp.int32)
    pad = jnp.zeros((n_e, LANES - len(offs)), F32)
    cnt_ref[0] = jnp.concatenate(offs + [pad], axis=1).astype(jnp.int32)


def _route_call(lg_t, cap):
    nb, n_e, n = lg_t.shape
    blk3 = lambda b: (b, 0, 0)
    return pl.pallas_call(
        functools.partial(_route_kernel, cap=cap),
        grid=(nb,),
        in_specs=[pl.BlockSpec((1, n_e, n), blk3)],
        out_specs=[pl.BlockSpec((1, n_e, n), blk3), pl.BlockSpec((1, n_e, n), blk3),
                   pl.BlockSpec((1, n_e, LANES), blk3)],
        out_shape=[jax.ShapeDtypeStruct((nb, n_e, n), jnp.int32), jax.ShapeDtypeStruct((nb, n_e, n), F32),
                   jax.ShapeDtypeStruct((nb, n_e, LANES), jnp.int32)],
        compiler_params=_cparams(("arbitrary",)),
        name="route_select",
    )(lg_t)


def _slot_windows(cnt_ref, b, c, n_e, n_chunk, cap):
    wins = []
    for e in range(n_e):
        base = (b * n_e + e) * (n_chunk + 1) + c
        lo = cnt_ref[base]
        hi = cnt_ref[base + 1]
        lo_al = jnp.minimum((lo // SLOT_ALIGN) * SLOT_ALIGN, cap - ROUTE_WIN)
        extra = jnp.maximum((hi - lo_al + ROUTE_WIN - 1) // ROUTE_WIN - 1, 0)
        wins.append((pl.multiple_of(lo_al, SLOT_ALIGN), extra))
    return wins


def _one_hot_t(rank_row, start, first=None):
    slot = start + lax.broadcasted_iota(jnp.int32, (ROUTE_WIN, rank_row.shape[1]), 0)
    hit = rank_row == slot
    if first is not None:
        hit = hit & (slot >= first)
    return hit


def _spill_window(k, lo_al, cap):
    first = lo_al + k * ROUTE_WIN
    start = pl.multiple_of(jnp.minimum(first, cap - ROUTE_WIN), SLOT_ALIGN)
    return start, first


def _dispatch_kernel(cnt_ref, x_ref, rank_ref, aff_ref, xe_ref, g_ref, *, cap, n_chunk):
    b = pl.program_id(0)
    step = pl.program_id(1)
    n_e = rank_ref.shape[1]

    @pl.when(step == 0)
    def _():
        xe_ref[...] = jnp.zeros_like(xe_ref)
        g_ref[...] = jnp.zeros_like(g_ref)

    for sub in range(ROUTE_SUB):
        toks = slice(sub * ROUTE_CHUNK, (sub + 1) * ROUTE_CHUNK)
        c = step * ROUTE_SUB + sub
        x = x_ref[0, toks, :]
        wins = _slot_windows(cnt_ref, b, c, n_e, n_chunk, cap)
        hots = [_one_hot_t(rank_ref[0, e:e + 1, toks], wins[e][0]) for e in range(n_e)]
        stacked = jnp.concatenate([h.astype(BF16) for h in hots], axis=0)
        rows = jnp.dot(stacked, x, preferred_element_type=F32)
        for e in range(n_e):
            sl = pl.ds(wins[e][0], ROUTE_WIN)
            xe_ref[e, sl, :] = xe_ref[e, sl, :] + rows[e * ROUTE_WIN:(e + 1) * ROUTE_WIN].astype(BF16)
            g_ref[e, sl, :] = g_ref[e, sl, :] + jnp.sum(jnp.where(hots[e], aff_ref[0, e:e + 1, toks], 0.0),
                                                        axis=1, keepdims=True)

        @pl.when(sum(extra for _, extra in wins) > 0)
        def _(wins=wins, toks=toks):
            for e in range(n_e):
                lo_al, extra = wins[e]

                def spill(k, carry, e=e, lo_al=lo_al):
                    start, first = _spill_window(k, lo_al, cap)
                    hot = _one_hot_t(rank_ref[0, e:e + 1, toks], start, first)
                    sl2 = pl.ds(start, ROUTE_WIN)
                    xe_ref[e, sl2, :] = xe_ref[e, sl2, :] + jnp.dot(hot.astype(BF16), x_ref[0, toks, :],
                                                                   preferred_element_type=F32).astype(BF16)
                    g_ref[e, sl2, :] = g_ref[e, sl2, :] + jnp.sum(jnp.where(hot, aff_ref[0, e:e + 1, toks], 0.0),
                                                                  axis=1, keepdims=True)
                    return carry

                lax.fori_loop(1, extra + 1, spill, 0)


def _dispatch_call(cnt_flat, xn2, rank_t, aff_t, cap):
    nb, n, d = xn2.shape
    n_e = rank_t.shape[1]
    n_chunk = n // ROUTE_CHUNK
    grid_spec = pltpu.PrefetchScalarGridSpec(
        num_scalar_prefetch=1,
        grid=(nb, n_chunk // ROUTE_SUB),
        in_specs=[
            pl.BlockSpec((1, ROUTE_SUB * ROUTE_CHUNK, d), lambda b, c, cnt: (b, c, 0)),
            pl.BlockSpec((1, n_e, ROUTE_SUB * ROUTE_CHUNK), lambda b, c, cnt: (b, 0, c)),
            pl.BlockSpec((1, n_e, ROUTE_SUB * ROUTE_CHUNK), lambda b, c, cnt: (b, 0, c)),
        ],
        out_specs=[
            pl.BlockSpec((n_e, cap, d), lambda b, c, cnt: (0, b, 0)),
            pl.BlockSpec((n_e, cap, 1), lambda b, c, cnt: (0, b, 0)),
        ],
    )
    return pl.pallas_call(
        functools.partial(_dispatch_kernel, cap=cap, n_chunk=n_chunk),
        grid_spec=grid_spec,
        out_shape=[jax.ShapeDtypeStruct((n_e, nb * cap, d), BF16), jax.ShapeDtypeStruct((n_e, nb * cap, 1), F32)],
        compiler_params=_cparams(("arbitrary", "arbitrary")),
        name="moe_dispatch",
    )(cnt_flat, xn2, rank_t, aff_t)


def _ffn_kernel(xe_ref, g_ref, wg_ref, wu_ref, wd_ref, o_ref, acc_s):
    f = pl.program_id(2)
    last = pl.num_programs(2) - 1
    d = o_ref.shape[2]
    xe = xe_ref[0]
    h1 = jnp.dot(xe, wg_ref[0].astype(BF16), preferred_element_type=F32)
    h2 = jnp.dot(xe, wu_ref[0].astype(BF16), preferred_element_type=F32)
    hid = (h1 * _sigmoid(h1) * h2).astype(BF16)
    wd = wd_ref[0].astype(BF16)

    def chunks():
        for c in range(d // FFN_TN):
            cols = slice(c * FFN_TN, (c + 1) * FFN_TN)
            yield cols, jnp.dot(hid, wd[:, cols], preferred_element_type=F32)

    @pl.when(f == 0)
    def _():
        for cols, part in chunks():
            acc_s[:, cols] = part

    @pl.when((f > 0) & (f < last))
    def _():
        for cols, part in chunks():
            acc_s[:, cols] += part

    @pl.when(f == last)
    def _():
        for cols, part in chunks():
            o_ref[0, :, cols] = ((acc_s[:, cols] + part) * g_ref[0]).astype(BF16)


def _ffn_call(xe, g, w_gate, w_up, w_down):
    e, m, d = xe.shape
    dff = w_gate.shape[2]
    tm = min(FFN_TM, m)
    assert dff // FFN_TF >= 2 and m % tm == 0 and d % FFN_TN == 0
    return pl.pallas_call(
        _ffn_kernel,
        grid=(e, m // tm, dff // FFN_TF),
        in_specs=[
            pl.BlockSpec((1, tm, d), lambda ei, mi, fi: (ei, mi, 0)),
            pl.BlockSpec((1, tm, 1), lambda ei, mi, fi: (ei, mi, 0)),
            pl.BlockSpec((1, d, FFN_TF), lambda ei, mi, fi: (ei, 0, fi)),
            pl.BlockSpec((1, d, FFN_TF), lambda ei, mi, fi: (ei, 0, fi)),
            pl.BlockSpec((1, FFN_TF, d), lambda ei, mi, fi: (ei, fi, 0)),
        ],
        out_specs=pl.BlockSpec((1, tm, d), lambda ei, mi, fi: (ei, mi, 0)),
        out_shape=jax.ShapeDtypeStruct((e, m, d), BF16),
        scratch_shapes=[pltpu.VMEM((tm, d), F32)],
        compiler_params=_cparams(("arbitrary", "arbitrary", "arbitrary")),
        name="expert_ffn",
    )(xe, g, w_gate, w_up, w_down)


def _combine_kernel(cnt_ref, ye_ref, rank_ref, x1_ref, g2_ref, fn_ref, o_ref, acc_s, *, cap, n_chunk):
    b = pl.program_id(0)
    step = pl.program_id(1)
    n_e = rank_ref.shape[1]
    tn_dims = (((0,), (0,)), ((), ()))

    for sub in range(ROUTE_SUB):
        toks = slice(sub * ROUTE_CHUNK, (sub + 1) * ROUTE_CHUNK)
        c = step * ROUTE_SUB + sub
        wins = _slot_windows(cnt_ref, b, c, n_e, n_chunk, cap)
        hots = [_one_hot_t(rank_ref[0, e:e + 1, toks], wins[e][0]).astype(BF16) for e in range(n_e)]
        ys = [ye_ref[e, pl.ds(wins[e][0], ROUTE_WIN), :] for e in range(n_e)]
        moe = lax.dot_general(jnp.concatenate(hots, axis=0), jnp.concatenate(ys, axis=0), tn_dims,
                              preferred_element_type=F32)

        def finish(m, toks=toks):
            x = x1_ref[0, toks, :] + g2_ref[0] * m
            o_ref[0, toks, :] = _rms(x) * fn_ref[0]

        any_spill = sum(extra for _, extra in wins) > 0

        @pl.when(jnp.logical_not(any_spill))
        def _(moe=moe, finish=finish):
            finish(moe)

        @pl.when(any_spill)
        def _(moe=moe, finish=finish, wins=wins, toks=toks):
            acc_s[...] = moe
            for e in range(n_e):
                lo_al, extra = wins[e]

                def spill(k, carry, e=e, lo_al=lo_al):
                    start, first = _spill_window(k, lo_al, cap)
                    hot = _one_hot_t(rank_ref[0, e:e + 1, toks], start, first).astype(BF16)
                    acc_s[...] += lax.dot_general(hot, ye_ref[e, pl.ds(start, ROUTE_WIN), :], tn_dims,
                                                  preferred_element_type=F32)
                    return carry

                lax.fori_loop(1, extra + 1, spill, 0)
            finish(acc_s[...])


def _combine_call(cnt_flat, ye, rank_t, x1, g2, fnorm, cap):
    nb, n, d = x1.shape
    n_e = rank_t.shape[1]
    n_chunk = n // ROUTE_CHUNK
    grid_spec = pltpu.PrefetchScalarGridSpec(
        num_scalar_prefetch=1,
        grid=(nb, n_chunk // ROUTE_SUB),
        in_specs=[
            pl.BlockSpec((n_e, cap, d), lambda b, c, cnt: (0, b, 0)),
            pl.BlockSpec((1, n_e, ROUTE_SUB * ROUTE_CHUNK), lambda b, c, cnt: (b, 0, c)),
            pl.BlockSpec((1, ROUTE_SUB * ROUTE_CHUNK, d), lambda b, c, cnt: (b, c, 0)),
            pl.BlockSpec((1, 1, d), lambda b, c, cnt: (b, 0, 0)),
            pl.BlockSpec((1, 1, d), lambda b, c, cnt: (0, 0, 0)),
        ],
        out_specs=pl.BlockSpec((1, ROUTE_SUB * ROUTE_CHUNK, d), lambda b, c, cnt: (b, c, 0)),
        scratch_shapes=[pltpu.VMEM((ROUTE_CHUNK, d), F32)],
    )
    return pl.pallas_call(
        functools.partial(_combine_kernel, cap=cap, n_chunk=n_chunk),
        grid_spec=grid_spec,
        out_shape=jax.ShapeDtypeStruct((nb, n, d), F32),
        compiler_params=_cparams(("arbitrary", "arbitrary")),
        name="moe_combine_final_norm",
    )(cnt_flat, ye, rank_t, x1, g2, fnorm.reshape(1, 1, d))


def kernel(x, c, ctx, c_ctx, w_mod, b_mod, w_in, b_in, conv_w, conv_b, lru_wa, lru_ba, lru_wi, lru_bi,
           lru_lambda, na_rpb, w_proj_rnn, w_proj_na, w_out, w_router, w_exp_gate, w_exp_up, w_exp_down,
           final_norm):
    nb, n, d = x.shape
    l = ctx.shape[1]
    assert w_mod.shape[0] == 1, "single-layer problem"
    assert nb == SUBLANES and n % LRU_TN == 0 and l % LRU_TN == 0 and d % LRU_SLAB == 0
    lyr = 0

    cc = jnp.concatenate([c, c_ctx[None], jnp.zeros((2 * SUBLANES - nb - 1, d), F32)], axis=0)
    mod = _mod_call(cc, w_mod[lyr], b_mod[lyr])
    sh1, sc1, g1, sh2, sc2, g2 = [m[:nb, None, :] for m in jnp.split(mod, 6, axis=-1)]
    csh1, csc1 = [jnp.broadcast_to(m[nb:nb + 1, None, :], (nb, 1, d)) for m in jnp.split(mod, 6, axis=-1)[:2]]

    w_in_b = w_in[lyr].astype(BF16)
    b_in_r = b_in[lyr].reshape(1, -1)
    tables = _rope_tables(n)
    gy, xr_t, q, k, v, sgr, sgn = _inproj_call(
        x, sh1, sc1, w_in_b, b_in_r, tables,
        ("gelu", "tok_major", "rope_q", "rope_k", "bf16", "sigmoid", "sigmoid"), tuple(range(7)), "in_proj_latent",
        pair_major=(2,), col_blocked=(3, 4))
    xrc_t, kc, vc = _inproj_call(
        ctx, csh1, csc1, w_in_b, b_in_r, tables, ("tok_major", "bf16", "bf16"), (1, 3, 4), "in_proj_context",
        pair_major=(1, 2))

    gw = [_lru_gate_weights(lru_wa[lyr, dr], lru_ba[lyr, dr], lru_wi[lyr, dr], lru_bi[lyr, dr], lru_lambda[lyr, dr])
          for dr in range(2)]
    zeros_h = jnp.zeros((nb, d), F32)
    conv = (conv_w[lyr], conv_b[lyr])
    _, xcc_t, hc_f = _lru_call(xrc_t, *gw[0], zeros_h, conv=conv, reverse=False, name="rglru_ctx_fwd")
    _, hc_b = _lru_call(xcc_t, *gw[1], zeros_h, hf=xcc_t, gy=jnp.zeros((nb, l, d), BF16), reverse=True,
                        name="rglru_ctx_bwd")
    hf_t, xc_t, _ = _lru_call(xr_t, *gw[0], hc_f, conv=conv, reverse=False, name="rglru_fwd")
    y_rnn, _ = _lru_call(xc_t, *gw[1], hc_b, hf=hf_t, gy=gy, reverse=True, name="rglru_bwd")

    y_na = _na_call(q, k, v, kc, vc, _na_bias_table(na_rpb[lyr]))

    x1, xn2, lg_tiles = _mix_call(y_rnn, y_na, sgr, sgn, x, g1, sh2, sc2,
                                  w_proj_rnn[lyr].astype(BF16), w_proj_na[lyr].astype(BF16),
                                  w_out[lyr].astype(BF16), w_router[lyr].T.astype(BF16))
    n_e = w_router.shape[2]
    lg_t = lg_tiles.reshape(n // TQ, n_e, nb, TQ).transpose(2, 1, 0, 3).reshape(nb, n_e, n)

    cap = EC_CAPACITY * n // N_EXPERTS
    n_chunk = n // ROUTE_CHUNK
    assert cap % SLOT_ALIGN == 0 and cap >= ROUTE_WIN and n % (ROUTE_SUB * ROUTE_CHUNK) == 0
    rank_t, aff_t, cnt = _route_call(lg_t, cap)
    cnt_flat = cnt[:, :, :n_chunk + 1].reshape(-1)
    xe, ge = _dispatch_call(cnt_flat, xn2, rank_t, aff_t, cap)
    ye = _ffn_call(xe, ge, w_exp_gate[lyr], w_exp_up[lyr], w_exp_down[lyr])
    return _combine_call(cnt_flat, ye, rank_t, x1, g2, final_norm, cap)
```
